```python
import math
import jax, jax.numpy as jnp
from jax import lax
import numpy as np

D_MODEL = 2048
BATCH = 1
SEQ = 16384
DEPTH = 1

D_MIX = D_MODEL
HEAD_DIM = 64
ATTN_WIDTH = D_MIX // 2
N_Q_HEADS = ATTN_WIDTH // HEAD_DIM
N_KV_HEADS = 4
Q_PER_KV = N_Q_HEADS // N_KV_HEADS
KV_WIDTH = N_KV_HEADS * HEAD_DIM
WINDOW = 128
BLOCK = 128
ROPE_THETA = 10000.0
SSM_WIDTH = D_MIX - ATTN_WIDTH
SSM_GROUP = 16
SSM_GROUPS = SSM_WIDTH // SSM_GROUP
SSM_STATE = 64
IN_COLS = ATTN_WIDTH + 2 * KV_WIDTH + SSM_WIDTH
D_FF = 5632
FFN_RESIDUAL = 0.5
EPS = 1e-6

kernel_name = 'hybrid_swa_s5_macaron'


def rms_norm(x, gain):
    x32 = x.astype(jnp.float32)
    y = x32 * lax.rsqrt(jnp.mean(x32 * x32, axis=-1, keepdims=True) + EPS)
    return (y * gain.astype(jnp.float32)).astype(x.dtype)


def swiglu_ffn(h, w_gate, w_up, w_down):
    g = jnp.einsum('bld,df->blf', h, w_gate)
    u = jnp.einsum('bld,df->blf', h, w_up)
    return jnp.einsum('blf,fd->bld', jax.nn.silu(g) * u, w_down)


def rotary(t, positions):
    half = HEAD_DIM // 2
    inv_freq = ROPE_THETA ** (-jnp.arange(half, dtype=jnp.float32) * 2.0 / HEAD_DIM)
    ang = positions.astype(jnp.float32)[..., None] * inv_freq
    cos = jnp.cos(ang)[:, :, None, :]
    sin = jnp.sin(ang)[:, :, None, :]
    t32 = t.astype(jnp.float32)
    t1, t2 = t32[..., :half], t32[..., half:]
    out = jnp.concatenate([t1 * cos - t2 * sin, t2 * cos + t1 * sin], axis=-1)
    return out.astype(t.dtype)


def sliding_window_attention(q, k, v, sinks):
    b, l = q.shape[0], q.shape[1]
    nb = l // BLOCK
    qb = q.reshape(b, nb, BLOCK, N_KV_HEADS, Q_PER_KV, HEAD_DIM)

    def band(t):
        tp = jnp.pad(t, ((0, 0), (BLOCK, 0), (0, 0), (0, 0)))
        prev = tp[:, :l].reshape(b, nb, BLOCK, N_KV_HEADS, HEAD_DIM)
        cur = t.reshape(b, nb, BLOCK, N_KV_HEADS, HEAD_DIM)
        return jnp.concatenate([prev, cur], axis=2)

    kb, vb = band(k), band(v)
    s = jnp.einsum('bnqhgd,bnkhd->bnhgqk', qb, kb,
                   preferred_element_type=jnp.float32).astype(jnp.float32) * (1.0 / math.sqrt(HEAD_DIM))
    qi = jnp.arange(BLOCK)[:, None]
    kj = jnp.arange(2 * BLOCK)[None, :]
    diff = qi + BLOCK - kj
    in_window = (diff >= 0) & (diff < WINDOW)
    blk = jnp.arange(nb)[:, None, None]
    key_exists = (blk * BLOCK - BLOCK + kj[None]) >= 0
    mask = in_window[None] & key_exists
    s = jnp.where(mask[None, :, None, None], s, -jnp.inf)
    sink = sinks.astype(jnp.float32).reshape(N_KV_HEADS, Q_PER_KV)[None, None, :, :, None, None]
    m = jnp.maximum(jnp.max(s, axis=-1, keepdims=True), sink)
    p = jnp.exp(s - m)
    p = p / (jnp.sum(p, axis=-1, keepdims=True) + jnp.exp(sink - m))
    o = jnp.einsum('bnhgqk,bnkhd->bnqhgd', p, vb.astype(jnp.float32))
    return o.reshape(b, l, N_Q_HEADS * HEAD_DIM).astype(q.dtype)


def _ssm_combine(earlier, later):
    a1r, a1i, b1r, b1i = earlier
    a2r, a2i, b2r, b2i = later
    return (a2r * a1r - a2i * a1i,
            a2r * a1i + a2i * a1r,
            a2r * b1r - a2i * b1i + b2r,
            a2r * b1i + a2i * b1r + b2i)


def s5_ssm(u, log_dt, a_re, a_im, b_re, b_im, c_re, c_im, d_skip, w_glu, b_glu):
    bsz, l = u.shape[0], u.shape[1]
    u32 = u.astype(jnp.float32).reshape(bsz, l, SSM_GROUPS, SSM_GROUP)
    dt = jnp.exp(log_dt.astype(jnp.float32))[:, None]
    a_re = a_re.astype(jnp.float32)
    a_im = a_im.astype(jnp.float32)
    mag = jnp.exp(a_re * dt)
    ab_re = mag * jnp.cos(a_im * dt)
    ab_im = mag * jnp.sin(a_im * dt)
    inv_den = 1.0 / (a_re * a_re + a_im * a_im)
    nr = ab_re - 1.0
    coef_re = (nr * a_re + ab_im * a_im) * inv_den
    coef_im = (ab_im * a_re - nr * a_im) * inv_den
    b_re = b_re.astype(jnp.float32)
    b_im = b_im.astype(jnp.float32)
    bb_re = coef_re[..., None] * b_re - coef_im[..., None] * b_im
    bb_im = coef_re[..., None] * b_im + coef_im[..., None] * b_re
    bu_re = jnp.einsum('blgh,gph->blgp', u32, bb_re)
    bu_im = jnp.einsum('blgh,gph->blgp', u32, bb_im)
    a_seq_re = jnp.broadcast_to(ab_re, bu_re.shape)
    a_seq_im = jnp.broadcast_to(ab_im, bu_im.shape)
    _, _, s_re, s_im = lax.associative_scan(_ssm_combine, (a_seq_re, a_seq_im, bu_re, bu_im), axis=1)
    y = (jnp.einsum('blgp,ghp->blgh', s_re, c_re.astype(jnp.float32))
         - jnp.einsum('blgp,ghp->blgh', s_im, c_im.astype(jnp.float32))
         + d_skip.astype(jnp.float32) * u32)
    z = jax.nn.gelu(y)
    g = jnp.einsum('blgh,ghk->blgk', z, w_glu.astype(jnp.float32)) + b_glu.astype(jnp.float32)
    out = g[..., :SSM_GROUP] * jax.nn.sigmoid(g[..., SSM_GROUP:])
    return out.reshape(bsz, l, SSM_WIDTH).astype(u.dtype)


def setup_inputs(seed: int = 0) -> dict:
    key = jax.random.key(seed)
    keys = iter(jax.random.split(key, 40))

    def normal(shape, scale):
        return jax.random.normal(next(keys), shape, dtype=jnp.float32) * scale

    def gain(n):
        return 1.0 + normal((DEPTH, n), 0.02)

    x = normal((BATCH, SEQ, D_MODEL), 1.0)
    positions = jnp.broadcast_to(jnp.arange(SEQ, dtype=jnp.int32), (BATCH, SEQ))
    inp = {'x': x, 'positions': positions}
    inp['ffn1_norm'] = gain(D_MODEL)
    inp['ffn1_w_gate'] = normal((DEPTH, D_MODEL, D_FF), D_MODEL ** -0.5)
    inp['ffn1_w_up'] = normal((DEPTH, D_MODEL, D_FF), D_MODEL ** -0.5)
    inp['ffn1_w_down'] = normal((DEPTH, D_FF, D_MODEL), D_FF ** -0.5)
    inp['mix_norm'] = gain(D_MODEL)
    inp['w_in'] = normal((DEPTH, D_MODEL, IN_COLS), D_MODEL ** -0.5)
    inp['q_norm'] = gain(HEAD_DIM)
    inp['k_norm'] = gain(HEAD_DIM)
    inp['attn_sinks'] = normal((DEPTH, N_Q_HEADS), 0.5)
    inp['ssm_log_dt'] = jax.random.uniform(next(keys), (DEPTH, SSM_GROUPS), jnp.float32,
                                           math.log(1e-3), math.log(1e-1))
    inp['ssm_a_re'] = -0.5 + normal((DEPTH, SSM_GROUPS, SSM_STATE), 0.01)
    inp['ssm_a_im'] = jnp.broadcast_to(jnp.pi * jnp.arange(SSM_STATE, dtype=jnp.float32),
                                       (DEPTH, SSM_GROUPS, SSM_STATE))
    inp['ssm_b_re'] = normal((DEPTH, SSM_GROUPS, SSM_STATE, SSM_GROUP), (2 * SSM_GROUP) ** -0.5)
    inp['ssm_b_im'] = normal((DEPTH, SSM_GROUPS, SSM_STATE, SSM_GROUP), (2 * SSM_GROUP) ** -0.5)
    inp['ssm_c_re'] = normal((DEPTH, SSM_GROUPS, SSM_GROUP, SSM_STATE), (2 * SSM_STATE) ** -0.5)
    inp['ssm_c_im'] = normal((DEPTH, SSM_GROUPS, SSM_GROUP, SSM_STATE), (2 * SSM_STATE) ** -0.5)
    inp['ssm_d'] = normal((DEPTH, SSM_GROUPS, SSM_GROUP), 1.0)
    inp['ssm_w_glu'] = normal((DEPTH, SSM_GROUPS, SSM_GROUP, 2 * SSM_GROUP), SSM_GROUP ** -0.5)
    inp['ssm_b_glu'] = normal((DEPTH, SSM_GROUPS, 2 * SSM_GROUP), 0.01)
    inp['attn_out_norm'] = gain(ATTN_WIDTH)
    inp['ssm_out_norm'] = gain(SSM_WIDTH)
    inp['w_out'] = normal((DEPTH, D_MIX, D_MODEL), D_MIX ** -0.5)
    inp['ffn2_norm'] = gain(D_MODEL)
    inp['ffn2_w_gate'] = normal((DEPTH, D_MODEL, D_FF), D_MODEL ** -0.5)
    inp['ffn2_w_up'] = normal((DEPTH, D_MODEL, D_FF), D_MODEL ** -0.5)
    inp['ffn2_w_down'] = normal((DEPTH, D_FF, D_MODEL), D_FF ** -0.5)
    return inp


def reference(x, positions, ffn1_norm, ffn1_w_gate, ffn1_w_up, ffn1_w_down,
              mix_norm, w_in, q_norm, k_norm, attn_sinks,
              ssm_log_dt, ssm_a_re, ssm_a_im, ssm_b_re, ssm_b_im, ssm_c_re, ssm_c_im,
              ssm_d, ssm_w_glu, ssm_b_glu, attn_out_norm, ssm_out_norm, w_out,
              ffn2_norm, ffn2_w_gate, ffn2_w_up, ffn2_w_down):
    b, l = x.shape[0], x.shape[1]
    for i in range(DEPTH):
        h = rms_norm(x, ffn1_norm[i])
        x = x + FFN_RESIDUAL * swiglu_ffn(h, ffn1_w_gate[i], ffn1_w_up[i], ffn1_w_down[i])
        h = rms_norm(x, mix_norm[i])
        proj = jnp.einsum('bld,dc->blc', h, w_in[i])
        q = proj[..., :ATTN_WIDTH].reshape(b, l, N_Q_HEADS, HEAD_DIM)
        k = proj[..., ATTN_WIDTH:ATTN_WIDTH + KV_WIDTH].reshape(b, l, N_KV_HEADS, HEAD_DIM)
        v = proj[..., ATTN_WIDTH + KV_WIDTH:ATTN_WIDTH + 2 * KV_WIDTH].reshape(b, l, N_KV_HEADS, HEAD_DIM)
        u = proj[..., ATTN_WIDTH + 2 * KV_WIDTH:]
        q = rotary(rms_norm(q, q_norm[i]), positions)
        k = rotary(rms_norm(k, k_norm[i]), positions)
        attn = sliding_window_attention(q, k, v, attn_sinks[i])
        ssm = s5_ssm(u, ssm_log_dt[i], ssm_a_re[i], ssm_a_im[i], ssm_b_re[i], ssm_b_im[i],
                     ssm_c_re[i], ssm_c_im[i], ssm_d[i], ssm_w_glu[i], ssm_b_glu[i])
        mixed = jnp.concatenate([rms_norm(attn, attn_out_norm[i]),
                                 rms_norm(ssm, ssm_out_norm[i])], axis=-1)
        x = x + jnp.einsum('blc,cd->bld', mixed, w_out[i])
        h = rms_norm(x, ffn2_norm[i])
        x = x + FFN_RESIDUAL * swiglu_ffn(h, ffn2_w_gate[i], ffn2_w_up[i], ffn2_w_down[i])
    return x
```

```python
import functools
import math

import jax
import jax.numpy as jnp
from jax import lax
from jax.experimental import pallas as pl
from jax.experimental.pallas import tpu as pltpu

HEAD_DIM = 64
N_Q_HEADS = 16
N_KV_HEADS = 4
ATTN_WIDTH = N_Q_HEADS * HEAD_DIM
KV_WIDTH = N_KV_HEADS * HEAD_DIM
BLOCK = 128
ROPE_THETA = 10000.0
SSM_GROUP = 16
SSM_GROUPS = 64
SSM_STATE = 64
SSM_WIDTH = SSM_GROUP * SSM_GROUPS
FFN_RESIDUAL = 0.5
EPS = 1e-6

LANES = 128
SUBLANES = 8
SSM_CHUNK = 16
SSM_SEGMENTS = SUBLANES
SSM_GROUPS_PER_STEP = 8
VMEM_LIMIT = 56 * 1024 * 1024

_BF16 = jnp.bfloat16
_F32 = jnp.float32
_HI = lax.Precision.HIGHEST


def _rms(x, gain):
    ms = jnp.mean(x * x, axis=-1, keepdims=True)
    return x * lax.rsqrt(ms + EPS) * gain


def _ffn_body(x_ref, gain_ref, wg_ref, wu_ref, wd_ref, o_ref, h_ref, acc_ref):
    j = pl.program_id(1)
    last = pl.num_programs(1) - 1

    @pl.when(j == 0)
    def _():
        h_ref[...] = _rms(x_ref[...], gain_ref[...]).astype(_BF16)

    h = h_ref[...]
    g = jnp.dot(h, wg_ref[...], preferred_element_type=_F32)
    u = jnp.dot(h, wu_ref[...], preferred_element_type=_F32)
    a = (g * jax.nn.sigmoid(g) * u).astype(_BF16)
    d = jnp.dot(a, wd_ref[...], preferred_element_type=_F32)

    @pl.when(j == 0)
    def _():
        acc_ref[...] = d

    @pl.when(j > 0)
    def _():
        acc_ref[...] += d

    @pl.when(j == last)
    def _():
        o_ref[...] = x_ref[...] + FFN_RESIDUAL * acc_ref[...]


def _ffn(x, gain, wg, wu, wd, *, tm=512, tf=512):
    L, D = x.shape
    F = wg.shape[1]
    return pl.pallas_call(
        _ffn_body,
        grid=(L // tm, F // tf),
        in_specs=[
            pl.BlockSpec((tm, D), lambda i, j: (i, 0)),
            pl.BlockSpec((1, D), lambda i, j: (0, 0)),
            pl.BlockSpec((D, tf), lambda i, j: (0, j)),
            pl.BlockSpec((D, tf), lambda i, j: (0, j)),
            pl.BlockSpec((tf, D), lambda i, j: (j, 0)),
        ],
        out_specs=pl.BlockSpec((tm, D), lambda i, j: (i, 0)),
        out_shape=jax.ShapeDtypeStruct((L, D), _F32),
        scratch_shapes=[pltpu.VMEM((tm, D), _BF16), pltpu.VMEM((tm, D), _F32)],
        compiler_params=pltpu.CompilerParams(
            dimension_semantics=("arbitrary", "arbitrary"), vmem_limit_bytes=VMEM_LIMIT),
        name="ffn",
    )(x, gain.reshape(1, D), wg, wu, wd)


def _proj_body(x_ref, gain_ref, w_ref, pos_ref, invf_ref, qg_ref, kg_ref, pn_ref,
               q_ref, k_ref, v_ref, u_ref):
    tm = x_ref.shape[0]
    h = _rms(x_ref[...], gain_ref[...]).astype(_BF16)
    proj = jnp.dot(h, w_ref[...], preferred_element_type=_F32)

    ang = pos_ref[...] * invf_ref[...]
    cos = jnp.cos(ang)
    sin = jnp.sin(ang)
    lane = lax.broadcasted_iota(jnp.int32, (tm, LANES), 1)
    first_half = (lane & (HEAD_DIM // 2)) == 0
    low_head = lane < HEAD_DIM
    sin_signed = jnp.where(first_half, -sin, sin)

    def norm_rotary(xc, gain):
        ms = jnp.dot((xc * xc).astype(_BF16), pn_ref[...], preferred_element_type=_F32)
        y = xc * lax.rsqrt(ms + EPS) * gain
        swapped = jnp.where(first_half, pltpu.roll(y, LANES - HEAD_DIM // 2, 1),
                            pltpu.roll(y, HEAD_DIM // 2, 1))
        return y * cos + swapped * sin_signed

    def dup_heads(xc):
        r = pltpu.roll(xc, HEAD_DIM, 1)
        return jnp.where(low_head, xc, r), jnp.where(low_head, r, xc)

    scale = 1.0 / math.sqrt(HEAD_DIM)
    for c in range(ATTN_WIDTH // LANES):
        qc = norm_rotary(proj[:, c * LANES:(c + 1) * LANES], qg_ref[...])
        q_ref[:, c * LANES:(c + 1) * LANES] = (qc * scale).astype(_BF16)
    for c in range(KV_WIDTH // LANES):
        base = ATTN_WIDTH + c * LANES
        ka, kb = dup_heads(norm_rotary(proj[:, base:base + LANES], kg_ref[...]))
        k_ref[:, 2 * c * LANES:(2 * c + 1) * LANES] = ka.astype(_BF16)
        k_ref[:, (2 * c + 1) * LANES:(2 * c + 2) * LANES] = kb.astype(_BF16)
        base = ATTN_WIDTH + KV_WIDTH + c * LANES
        va, vb = dup_heads(proj[:, base:base + LANES])
        v_ref[:, 2 * c * LANES:(2 * c + 1) * LANES] = va.astype(_BF16)
        v_ref[:, (2 * c + 1) * LANES:(2 * c + 2) * LANES] = vb.astype(_BF16)
    u_ref[...] = proj[:, ATTN_WIDTH + 2 * KV_WIDTH:].astype(_BF16)


def _proj(x, gain, w_in, pos_f, invf, q_gain, k_gain, *, tm=512):
    L, D = x.shape
    C = w_in.shape[1]
    head_of_lane = jnp.arange(LANES) // HEAD_DIM
    pn = jnp.where(head_of_lane[:, None] == head_of_lane[None, :], 1.0 / HEAD_DIM, 0.0).astype(_BF16)
    qg = jnp.tile(q_gain, LANES // HEAD_DIM).reshape(1, LANES)
    kg = jnp.tile(k_gain, LANES // HEAD_DIM).reshape(1, LANES)
    row = lambda i: (i, 0)
    fixed = lambda i: (0, 0)
    return pl.pallas_call(
        _proj_body,
        grid=(L // tm,),
        in_specs=[
            pl.BlockSpec((tm, D), row),
            pl.BlockSpec((1, D), fixed),
            pl.BlockSpec((D, C), fixed),
            pl.BlockSpec((tm, LANES), row),
            pl.BlockSpec((1, LANES), fixed),
            pl.BlockSpec((1, LANES), fixed),
            pl.BlockSpec((1, LANES), fixed),
            pl.BlockSpec((LANES, LANES), fixed),
        ],
        out_specs=[
            pl.BlockSpec((tm, ATTN_WIDTH), row),
            pl.BlockSpec((tm, 2 * KV_WIDTH), row),
            pl.BlockSpec((tm, 2 * KV_WIDTH), row),
            pl.BlockSpec((tm, SSM_WIDTH), row),
        ],
        out_shape=[
            jax.ShapeDtypeStruct((L, ATTN_WIDTH), _BF16),
            jax.ShapeDtypeStruct((L, 2 * KV_WIDTH), _BF16),
            jax.ShapeDtypeStruct((L, 2 * KV_WIDTH), _BF16),
            jax.ShapeDtypeStruct((L, SSM_WIDTH), _BF16),
        ],
        compiler_params=pltpu.CompilerParams(
            dimension_semantics=("arbitrary",), vmem_limit_bytes=VMEM_LIMIT),
        name="proj",
    )(x, gain.reshape(1, D), w_in, pos_f, invf, qg, kg, pn)


def _attn_body(sink_ref, q_ref, kc_ref, vc_ref, kp_ref, vp_ref, o_ref):
    i = pl.program_id(0)
    tq = q_ref.shape[0]
    qi = lax.broadcasted_iota(jnp.int32, (BLOCK, 2 * BLOCK), 0)
    kj = lax.broadcasted_iota(jnp.int32, (BLOCK, 2 * BLOCK), 1)
    diff = qi + BLOCK - kj
    band = (diff >= 0) & (diff < BLOCK)
    band_first = band & ((kj >= BLOCK) | (i > 0))
    low_head = lax.broadcasted_iota(jnp.int32, (2 * BLOCK, LANES), 1) < HEAD_DIM
    nt = (((1,), (1,)), ((), ()))

    for b in range(tq // BLOCK):
        mask = band_first if b == 0 else band
        rows = slice(b * BLOCK, (b + 1) * BLOCK)
        for hk in range(N_KV_HEADS):
            cols = slice(hk * LANES, (hk + 1) * LANES)
            if b == 0:
                kd = jnp.concatenate([kp_ref[:, cols], kc_ref[0:BLOCK, cols]], axis=0)
                vd = jnp.concatenate([vp_ref[:, cols], vc_ref[0:BLOCK, cols]], axis=0)
            else:
                kd = kc_ref[(b - 1) * BLOCK:(b + 1) * BLOCK, cols]
                vd = vc_ref[(b - 1) * BLOCK:(b + 1) * BLOCK, cols]
            zero = jnp.zeros_like(kd)
            k_half = (jnp.where(low_head, kd, zero), jnp.where(low_head, zero, kd))
            v_half = (jnp.where(low_head, vd, zero), jnp.where(low_head, zero, vd))
            for pp in range(2):
                pair = hk * 2 + pp
                qp = q_ref[rows, pair * LANES:(pair + 1) * LANES]
                acc = None
                for half in range(2):
                    sink = sink_ref[pair * 2 + half]
                    s = lax.dot_general(qp, k_half[half], nt, preferred_element_type=_F32)
                    s = jnp.where(mask, s, -jnp.inf)
                    m = jnp.maximum(jnp.max(s, axis=-1, keepdims=True), sink)
                    p = jnp.exp(s - m)
                    den = jnp.sum(p, axis=-1, keepdims=True) + jnp.exp(sink - m)
                    o = jnp.dot(p.astype(_BF16), v_half[half], preferred_element_type=_F32) / den
                    acc = o if acc is None else acc + o
                o_ref[rows, pair * LANES:(pair + 1) * LANES] = acc.astype(_BF16)


def _attn(q, k2, v2, sinks, *, tq=512):
    L = q.shape[0]
    per = tq // BLOCK
    row = lambda i: (i, 0)
    prev = lambda i: (jnp.maximum(i * per - 1, 0), 0)
    return pl.pallas_call(
        _attn_body,
        grid=(L // tq,),
        in_specs=[
            pl.BlockSpec(memory_space=pltpu.SMEM),
            pl.BlockSpec((tq, ATTN_WIDTH), row),
            pl.BlockSpec((tq, 2 * KV_WIDTH), row),
            pl.BlockSpec((tq, 2 * KV_WIDTH), row),
            pl.BlockSpec((BLOCK, 2 * KV_WIDTH), prev),
            pl.BlockSpec((BLOCK, 2 * KV_WIDTH), prev),
        ],
        out_specs=pl.BlockSpec((tq, ATTN_WIDTH), row),
        out_shape=jax.ShapeDtypeStruct((L, ATTN_WIDTH), _BF16),
        compiler_params=pltpu.CompilerParams(
            dimension_semantics=("arbitrary",), vmem_limit_bytes=VMEM_LIMIT),
        name="attn",
    )(sinks, q, k2, v2, k2, v2)


def _ssm_body(u_ref, wst_ref, toep_ref, cp_ref, wglu_ref, bglu_ref, a16_ref, aseg_ref,
              o_ref, sre_ref, sim_ref):
    gb = u_ref.shape[0]
    n_w = u_ref.shape[1] // SSM_SEGMENTS
    width = sre_ref.shape[1]

    for pr in range(gb // 2):
        ucat = jnp.concatenate([u_ref[2 * pr], u_ref[2 * pr + 1]], axis=1)
        s = jnp.dot(ucat, wst_ref[pr], preferred_element_type=_F32)
        sre_ref[:, pr * LANES:(pr + 1) * LANES] = s[:, :LANES]
        sim_ref[:, pr * LANES:(pr + 1) * LANES] = s[:, LANES:]

    shape = (SSM_SEGMENTS, width)
    a_re = jnp.broadcast_to(a16_ref[0, 0:1, :], shape)
    a_im = jnp.broadcast_to(a16_ref[0, 1:2, :], shape)
    g_re = jnp.broadcast_to(aseg_ref[0, 0:1, :], shape)
    g_im = jnp.broadcast_to(aseg_ref[0, 1:2, :], shape)

    def step(w, carry):
        z_re, z_im = carry
        rows = pl.ds(pl.multiple_of(w * SSM_SEGMENTS, SSM_SEGMENTS), SSM_SEGMENTS)
        return (a_re * z_re - a_im * z_im + sre_ref[rows, :],
                a_re * z_im + a_im * z_re + sim_ref[rows, :])

    zero = jnp.zeros(shape, _F32)
    f_re, f_im = lax.fori_loop(0, n_w, step, (zero, zero))

    seg = lax.broadcasted_iota(jnp.int32, shape, 0)
    i_re, i_im = zero, zero
    for j in range(SSM_SEGMENTS - 1):
        c_re = g_re * i_re - g_im * i_im + f_re
        c_im = g_re * i_im + g_im * i_re + f_im
        i_re = jnp.where(seg == j + 1, pltpu.roll(c_re, 1, 0), i_re)
        i_im = jnp.where(seg == j + 1, pltpu.roll(c_im, 1, 0), i_im)

    def step_store(w, carry):
        z_re, z_im = carry
        rows = pl.ds(pl.multiple_of(w * SSM_SEGMENTS, SSM_SEGMENTS), SSM_SEGMENTS)
        s_re = sre_ref[rows, :]
        s_im = sim_ref[rows, :]
        sre_ref[rows, :] = z_re
        sim_ref[rows, :] = z_im
        return (a_re * z_re - a_im * z_im + s_re, a_re * z_im + a_im * z_re + s_im)

    lax.fori_loop(0, n_w, step_store, (i_re, i_im))

    half = SSM_CHUNK * SSM_GROUP
    for g in range(gb):
        pr = g // 2
        xcat = jnp.concatenate([sre_ref[:, pr * LANES:(pr + 1) * LANES],
                                sim_ref[:, pr * LANES:(pr + 1) * LANES]], axis=1).astype(_BF16)
        y = (jnp.dot(u_ref[g], toep_ref[g], preferred_element_type=_F32)
             + jnp.dot(xcat, cp_ref[g], preferred_element_type=_F32))
        z = jax.nn.gelu(y, approximate=True)
        gt = jnp.dot(z.astype(_BF16), wglu_ref[g], preferred_element_type=_F32) + bglu_ref[g]
        o_ref[g] = (gt[:, :half] * jax.nn.sigmoid(gt[:, half:])).astype(_BF16)


def _complex_power(re, im, n):
    out_re, out_im = None, None
    while n:
        if n & 1:
            if out_re is None:
                out_re, out_im = re, im
            else:
                out_re, out_im = out_re * re - out_im * im, out_re * im + out_im * re
        n >>= 1
        if n:
            re, im = re * re - im * im, 2.0 * re * im
    return out_re, out_im


def _ssm_weights(log_dt, a_re, a_im, b_re, b_im, c_re, c_im, d_skip, w_glu, b_glu, n_w):
    G, P, H, T = SSM_GROUPS, SSM_STATE, SSM_GROUP, SSM_CHUNK
    dt = jnp.exp(log_dt)[:, None]
    x_re = a_re * dt
    x_im = a_im * dt
    mag = jnp.exp(x_re)
    ab_re = mag * jnp.cos(x_im)
    ab_im = mag * jnp.sin(x_im)
    inv_den = 1.0 / (a_re * a_re + a_im * a_im)
    nr = ab_re - 1.0
    coef_re = (nr * a_re + ab_im * a_im) * inv_den
    coef_im = (ab_im * a_re - nr * a_im) * inv_den
    bb_re = coef_re[..., None] * b_re - coef_im[..., None] * b_im
    bb_im = coef_re[..., None] * b_im + coef_im[..., None] * b_re

    k = jnp.arange(T + 1, dtype=_F32)[:, None, None]
    pw_mag = jnp.exp(x_re[None] * k)
    pw_re = pw_mag * jnp.cos(x_im[None] * k)
    pw_im = pw_mag * jnp.sin(x_im[None] * k)

    cp_re = c_re[None] * pw_re[:, :, None, :] - c_im[None] * pw_im[:, :, None, :]
    cp_im = c_re[None] * pw_im[:, :, None, :] + c_im[None] * pw_re[:, :, None, :]

    kern = (jnp.einsum('kgop,gpi->kgoi', cp_re[:T], bb_re, precision=_HI)
            - jnp.einsum('kgop,gpi->kgoi', cp_im[:T], bb_im, precision=_HI))
    kern = kern.at[0].add(d_skip[:, :, None] * jnp.eye(H, dtype=_F32))
    lag = jnp.arange(T)[None, :] - jnp.arange(T)[:, None]
    toep = jnp.where((lag >= 0)[:, :, None, None, None], kern[jnp.maximum(lag, 0)], 0.0)
    toep = toep.transpose(2, 0, 4, 1, 3).reshape(G, T * H, T * H)

    rev_re = pw_re[T - 1::-1][:T]
    rev_im = pw_im[T - 1::-1][:T]
    st_re = rev_re[..., None] * bb_re[None] - rev_im[..., None] * bb_im[None]
    st_im = rev_re[..., None] * bb_im[None] + rev_im[..., None] * bb_re[None]
    st_re = st_re.transpose(1, 0, 3, 2).reshape(G // 2, 2, T * H, P)
    st_im = st_im.transpose(1, 0, 3, 2).reshape(G // 2, 2, T * H, P)
    zs = jnp.zeros_like(st_re[:, 0])
    wst = jnp.concatenate([
        jnp.concatenate([st_re[:, 0], zs, st_im[:, 0], zs], axis=-1),
        jnp.concatenate([zs, st_re[:, 1], zs, st_im[:, 1]], axis=-1)], axis=1)

    m_re = cp_re[1:].transpose(1, 3, 0, 2).reshape(G // 2, 2, P, T * H)
    m_im = -cp_im[1:].transpose(1, 3, 0, 2).reshape(G // 2, 2, P, T * H)
    zc = jnp.zeros_like(m_re[:, 0])
    cp = jnp.stack([
        jnp.concatenate([m_re[:, 0], zc, m_im[:, 0], zc], axis=1),
        jnp.concatenate([zc, m_re[:, 1], zc, m_im[:, 1]], axis=1)], axis=1)
    cp = cp.reshape(G, 4 * P, T * H)

    eye_t = jnp.eye(T, dtype=_F32)
    glu = jnp.einsum('ts,ghk->gthsk', eye_t, w_glu)
    wglu = jnp.concatenate([glu[..., :H].reshape(G, T * H, T * H),
                            glu[..., H:].reshape(G, T * H, T * H)], axis=-1)
    bglu = jnp.concatenate([jnp.tile(b_glu[:, :H], (1, T)), jnp.tile(b_glu[:, H:], (1, T))],
                           axis=-1).reshape(G, 1, 2 * T * H)

    gbs = SSM_GROUPS_PER_STEP
    a16 = jnp.stack([pw_re[T].reshape(G // gbs, gbs * P), pw_im[T].reshape(G // gbs, gbs * P)], axis=1)
    seg_re, seg_im = _complex_power(pw_re[T], pw_im[T], n_w)
    aseg = jnp.stack([seg_re.reshape(G // gbs, gbs * P), seg_im.reshape(G // gbs, gbs * P)], axis=1)
    return (wst.astype(_BF16), toep.astype(_BF16), cp.astype(_BF16), wglu.astype(_BF16), bglu, a16, aseg)


def _ssm(u_g, weights):
    G, R, W = u_g.shape
    gb = SSM_GROUPS_PER_STEP
    wst, toep, cp, wglu, bglu, a16, aseg = weights
    blk = lambda i: (i, 0, 0)
    return pl.pallas_call(
        _ssm_body,
        grid=(G // gb,),
        in_specs=[
            pl.BlockSpec((gb, R, W), blk),
            pl.BlockSpec((gb // 2, 2 * W, W), blk),
            pl.BlockSpec((gb, W, W), blk),
            pl.BlockSpec((gb, W, W), blk),
            pl.BlockSpec((gb, W, 2 * W), blk),
            pl.BlockSpec((gb, 1, 2 * W), blk),
            pl.BlockSpec((1, 2, gb * SSM_STATE), blk),
            pl.BlockSpec((1, 2, gb * SSM_STATE), blk),
        ],
        out_specs=pl.BlockSpec((gb, R, W), blk),
        out_shape=jax.ShapeDtypeStruct((G, R, W), _BF16),
        scratch_shapes=[pltpu.VMEM((R, gb * SSM_STATE), _F32), pltpu.VMEM((R, gb * SSM_STATE), _F32)],
        compiler_params=pltpu.CompilerParams(
            dimension_semantics=("arbitrary",), vmem_limit_bytes=VMEM_LIMIT),
        name="ssm",
    )(u_g, wst, toep, cp, wglu, bglu, a16, aseg)


def _to_group_rows(u, n_w):
    t = u.reshape(SSM_SEGMENTS, n_w, SSM_CHUNK, SSM_GROUPS, SSM_GROUP)
    return t.transpose(3, 1, 0, 2, 4).reshape(SSM_GROUPS, n_w * SSM_SEGMENTS, SSM_CHUNK * SSM_GROUP)


def _from_group_rows(y, n_w):
    t = y.reshape(SSM_GROUPS, n_w, SSM_SEGMENTS, SSM_CHUNK, SSM_GROUP)
    return t.transpose(2, 1, 3, 0, 4).reshape(SSM_SEGMENTS * n_w * SSM_CHUNK, SSM_WIDTH)


def _outproj_body(a_ref, s_ref, x_ref, ag_ref, sg_ref, w_ref, o_ref):
    an = _rms(a_ref[...].astype(_F32), ag_ref[...]).astype(_BF16)
    sn = _rms(s_ref[...].astype(_F32), sg_ref[...]).astype(_BF16)
    mixed = jnp.concatenate([an, sn], axis=1)
    o_ref[...] = x_ref[...] + jnp.dot(mixed, w_ref[...], preferred_element_type=_F32)


def _outproj(attn, ssm, x, a_gain, s_gain, w_out, *, tm=512):
    L, D = x.shape
    row = lambda i: (i, 0)
    fixed = lambda i: (0, 0)
    return pl.pallas_call(
        _outproj_body,
        grid=(L // tm,),
        in_specs=[
            pl.BlockSpec((tm, ATTN_WIDTH), row),
            pl.BlockSpec((tm, SSM_WIDTH), row),
            pl.BlockSpec((tm, D), row),
            pl.BlockSpec((1, ATTN_WIDTH), fixed),
            pl.BlockSpec((1, SSM_WIDTH), fixed),
            pl.BlockSpec((ATTN_WIDTH + SSM_WIDTH, D), fixed),
        ],
        out_specs=pl.BlockSpec((tm, D), row),
        out_shape=jax.ShapeDtypeStruct((L, D), _F32),
        compiler_params=pltpu.CompilerParams(
            dimension_semantics=("arbitrary",), vmem_limit_bytes=VMEM_LIMIT),
        name="outproj",
    )(attn, ssm, x, a_gain.reshape(1, -1), s_gain.reshape(1, -1), w_out)


def _layer(x, pos_f, invf, p):
    L = x.shape[0]
    n_w = L // (SSM_SEGMENTS * SSM_CHUNK)
    x = _ffn(x, p['ffn1_norm'], p['ffn1_w_gate'].astype(_BF16), p['ffn1_w_up'].astype(_BF16),
             p['ffn1_w_down'].astype(_BF16))
    q, k2, v2, u = _proj(x, p['mix_norm'], p['w_in'].astype(_BF16), pos_f, invf, p['q_norm'], p['k_norm'])
    attn = _attn(q, k2, v2, p['attn_sinks'])
    weights = _ssm_weights(p['ssm_log_dt'], p['ssm_a_re'], p['ssm_a_im'], p['ssm_b_re'], p['ssm_b_im'],
                           p['ssm_c_re'], p['ssm_c_im'], p['ssm_d'], p['ssm_w_glu'], p['ssm_b_glu'], n_w)
    ssm = _from_group_rows(_ssm(_to_group_rows(u, n_w), weights), n_w)
    x = _outproj(attn, ssm, x, p['attn_out_norm'], p['ssm_out_norm'], p['w_out'].astype(_BF16))
    return _ffn(x, p['ffn2_norm'], p['ffn2_w_gate'].astype(_BF16), p['ffn2_w_up'].astype(_BF16),
                p['ffn2_w_down'].astype(_BF16))


def kernel(x, positions, ffn1_norm, ffn1_w_gate, ffn1_w_up, ffn1_w_down, mix_norm, w_in, q_norm, k_norm,
           attn_sinks, ssm_log_dt, ssm_a_re, ssm_a_im, ssm_b_re, ssm_b_im, ssm_c_re, ssm_c_im, ssm_d,
           ssm_w_glu, ssm_b_glu, attn_out_norm, ssm_out_norm, w_out, ffn2_norm, ffn2_w_gate, ffn2_w_up,
           ffn2_w_down):
    params = dict(
        ffn1_norm=ffn1_norm, ffn1_w_gate=ffn1_w_gate, ffn1_w_up=ffn1_w_up, ffn1_w_down=ffn1_w_down,
        mix_norm=mix_norm, w_in=w_in, q_norm=q_norm, k_norm=k_norm, attn_sinks=attn_sinks,
        ssm_log_dt=ssm_log_dt, ssm_a_re=ssm_a_re, ssm_a_im=ssm_a_im, ssm_b_re=ssm_b_re, ssm_b_im=ssm_b_im,
        ssm_c_re=ssm_c_re, ssm_c_im=ssm_c_im, ssm_d=ssm_d, ssm_w_glu=ssm_w_glu, ssm_b_glu=ssm_b_glu,
        attn_out_norm=attn_out_norm, ssm_out_norm=ssm_out_norm, w_out=w_out,
        ffn2_norm=ffn2_norm, ffn2_w_gate=ffn2_w_gate, ffn2_w_up=ffn2_w_up, ffn2_w_down=ffn2_w_down)
    depth = ffn1_norm.shape[0]
    half = HEAD_DIM // 2
    inv_freq = ROPE_THETA ** (-jnp.arange(half, dtype=_F32) * 2.0 / HEAD_DIM)
    invf = jnp.tile(inv_freq, LANES // half).reshape(1, LANES)
    outs = []
    for b in range(x.shape[0]):
        xb = x[b]
        pos_f = jnp.broadcast_to(positions[b].astype(_F32)[:, None], (xb.shape[0], LANES))
        for i in range(depth):
            xb = _layer(xb, pos_f, invf, {name: val[i] for name, val in params.items()})
        outs.append(xb)
    return jnp.stack(outs, axis=0)
```

```python
import functools
import math

import jax
import jax.numpy as jnp
from jax import lax
from jax.experimental import pallas as pl
from jax.experimental.pallas import tpu as pltpu

HEAD_DIM = 64
N_Q_HEADS = 16
N_KV_HEADS = 4
ATTN_WIDTH = N_Q_HEADS * HEAD_DIM
KV_WIDTH = N_KV_HEADS * HEAD_DIM
BLOCK = 128
ROPE_THETA = 10000.0
SSM_GROUP = 16
SSM_GROUPS = 64
SSM_STATE = 64
SSM_WIDTH = SSM_GROUP * SSM_GROUPS
FFN_RESIDUAL = 0.5
EPS = 1e-6

LANES = 128
SUBLANES = 8
MXU_DIM = 256
SSM_CHUNK = 8
SSM_SEGMENTS = SUBLANES
SLAB_GROUPS = LANES // SSM_GROUP
SLAB_STATES = SLAB_GROUPS * SSM_STATE
VMEM_LIMIT = 56 * 1024 * 1024

_BF16 = jnp.bfloat16
_F32 = jnp.float32
_U32 = jnp.uint32
_HI = lax.Precision.HIGHEST


def _rms(x, gain):
    ms = jnp.mean(x * x, axis=-1, keepdims=True)
    return x * lax.rsqrt(ms + EPS) * gain


def _ffn_up_body(x_ref, gain_ref, wg_ref, wu_ref, a_ref, h_ref):
    @pl.when(pl.program_id(1) == 0)
    def _():
        h_ref[...] = _rms(x_ref[...], gain_ref[...]).astype(_BF16)

    h = h_ref[...]
    g = jnp.dot(h, wg_ref[...], preferred_element_type=_F32)
    u = jnp.dot(h, wu_ref[...], preferred_element_type=_F32)
    a_ref[...] = (g * jax.nn.sigmoid(g) * u).astype(_BF16)


def _ffn_down_body(a_ref, wd_ref, x_ref, o_ref):
    d = jnp.dot(a_ref[...], wd_ref[...], preferred_element_type=_F32)
    o_ref[...] = x_ref[...] + FFN_RESIDUAL * d


def _ffn(x, gain, wg, wu, wd, *, tm=1024, tf=512, tn=512):
    L, D = x.shape
    F = wg.shape[1]
    tm = min(tm, L)
    params = pltpu.CompilerParams(
        dimension_semantics=("arbitrary", "arbitrary"), vmem_limit_bytes=VMEM_LIMIT)
    act = pl.pallas_call(
        _ffn_up_body,
        grid=(L // tm, F // tf),
        in_specs=[
            pl.BlockSpec((tm, D), lambda i, j: (i, 0)),
            pl.BlockSpec((1, D), lambda i, j: (0, 0)),
            pl.BlockSpec((D, tf), lambda i, j: (0, j)),
            pl.BlockSpec((D, tf), lambda i, j: (0, j)),
        ],
        out_specs=pl.BlockSpec((tm, tf), lambda i, j: (i, j)),
        out_shape=jax.ShapeDtypeStruct((L, F), _BF16),
        scratch_shapes=[pltpu.VMEM((tm, D), _BF16)],
        compiler_params=params,
        name="ffn_up",
    )(x, gain.reshape(1, D), wg, wu)
    return pl.pallas_call(
        _ffn_down_body,
        grid=(L // tm, D // tn),
        in_specs=[
            pl.BlockSpec((tm, F), lambda i, j: (i, 0)),
            pl.BlockSpec((F, tn), lambda i, j: (0, j)),
            pl.BlockSpec((tm, tn), lambda i, j: (i, j)),
        ],
        out_specs=pl.BlockSpec((tm, tn), lambda i, j: (i, j)),
        out_shape=jax.ShapeDtypeStruct((L, D), _F32),
        compiler_params=params,
        name="ffn_down",
    )(act, wd, x)


def _proj_body(x_ref, gain_ref, w_ref, pos_ref, invf_ref, qg_ref, kg_ref, pn_ref,
               q_ref, k_ref, v_ref, u_ref, us_ref):
    tm = x_ref.shape[0]
    h = _rms(x_ref[...], gain_ref[...]).astype(_BF16)
    proj = jnp.dot(h, w_ref[...], preferred_element_type=_F32)

    ang = pos_ref[...] * invf_ref[...]
    cos = jnp.cos(ang)
    sin = jnp.sin(ang)
    lane = lax.broadcasted_iota(jnp.int32, (tm, LANES), 1)
    first_half = (lane & (HEAD_DIM // 2)) == 0
    low_head = lane < HEAD_DIM
    sin_signed = jnp.where(first_half, -sin, sin)

    def norm_rotary(xc, gain):
        ms = jnp.dot((xc * xc).astype(_BF16), pn_ref[...], preferred_element_type=_F32)
        y = xc * lax.rsqrt(ms + EPS) * gain
        swapped = jnp.where(first_half, pltpu.roll(y, LANES - HEAD_DIM // 2, 1),
                            pltpu.roll(y, HEAD_DIM // 2, 1))
        return y * cos + swapped * sin_signed

    def dup_heads(xc):
        r = pltpu.roll(xc, HEAD_DIM, 1)
        return jnp.where(low_head, xc, r), jnp.where(low_head, r, xc)

    scale = 1.0 / math.sqrt(HEAD_DIM)
    for c in range(ATTN_WIDTH // LANES):
        qc = norm_rotary(proj[:, c * LANES:(c + 1) * LANES], qg_ref[...])
        q_ref[:, c * LANES:(c + 1) * LANES] = (qc * scale).astype(_BF16)
    for c in range(KV_WIDTH // LANES):
        base = ATTN_WIDTH + c * LANES
        ka, kb = dup_heads(norm_rotary(proj[:, base:base + LANES], kg_ref[...]))
        k_ref[:, 2 * c * LANES:(2 * c + 1) * LANES] = ka.astype(_BF16)
        k_ref[:, (2 * c + 1) * LANES:(2 * c + 2) * LANES] = kb.astype(_BF16)
        base = ATTN_WIDTH + KV_WIDTH + c * LANES
        va, vb = dup_heads(proj[:, base:base + LANES])
        v_ref[:, 2 * c * LANES:(2 * c + 1) * LANES] = va.astype(_BF16)
        v_ref[:, (2 * c + 1) * LANES:(2 * c + 2) * LANES] = vb.astype(_BF16)
    base = ATTN_WIDTH + 2 * KV_WIDTH
    for s in range(SSM_WIDTH // LANES):
        us_ref[s] = proj[:, base + s * LANES:base + (s + 1) * LANES]
    for t in range(SSM_CHUNK):
        for s in range(SSM_WIDTH // LANES):
            rows = us_ref[s, pl.ds(t, tm // SSM_CHUNK, stride=SSM_CHUNK), :]
            u_ref[t, :, s * LANES:(s + 1) * LANES] = rows.astype(_BF16)


def _proj(x, gain, w_in, pos_f, invf, q_gain, k_gain, *, tm=512):
    L, D = x.shape
    C = w_in.shape[1]
    head_of_lane = jnp.arange(LANES) // HEAD_DIM
    pn = jnp.where(head_of_lane[:, None] == head_of_lane[None, :], 1.0 / HEAD_DIM, 0.0).astype(_BF16)
    qg = jnp.tile(q_gain, LANES // HEAD_DIM).reshape(1, LANES)
    kg = jnp.tile(k_gain, LANES // HEAD_DIM).reshape(1, LANES)
    row = lambda i: (i, 0)
    fixed = lambda i: (0, 0)
    return pl.pallas_call(
        _proj_body,
        grid=(L // tm,),
        in_specs=[
            pl.BlockSpec((tm, D), row),
            pl.BlockSpec((1, D), fixed),
            pl.BlockSpec((D, C), fixed),
            pl.BlockSpec((tm, LANES), row),
            pl.BlockSpec((1, LANES), fixed),
            pl.BlockSpec((1, LANES), fixed),
            pl.BlockSpec((1, LANES), fixed),
            pl.BlockSpec((LANES, LANES), fixed),
        ],
        out_specs=[
            pl.BlockSpec((tm, ATTN_WIDTH), row),
            pl.BlockSpec((tm, 2 * KV_WIDTH), row),
            pl.BlockSpec((tm, 2 * KV_WIDTH), row),
            pl.BlockSpec((SSM_CHUNK, tm // SSM_CHUNK, SSM_WIDTH), lambda i: (0, i, 0)),
        ],
        out_shape=[
            jax.ShapeDtypeStruct((L, ATTN_WIDTH), _BF16),
            jax.ShapeDtypeStruct((L, 2 * KV_WIDTH), _BF16),
            jax.ShapeDtypeStruct((L, 2 * KV_WIDTH), _BF16),
            jax.ShapeDtypeStruct((SSM_CHUNK, L // SSM_CHUNK, SSM_WIDTH), _BF16),
        ],
        scratch_shapes=[pltpu.VMEM((SSM_WIDTH // LANES, tm, LANES), _F32)],
        compiler_params=pltpu.CompilerParams(
            dimension_semantics=("arbitrary",), vmem_limit_bytes=VMEM_LIMIT),
        name="proj",
    )(x, gain.reshape(1, D), w_in, pos_f, invf, qg, kg, pn)


def _attn_body(sink_ref, q_ref, kc_ref, vc_ref, kp_ref, vp_ref, o_ref):
    i = pl.program_id(0)
    tq = q_ref.shape[0]
    qi = lax.broadcasted_iota(jnp.int32, (BLOCK, 2 * BLOCK), 0)
    kj = lax.broadcasted_iota(jnp.int32, (BLOCK, 2 * BLOCK), 1)
    diff = qi + BLOCK - kj
    band = (diff >= 0) & (diff < BLOCK)
    band_first = band & ((kj >= BLOCK) | (i > 0))
    low_head = lax.broadcasted_iota(jnp.int32, (2 * BLOCK, LANES), 1) < HEAD_DIM
    nt = (((1,), (1,)), ((), ()))

    for b in range(tq // BLOCK):
        mask = band_first if b == 0 else band
        rows = slice(b * BLOCK, (b + 1) * BLOCK)
        for hk in range(N_KV_HEADS):
            cols = slice(hk * LANES, (hk + 1) * LANES)
            if b == 0:
                kd = jnp.concatenate([kp_ref[:, cols], kc_ref[0:BLOCK, cols]], axis=0)
                vd = jnp.concatenate([vp_ref[:, cols], vc_ref[0:BLOCK, cols]], axis=0)
            else:
                kd = kc_ref[(b - 1) * BLOCK:(b + 1) * BLOCK, cols]
                vd = vc_ref[(b - 1) * BLOCK:(b + 1) * BLOCK, cols]
            zero = jnp.zeros_like(kd)
            k_half = (jnp.where(low_head, kd, zero), jnp.where(low_head, zero, kd))
            v_half = (jnp.where(low_head, vd, zero), jnp.where(low_head, zero, vd))
            for pp in range(2):
                pair = hk * 2 + pp
                qp = q_ref[rows, pair * LANES:(pair + 1) * LANES]
                acc = None
                for half in range(2):
                    sink = sink_ref[pair * 2 + half]
                    s = lax.dot_general(qp, k_half[half], nt, preferred_element_type=_F32)
                    s = jnp.where(mask, s, -jnp.inf)
                    m = jnp.maximum(jnp.max(s, axis=-1, keepdims=True), sink)
                    p = jnp.exp(s - m)
                    den = jnp.sum(p, axis=-1, keepdims=True) + jnp.exp(sink - m)
                    o = jnp.dot(p.astype(_BF16), v_half[half], preferred_element_type=_F32) / den
                    acc = o if acc is None else acc + o
                o_ref[rows, pair * LANES:(pair + 1) * LANES] = acc.astype(_BF16)


def _attn(q, k2, v2, sinks, *, tq=512):
    L = q.shape[0]
    per = tq // BLOCK
    row = lambda i: (i, 0)
    prev = lambda i: (jnp.maximum(i * per - 1, 0), 0)
    return pl.pallas_call(
        _attn_body,
        grid=(L // tq,),
        in_specs=[
            pl.BlockSpec(memory_space=pltpu.SMEM),
            pl.BlockSpec((tq, ATTN_WIDTH), row),
            pl.BlockSpec((tq, 2 * KV_WIDTH), row),
            pl.BlockSpec((tq, 2 * KV_WIDTH), row),
            pl.BlockSpec((BLOCK, 2 * KV_WIDTH), prev),
            pl.BlockSpec((BLOCK, 2 * KV_WIDTH), prev),
        ],
        out_specs=pl.BlockSpec((tq, ATTN_WIDTH), row),
        out_shape=jax.ShapeDtypeStruct((L, ATTN_WIDTH), _BF16),
        compiler_params=pltpu.CompilerParams(
            dimension_semantics=("arbitrary",), vmem_limit_bytes=VMEM_LIMIT),
        name="attn",
    )(sinks, q, k2, v2, k2, v2)


def _ssm_body(u_ref, wst_ref, kw_ref, cp_ref, wg_ref, bg_ref, a_ref, aseg_ref, o_ref,
              lhs_ref, toep_ref, s_ref, xb_ref):
    T = SSM_CHUNK
    n_c = u_ref.shape[1]
    seg = n_c // SSM_SEGMENTS
    pitch = s_ref.shape[1] // SSM_SEGMENTS
    n_state_slabs = SLAB_STATES // LANES
    pair_w = 2 * LANES

    for t in range(T):
        lhs_ref[:, t * LANES:(t + 1) * LANES] = u_ref[t]

    toep_ref[...] = jnp.zeros(toep_ref.shape, _BF16)
    for t in range(T):
        for tp in range(t, T):
            toep_ref[t * LANES:(t + 1) * LANES, tp * LANES:(tp + 1) * LANES] = (
                kw_ref[:, (tp - t) * LANES:(tp - t + 1) * LANES])

    for nb in range(2 * n_state_slabs // 2):
        res = jnp.dot(lhs_ref[...], wst_ref[:, nb * pair_w:(nb + 1) * pair_w], preferred_element_type=_F32)
        for half in range(2):
            for j in range(SSM_SEGMENTS):
                s_ref[2 * nb + half, j * pitch:j * pitch + seg, :] = (
                    res[j * seg:(j + 1) * seg, half * LANES:(half + 1) * LANES])

    shape = (SSM_SEGMENTS, LANES)
    a_re = [jnp.broadcast_to(a_ref[0:1, k * LANES:(k + 1) * LANES], shape) for k in range(n_state_slabs)]
    a_im = [jnp.broadcast_to(a_ref[1:2, k * LANES:(k + 1) * LANES], shape) for k in range(n_state_slabs)]
    g_re = [jnp.broadcast_to(aseg_ref[0:1, k * LANES:(k + 1) * LANES], shape) for k in range(n_state_slabs)]
    g_im = [jnp.broadcast_to(aseg_ref[1:2, k * LANES:(k + 1) * LANES], shape) for k in range(n_state_slabs)]

    def seg_rows(w):
        return pl.ds(w, SSM_SEGMENTS, stride=pitch)

    def advance(w, carry, store):
        out = []
        for k in range(n_state_slabs):
            z_re, z_im = carry[2 * k], carry[2 * k + 1]
            s_re = s_ref[k, seg_rows(w), :]
            s_im = s_ref[n_state_slabs + k, seg_rows(w), :]
            if store:
                s_ref[k, seg_rows(w), :] = z_re
                s_ref[n_state_slabs + k, seg_rows(w), :] = z_im
            out.append(a_re[k] * z_re - a_im[k] * z_im + s_re)
            out.append(a_re[k] * z_im + a_im[k] * z_re + s_im)
        return tuple(out)

    zero = jnp.zeros(shape, _F32)
    ends = lax.fori_loop(0, seg, lambda w, c: advance(w, c, False), (zero,) * (2 * n_state_slabs))

    segidx = lax.broadcasted_iota(jnp.int32, shape, 0)
    init = []
    for k in range(n_state_slabs):
        i_re, i_im = zero, zero
        f_re, f_im = ends[2 * k], ends[2 * k + 1]
        for j in range(SSM_SEGMENTS - 1):
            c_re = g_re[k] * i_re - g_im[k] * i_im + f_re
            c_im = g_re[k] * i_im + g_im[k] * i_re + f_im
            i_re = jnp.where(segidx == j + 1, pltpu.roll(c_re, 1, 0), i_re)
            i_im = jnp.where(segidx == j + 1, pltpu.roll(c_im, 1, 0), i_im)
        init += [i_re, i_im]

    lax.fori_loop(0, seg, lambda w, c: advance(w, c, True), tuple(init))

    for col in range(2 * n_state_slabs):
        for j in range(SSM_SEGMENTS):
            xb_ref[j * seg:(j + 1) * seg, col * LANES:(col + 1) * LANES] = (
                s_ref[col, j * pitch:j * pitch + seg, :].astype(_BF16))

    for i in range(T // 2):
        kk = (i + 1) * pair_w
        cols = slice(i * pair_w, (i + 1) * pair_w)
        y = (jnp.dot(lhs_ref[:, :kk], toep_ref[:kk, cols], preferred_element_type=_F32)
             + jnp.dot(xb_ref[...], cp_ref[:, cols], preferred_element_type=_F32))
        z = jax.nn.gelu(y, approximate=True)
        gt = jnp.dot(z.astype(_BF16), wg_ref[...], preferred_element_type=_F32) + bg_ref[...]
        out = gt[:, :pair_w] * jax.nn.sigmoid(gt[:, pair_w:])
        o_ref[2 * i] = out[:, :LANES].astype(_BF16)
        o_ref[2 * i + 1] = out[:, LANES:].astype(_BF16)


def _complex_power(re, im, n):
    out_re, out_im = None, None
    while n:
        if n & 1:
            if out_re is None:
                out_re, out_im = re, im
            else:
                out_re, out_im = out_re * re - out_im * im, out_re * im + out_im * re
        n >>= 1
        if n:
            re, im = re * re - im * im, 2.0 * re * im
    return out_re, out_im


def _ssm_weights(log_dt, a_re, a_im, b_re, b_im, c_re, c_im, d_skip, w_glu, b_glu, seg):
    G, P, H, T = SSM_GROUPS, SSM_STATE, SSM_GROUP, SSM_CHUNK
    SG, NS = SLAB_GROUPS, SSM_GROUPS // SLAB_GROUPS
    dt = jnp.exp(log_dt)[:, None]
    x_re = a_re * dt
    x_im = a_im * dt
    mag = jnp.exp(x_re)
    ab_re = mag * jnp.cos(x_im)
    ab_im = mag * jnp.sin(x_im)
    inv_den = 1.0 / (a_re * a_re + a_im * a_im)
    nr = ab_re - 1.0
    coef_re = (nr * a_re + ab_im * a_im) * inv_den
    coef_im = (ab_im * a_re - nr * a_im) * inv_den
    bb_re = coef_re[..., None] * b_re - coef_im[..., None] * b_im
    bb_im = coef_re[..., None] * b_im + coef_im[..., None] * b_re

    k = jnp.arange(T + 1, dtype=_F32)[:, None, None]
    pw_mag = jnp.exp(x_re[None] * k)
    pw_re = (pw_mag * jnp.cos(x_im[None] * k)).reshape(T + 1, NS, SG * P)
    pw_im = (pw_mag * jnp.sin(x_im[None] * k)).reshape(T + 1, NS, SG * P)

    group_of_chan = jnp.arange(SG * H) // H
    group_of_state = jnp.arange(SG * P) // P
    same_cs = group_of_chan[:, None] == group_of_state[None, :]
    same_cc = group_of_chan[:, None] == group_of_chan[None, :]

    def chan_by_state(bb):
        t = bb.reshape(NS, SG, P, H).transpose(0, 3, 1, 2).reshape(NS, H, SG * P)
        return jnp.where(same_cs, jnp.tile(t, (1, SG, 1)), 0.0)

    def state_by_chan(c):
        t = c.reshape(NS, SG, H, P).transpose(0, 1, 3, 2).reshape(NS, SG * P, H)
        return jnp.where(same_cs.T, jnp.tile(t, (1, 1, SG)), 0.0)

    bm_re, bm_im = chan_by_state(bb_re), chan_by_state(bb_im)
    cm_re, cm_im = state_by_chan(c_re), state_by_chan(c_im)

    rev_re = pw_re[T - 1::-1][:, :, None, :]
    rev_im = pw_im[T - 1::-1][:, :, None, :]
    wst = jnp.concatenate([bm_re[None] * rev_re - bm_im[None] * rev_im,
                           bm_re[None] * rev_im + bm_im[None] * rev_re], axis=-1)
    wst = wst.transpose(1, 0, 2, 3).reshape(NS, T * SG * H, 2 * SG * P)

    col_re = pw_re[..., None]
    col_im = pw_im[..., None]
    ck_re = cm_re[None] * col_re - cm_im[None] * col_im
    ck_im = cm_re[None] * col_im + cm_im[None] * col_re

    kern = jnp.einsum('sap,kspb->sakb', jnp.concatenate([bm_re, -bm_im], axis=-1),
                      jnp.concatenate([ck_re[:T], ck_im[:T]], axis=2), precision=_HI)
    d_diag = jnp.where(jnp.eye(SG * H, dtype=bool), d_skip.reshape(NS, 1, SG * H), 0.0)
    kern = kern.at[:, :, 0, :].add(d_diag)
    kw = kern.reshape(NS, SG * H, T * SG * H)

    cp = jnp.concatenate([ck_re[1:], -ck_im[1:]], axis=2)
    cp = cp.transpose(1, 2, 0, 3).reshape(NS, 2 * SG * P, T * SG * H)

    def chan_by_chan(w):
        return jnp.where(same_cc, jnp.tile(w.reshape(NS, SG * H, H), (1, 1, SG)), 0.0)

    w_lin, w_gate = chan_by_chan(w_glu[..., :H]), chan_by_chan(w_glu[..., H:])
    zz = jnp.zeros_like(w_lin)
    wg = jnp.concatenate([jnp.concatenate([w_lin, zz, w_gate, zz], axis=-1),
                          jnp.concatenate([zz, w_lin, zz, w_gate], axis=-1)], axis=1)
    b_lin = b_glu[:, :H].reshape(NS, 1, SG * H)
    b_gate = b_glu[:, H:].reshape(NS, 1, SG * H)
    bg = jnp.concatenate([b_lin, b_lin, b_gate, b_gate], axis=-1)

    a_chunk = jnp.stack([pw_re[T], pw_im[T]], axis=1)
    seg_re, seg_im = _complex_power(pw_re[T], pw_im[T], seg)
    a_seg = jnp.stack([seg_re, seg_im], axis=1)
    return (wst.astype(_BF16), kw.astype(_BF16), cp.astype(_BF16), wg.astype(_BF16), bg, a_chunk, a_seg)


def _ssm(u_t, weights):
    T, n_c, W = u_t.shape
    seg = n_c // SSM_SEGMENTS
    pitch = seg + SUBLANES
    ns = W // LANES
    wst, kw, cp, wg, bg, a_chunk, a_seg = weights
    slab = lambda i: (0, 0, i)
    blk = lambda i: (i, 0, 0)
    once = pl.Buffered(1)
    return pl.pallas_call(
        _ssm_body,
        grid=(ns,),
        in_specs=[
            pl.BlockSpec((T, n_c, LANES), slab, pipeline_mode=once),
            pl.BlockSpec((None, T * LANES, 2 * SLAB_STATES), blk, pipeline_mode=once),
            pl.BlockSpec((None, LANES, T * LANES), blk),
            pl.BlockSpec((None, 2 * SLAB_STATES, T * LANES), blk, pipeline_mode=once),
            pl.BlockSpec((None, 2 * LANES, 4 * LANES), blk),
            pl.BlockSpec((None, 1, 4 * LANES), blk),
            pl.BlockSpec((None, 2, SLAB_STATES), blk),
            pl.BlockSpec((None, 2, SLAB_STATES), blk),
        ],
        out_specs=pl.BlockSpec((T, n_c, LANES), slab),
        out_shape=jax.ShapeDtypeStruct((T, n_c, W), _BF16),
        scratch_shapes=[
            pltpu.VMEM((n_c, T * LANES), _BF16),
            pltpu.VMEM((T * LANES, T * LANES), _BF16),
            pltpu.VMEM((2 * SLAB_STATES // LANES, SSM_SEGMENTS * pitch, LANES), _F32),
            pltpu.VMEM((n_c, 2 * SLAB_STATES), _BF16),
        ],
        compiler_params=pltpu.CompilerParams(
            dimension_semantics=("arbitrary",), vmem_limit_bytes=VMEM_LIMIT),
        name="ssm",
    )(u_t, wst, kw, cp, wg, bg, a_chunk, a_seg)


def _outproj_body(a_ref, s_ref, x_ref, ag_ref, sg_ref, w_ref, o_ref, sn_ref):
    tm = x_ref.shape[0]
    an = _rms(a_ref[...].astype(_F32), ag_ref[...]).astype(_BF16)
    for t in range(SSM_CHUNK):
        sn_t = _rms(s_ref[t].astype(_F32), sg_ref[...])
        for s in range(SSM_WIDTH // LANES):
            sn_ref[s, pl.ds(t, tm // SSM_CHUNK, stride=SSM_CHUNK), :] = sn_t[:, s * LANES:(s + 1) * LANES]
    sn = jnp.concatenate([sn_ref[s] for s in range(SSM_WIDTH // LANES)], axis=1).astype(_BF16)
    mixed = jnp.concatenate([an, sn], axis=1)
    o_ref[...] = x_ref[...] + jnp.dot(mixed, w_ref[...], preferred_element_type=_F32)


def _outproj(attn, ssm_t, x, a_gain, s_gain, w_out, *, tm=512):
    L, D = x.shape
    row = lambda i: (i, 0)
    fixed = lambda i: (0, 0)
    return pl.pallas_call(
        _outproj_body,
        grid=(L // tm,),
        in_specs=[
            pl.BlockSpec((tm, ATTN_WIDTH), row),
            pl.BlockSpec((SSM_CHUNK, tm // SSM_CHUNK, SSM_WIDTH), lambda i: (0, i, 0)),
            pl.BlockSpec((tm, D), row),
            pl.BlockSpec((1, ATTN_WIDTH), fixed),
            pl.BlockSpec((1, SSM_WIDTH), fixed),
            pl.BlockSpec((ATTN_WIDTH + SSM_WIDTH, D), fixed),
        ],
        out_specs=pl.BlockSpec((tm, D), row),
        out_shape=jax.ShapeDtypeStruct((L, D), _F32),
        scratch_shapes=[pltpu.VMEM((SSM_WIDTH // LANES, tm, LANES), _F32)],
        compiler_params=pltpu.CompilerParams(
            dimension_semantics=("arbitrary",), vmem_limit_bytes=VMEM_LIMIT),
        name="outproj",
    )(attn, ssm_t, x, a_gain.reshape(1, -1), s_gain.reshape(1, -1), w_out)


def _layer(x, pos_f, invf, p):
    L = x.shape[0]
    seg = L // (SSM_SEGMENTS * SSM_CHUNK)
    x = _ffn(x, p['ffn1_norm'], p['ffn1_w_gate'].astype(_BF16), p['ffn1_w_up'].astype(_BF16),
             p['ffn1_w_down'].astype(_BF16))
    q, k2, v2, u_t = _proj(x, p['mix_norm'], p['w_in'].astype(_BF16), pos_f, invf,
                           p['q_norm'], p['k_norm'])
    attn = _attn(q, k2, v2, p['attn_sinks'])
    weights = _ssm_weights(p['ssm_log_dt'], p['ssm_a_re'], p['ssm_a_im'], p['ssm_b_re'], p['ssm_b_im'],
                           p['ssm_c_re'], p['ssm_c_im'], p['ssm_d'], p['ssm_w_glu'], p['ssm_b_glu'], seg)
    ssm_t = _ssm(u_t, weights)
    x = _outproj(attn, ssm_t, x, p['attn_out_norm'], p['ssm_out_norm'], p['w_out'].astype(_BF16))
    return _ffn(x, p['ffn2_norm'], p['ffn2_w_gate'].astype(_BF16), p['ffn2_w_up'].astype(_BF16),
                p['ffn2_w_down'].astype(_BF16))


def kernel(x, positions, ffn1_norm, ffn1_w_gate, ffn1_w_up, ffn1_w_down, mix_norm, w_in, q_norm, k_norm,
           attn_sinks, ssm_log_dt, ssm_a_re, ssm_a_im, ssm_b_re, ssm_b_im, ssm_c_re, ssm_c_im, ssm_d,
           ssm_w_glu, ssm_b_glu, attn_out_norm, ssm_out_norm, w_out, ffn2_norm, ffn2_w_gate, ffn2_w_up,
           ffn2_w_down):
    params = dict(
        ffn1_norm=ffn1_norm, ffn1_w_gate=ffn1_w_gate, ffn1_w_up=ffn1_w_up, ffn1_w_down=ffn1_w_down,
        mix_norm=mix_norm, w_in=w_in, q_norm=q_norm, k_norm=k_norm, attn_sinks=attn_sinks,
        ssm_log_dt=ssm_log_dt, ssm_a_re=ssm_a_re, ssm_a_im=ssm_a_im, ssm_b_re=ssm_b_re, ssm_b_im=ssm_b_im,
        ssm_c_re=ssm_c_re, ssm_c_im=ssm_c_im, ssm_d=ssm_d, ssm_w_glu=ssm_w_glu, ssm_b_glu=ssm_b_glu,
        attn_out_norm=attn_out_norm, ssm_out_norm=ssm_out_norm, w_out=w_out,
        ffn2_norm=ffn2_norm, ffn2_w_gate=ffn2_w_gate, ffn2_w_up=ffn2_w_up, ffn2_w_down=ffn2_w_down)
    depth = ffn1_norm.shape[0]
    half = HEAD_DIM // 2
    inv_freq = ROPE_THETA ** (-jnp.arange(half, dtype=_F32) * 2.0 / HEAD_DIM)
    invf = jnp.tile(inv_freq, LANES // half).reshape(1, LANES)
    outs = []
    for b in range(x.shape[0]):
        xb = x[b]
        pos_f = jnp.broadcast_to(positions[b].astype(_F32)[:, None], (xb.shape[0], LANES))
        for i in range(depth):
            xb = _layer(xb, pos_f, invf, {name: val[i] for name, val in params.items()})
        outs.append(xb)
    return jnp.stack(outs, axis=0)
```

```python
import functools
import math

import jax
import jax.numpy as jnp
from jax import lax
from jax.experimental import pallas as pl
from jax.experimental.pallas import tpu as pltpu

HEAD_DIM = 64
N_Q_HEADS = 16
N_KV_HEADS = 4
ATTN_WIDTH = N_Q_HEADS * HEAD_DIM
KV_WIDTH = N_KV_HEADS * HEAD_DIM
BLOCK = 128
ROPE_THETA = 10000.0
SSM_GROUP = 16
SSM_GROUPS = 64
SSM_STATE = 64
SSM_WIDTH = SSM_GROUP * SSM_GROUPS
FFN_RESIDUAL = 0.5
EPS = 1e-6

LANES = 128
SUBLANES = 8
MXU_DIM = 256
SSM_CHUNK = 8
SSM_SEGMENTS = SUBLANES
SLAB_GROUPS = LANES // SSM_GROUP
SLAB_STATES = SLAB_GROUPS * SSM_STATE
SCAN_UNROLL = 4
VMEM_LIMIT = 56 * 1024 * 1024

_BF16 = jnp.bfloat16
_F32 = jnp.float32
_U32 = jnp.uint32
_HI = lax.Precision.HIGHEST


def _rms(x, gain):
    ms = jnp.mean(x * x, axis=-1, keepdims=True)
    return x * lax.rsqrt(ms + EPS) * gain


def _ffn_up_body(x_ref, gain_ref, wg_ref, wu_ref, a_ref, h_ref):
    @pl.when(pl.program_id(1) == 0)
    def _():
        h_ref[...] = _rms(x_ref[...], gain_ref[...]).astype(_BF16)

    h = h_ref[...]
    g = jnp.dot(h, wg_ref[...], preferred_element_type=_F32)
    u = jnp.dot(h, wu_ref[...], preferred_element_type=_F32)
    a_ref[...] = (g * jax.nn.sigmoid(g) * u).astype(_BF16)


def _ffn_down_body(a_ref, wd_ref, x_ref, o_ref):
    d = jnp.dot(a_ref[...], wd_ref[...], preferred_element_type=_F32)
    o_ref[...] = x_ref[...] + FFN_RESIDUAL * d


def _ffn(x, gain, wg, wu, wd, *, tm=1024, tf=512, tn=512):
    L, D = x.shape
    F = wg.shape[1]
    tm = min(tm, L)
    params = pltpu.CompilerParams(
        dimension_semantics=("arbitrary", "arbitrary"), vmem_limit_bytes=VMEM_LIMIT)
    act = pl.pallas_call(
        _ffn_up_body,
        grid=(L // tm, F // tf),
        in_specs=[
            pl.BlockSpec((tm, D), lambda i, j: (i, 0)),
            pl.BlockSpec((1, D), lambda i, j: (0, 0)),
            pl.BlockSpec((D, tf), lambda i, j: (0, j)),
            pl.BlockSpec((D, tf), lambda i, j: (0, j)),
        ],
        out_specs=pl.BlockSpec((tm, tf), lambda i, j: (i, j)),
        out_shape=jax.ShapeDtypeStruct((L, F), _BF16),
        scratch_shapes=[pltpu.VMEM((tm, D), _BF16)],
        compiler_params=params,
        name="ffn_up",
    )(x, gain.reshape(1, D), wg, wu)
    return pl.pallas_call(
        _ffn_down_body,
        grid=(L // tm, D // tn),
        in_specs=[
            pl.BlockSpec((tm, F), lambda i, j: (i, 0)),
            pl.BlockSpec((F, tn), lambda i, j: (0, j)),
            pl.BlockSpec((tm, tn), lambda i, j: (i, j)),
        ],
        out_specs=pl.BlockSpec((tm, tn), lambda i, j: (i, j)),
        out_shape=jax.ShapeDtypeStruct((L, D), _F32),
        compiler_params=params,
        name="ffn_down",
    )(act, wd, x)


def _proj_body(x_ref, gain_ref, w_ref, pos_ref, invf_ref, qg_ref, kg_ref, pn_ref,
               q_ref, k_ref, v_ref, u_ref, us_ref):
    tm = x_ref.shape[0]
    h = _rms(x_ref[...], gain_ref[...]).astype(_BF16)
    proj = jnp.dot(h, w_ref[...], preferred_element_type=_F32)

    half = HEAD_DIM // 2
    ang = pos_ref[...] * invf_ref[...]
    lane_q = lax.broadcasted_iota(jnp.int32, ang.shape, 1) // half

    def spread(table):
        parts = []
        for qtr in range(LANES // half):
            m = jnp.where(lane_q == qtr, table, 0.0)
            parts.append(m + pltpu.roll(m, half, 1) + pltpu.roll(m, 2 * half, 1) + pltpu.roll(m, 3 * half, 1))
        return jnp.concatenate(parts, axis=0)

    cos = spread(jnp.cos(ang))
    sin = spread(jnp.sin(ang))
    lane = lax.broadcasted_iota(jnp.int32, (tm, LANES), 1)
    first_half = (lane & (HEAD_DIM // 2)) == 0
    low_head = lane < HEAD_DIM
    sin_signed = jnp.where(first_half, -sin, sin)

    def norm_rotary(x4, gain):
        ms4 = jnp.dot((x4 * x4).astype(_BF16), pn_ref[...], preferred_element_type=_F32)
        out = []
        for part in range(2):
            lanes = slice(part * LANES, (part + 1) * LANES)
            y = x4[:, lanes] * lax.rsqrt(ms4[:, lanes] + EPS) * gain
            swapped = jnp.where(first_half, pltpu.roll(y, LANES - HEAD_DIM // 2, 1),
                                pltpu.roll(y, HEAD_DIM // 2, 1))
            out.append(y * cos + swapped * sin_signed)
        return out

    def dup_heads(xc):
        r = pltpu.roll(xc, HEAD_DIM, 1)
        return jnp.where(low_head, xc, r), jnp.where(low_head, r, xc)

    scale = 1.0 / math.sqrt(HEAD_DIM)
    for c in range(ATTN_WIDTH // MXU_DIM):
        for part, qc in enumerate(norm_rotary(proj[:, c * MXU_DIM:(c + 1) * MXU_DIM], qg_ref[...])):
            lanes = slice((2 * c + part) * LANES, (2 * c + part + 1) * LANES)
            q_ref[:, lanes] = (qc * scale).astype(_BF16)
    for c in range(KV_WIDTH // MXU_DIM):
        base = ATTN_WIDTH + c * MXU_DIM
        for part, kc in enumerate(norm_rotary(proj[:, base:base + MXU_DIM], kg_ref[...])):
            ka, kb = dup_heads(kc)
            at = (4 * c + 2 * part) * LANES
            k_ref[:, at:at + LANES] = ka.astype(_BF16)
            k_ref[:, at + LANES:at + 2 * LANES] = kb.astype(_BF16)
    for c in range(KV_WIDTH // LANES):
        base = ATTN_WIDTH + KV_WIDTH + c * LANES
        va, vb = dup_heads(proj[:, base:base + LANES])
        v_ref[:, 2 * c * LANES:(2 * c + 1) * LANES] = va.astype(_BF16)
        v_ref[:, (2 * c + 1) * LANES:(2 * c + 2) * LANES] = vb.astype(_BF16)
    base = ATTN_WIDTH + 2 * KV_WIDTH
    for s in range(SSM_WIDTH // LANES):
        us_ref[s] = proj[:, base + s * LANES:base + (s + 1) * LANES]
    for t in range(SSM_CHUNK):
        for s in range(SSM_WIDTH // LANES):
            rows = us_ref[s, pl.ds(t, tm // SSM_CHUNK, stride=SSM_CHUNK), :]
            u_ref[t, :, s * LANES:(s + 1) * LANES] = rows.astype(_BF16)


def _proj(x, gain, w_in, pos, invf, q_gain, k_gain, *, tm=512):
    L, D = x.shape
    C = w_in.shape[1]
    quarters = LANES // (HEAD_DIM // 2)
    pos_f = pos.astype(_F32).reshape(L // tm, quarters, tm // quarters).transpose(0, 2, 1)
    pos_f = jnp.repeat(pos_f, HEAD_DIM // 2, axis=2)
    head_of_lane = jnp.arange(MXU_DIM) // HEAD_DIM
    pn = jnp.where(head_of_lane[:, None] == head_of_lane[None, :], 1.0 / HEAD_DIM, 0.0).astype(_BF16)
    qg = jnp.tile(q_gain, LANES // HEAD_DIM).reshape(1, LANES)
    kg = jnp.tile(k_gain, LANES // HEAD_DIM).reshape(1, LANES)
    row = lambda i: (i, 0)
    fixed = lambda i: (0, 0)
    return pl.pallas_call(
        _proj_body,
        grid=(L // tm,),
        in_specs=[
            pl.BlockSpec((tm, D), row),
            pl.BlockSpec((1, D), fixed),
            pl.BlockSpec((D, C), fixed),
            pl.BlockSpec((None, tm // quarters, LANES), lambda i: (i, 0, 0)),
            pl.BlockSpec((1, LANES), fixed),
            pl.BlockSpec((1, LANES), fixed),
            pl.BlockSpec((1, LANES), fixed),
            pl.BlockSpec((MXU_DIM, MXU_DIM), fixed),
        ],
        out_specs=[
            pl.BlockSpec((tm, ATTN_WIDTH), row),
            pl.BlockSpec((tm, 2 * KV_WIDTH), row),
            pl.BlockSpec((tm, 2 * KV_WIDTH), row),
            pl.BlockSpec((SSM_CHUNK, tm // SSM_CHUNK, SSM_WIDTH), lambda i: (0, i, 0)),
        ],
        out_shape=[
            jax.ShapeDtypeStruct((L, ATTN_WIDTH), _BF16),
            jax.ShapeDtypeStruct((L, 2 * KV_WIDTH), _BF16),
            jax.ShapeDtypeStruct((L, 2 * KV_WIDTH), _BF16),
            jax.ShapeDtypeStruct((SSM_CHUNK, L // SSM_CHUNK, SSM_WIDTH), _BF16),
        ],
        scratch_shapes=[pltpu.VMEM((SSM_WIDTH // LANES, tm, LANES), _F32)],
        compiler_params=pltpu.CompilerParams(
            dimension_semantics=("arbitrary",), vmem_limit_bytes=VMEM_LIMIT),
        name="proj",
    )(x, gain.reshape(1, D), w_in, pos_f, invf, qg, kg, pn)


def _attn_body(sink_ref, q_ref, kc_ref, vc_ref, kp_ref, vp_ref, o_ref):
    tq = q_ref.shape[0]
    qi = lax.broadcasted_iota(jnp.int32, (BLOCK, BLOCK), 0)
    kj = lax.broadcasted_iota(jnp.int32, (BLOCK, BLOCK), 1)
    from_prev = kj > qi
    has_prev = pl.program_id(0) > 0
    low_head = lax.broadcasted_iota(jnp.int32, (2 * BLOCK, LANES), 1) < HEAD_DIM
    nt = (((1,), (1,)), ((), ()))

    for b in range(tq // BLOCK):
        rows = slice(b * BLOCK, (b + 1) * BLOCK)
        for hk in range(N_KV_HEADS):
            cols = slice(hk * LANES, (hk + 1) * LANES)
            if b == 0:
                kd = jnp.concatenate([kp_ref[:, cols], kc_ref[0:BLOCK, cols]], axis=0)
                vd = jnp.concatenate([vp_ref[:, cols], vc_ref[0:BLOCK, cols]], axis=0)
            else:
                kd = kc_ref[(b - 1) * BLOCK:(b + 1) * BLOCK, cols]
                vd = vc_ref[(b - 1) * BLOCK:(b + 1) * BLOCK, cols]
            zero = jnp.zeros_like(kd)
            k_half = (jnp.where(low_head, kd, zero), jnp.where(low_head, zero, kd))
            v_half = (jnp.where(low_head, vd, zero), jnp.where(low_head, zero, vd))
            for pp in range(2):
                pair = hk * 2 + pp
                qp = q_ref[rows, pair * LANES:(pair + 1) * LANES]
                acc = None
                for half in range(2):
                    sink = sink_ref[pair * 2 + half]
                    s2 = lax.dot_general(qp, k_half[half], nt, preferred_element_type=_F32)
                    s_prev = s2[:, :BLOCK]
                    if b == 0:
                        s_prev = jnp.where(has_prev, s_prev, -jnp.inf)
                    s = jnp.where(from_prev, s_prev, s2[:, BLOCK:])
                    m = jnp.maximum(jnp.max(s, axis=-1, keepdims=True), sink)
                    p = jnp.exp(s - m)
                    den = jnp.sum(p, axis=-1, keepdims=True) + jnp.exp(sink - m)
                    pb = p.astype(_BF16)
                    pz = jnp.zeros_like(pb)
                    p2 = jnp.concatenate([jnp.where(from_prev, pb, pz), jnp.where(from_prev, pz, pb)], axis=1)
                    o = jnp.dot(p2, v_half[half], preferred_element_type=_F32) / den
                    acc = o if acc is None else acc + o
                o_ref[rows, pair * LANES:(pair + 1) * LANES] = acc.astype(_BF16)


def _attn(q, k2, v2, sinks, *, tq=512):
    L = q.shape[0]
    per = tq // BLOCK
    row = lambda i: (i, 0)
    prev = lambda i: (jnp.maximum(i * per - 1, 0), 0)
    return pl.pallas_call(
        _attn_body,
        grid=(L // tq,),
        in_specs=[
            pl.BlockSpec(memory_space=pltpu.SMEM),
            pl.BlockSpec((tq, ATTN_WIDTH), row),
            pl.BlockSpec((tq, 2 * KV_WIDTH), row),
            pl.BlockSpec((tq, 2 * KV_WIDTH), row),
            pl.BlockSpec((BLOCK, 2 * KV_WIDTH), prev),
            pl.BlockSpec((BLOCK, 2 * KV_WIDTH), prev),
        ],
        out_specs=pl.BlockSpec((tq, ATTN_WIDTH), row),
        out_shape=jax.ShapeDtypeStruct((L, ATTN_WIDTH), _BF16),
        compiler_params=pltpu.CompilerParams(
            dimension_semantics=("arbitrary",), vmem_limit_bytes=VMEM_LIMIT),
        name="attn",
    )(sinks, q, k2, v2, k2, v2)


def _ssm_body(u_ref, wst_ref, kw_ref, cp_ref, wg_ref, bg_ref, a_ref, aseg_ref, o_ref,
              lhs_ref, toep_ref, s_ref, xb_ref):
    T = SSM_CHUNK
    n_c = u_ref.shape[1]
    seg = n_c // SSM_SEGMENTS
    pitch = s_ref.shape[1] // SSM_SEGMENTS
    n_state_slabs = SLAB_STATES // LANES
    pair_w = 2 * LANES
    out_w = 2 * MXU_DIM

    for t in range(T):
        lhs_ref[:, t * LANES:(t + 1) * LANES] = u_ref[t]

    toep_ref[...] = jnp.zeros(toep_ref.shape, _BF16)
    for t in range(T):
        for tp in range(t, T):
            toep_ref[t * LANES:(t + 1) * LANES, tp * LANES:(tp + 1) * LANES] = (
                kw_ref[:, (tp - t) * LANES:(tp - t + 1) * LANES])

    per_dot = out_w // LANES
    for nb in range(2 * n_state_slabs // per_dot):
        res = jnp.dot(lhs_ref[...], wst_ref[:, nb * out_w:(nb + 1) * out_w], preferred_element_type=_F32)
        for part in range(per_dot):
            for j in range(SSM_SEGMENTS):
                s_ref[per_dot * nb + part, j * pitch:j * pitch + seg, :] = (
                    res[j * seg:(j + 1) * seg, part * LANES:(part + 1) * LANES])

    shape = (SSM_SEGMENTS, LANES)
    a_re = [jnp.broadcast_to(a_ref[0:1, k * LANES:(k + 1) * LANES], shape) for k in range(n_state_slabs)]
    a_im = [jnp.broadcast_to(a_ref[1:2, k * LANES:(k + 1) * LANES], shape) for k in range(n_state_slabs)]
    g_re = [jnp.broadcast_to(aseg_ref[0:1, k * LANES:(k + 1) * LANES], shape) for k in range(n_state_slabs)]
    g_im = [jnp.broadcast_to(aseg_ref[1:2, k * LANES:(k + 1) * LANES], shape) for k in range(n_state_slabs)]

    def seg_rows(w):
        return pl.ds(w, SSM_SEGMENTS, stride=pitch)

    def advance(w, carry, store):
        out = []
        for k in range(n_state_slabs):
            z_re, z_im = carry[2 * k], carry[2 * k + 1]
            s_re = s_ref[k, seg_rows(w), :]
            s_im = s_ref[n_state_slabs + k, seg_rows(w), :]
            if store:
                s_ref[k, seg_rows(w), :] = z_re
                s_ref[n_state_slabs + k, seg_rows(w), :] = z_im
            out.append(a_re[k] * z_re - a_im[k] * z_im + s_re)
            out.append(a_re[k] * z_im + a_im[k] * z_re + s_im)
        return tuple(out)

    zero = jnp.zeros(shape, _F32)
    ends = lax.fori_loop(0, seg, lambda w, c: advance(w, c, False), (zero,) * (2 * n_state_slabs),
                         unroll=SCAN_UNROLL)

    segidx = lax.broadcasted_iota(jnp.int32, shape, 0)
    init = []
    for k in range(n_state_slabs):
        i_re, i_im = zero, zero
        f_re, f_im = ends[2 * k], ends[2 * k + 1]
        for j in range(SSM_SEGMENTS - 1):
            c_re = g_re[k] * i_re - g_im[k] * i_im + f_re
            c_im = g_re[k] * i_im + g_im[k] * i_re + f_im
            i_re = jnp.where(segidx == j + 1, pltpu.roll(c_re, 1, 0), i_re)
            i_im = jnp.where(segidx == j + 1, pltpu.roll(c_im, 1, 0), i_im)
        init += [i_re, i_im]

    lax.fori_loop(0, seg, lambda w, c: advance(w, c, True), tuple(init), unroll=SCAN_UNROLL)

    for col in range(2 * n_state_slabs):
        for j in range(SSM_SEGMENTS):
            xb_ref[j * seg:(j + 1) * seg, col * LANES:(col + 1) * LANES] = (
                s_ref[col, j * pitch:j * pitch + seg, :].astype(_BF16))

    steps = out_w // LANES
    for i in range(T // steps):
        kk = (i + 1) * out_w
        cols = slice(i * out_w, (i + 1) * out_w)
        y = (jnp.dot(lhs_ref[:, :kk], toep_ref[:kk, cols], preferred_element_type=_F32)
             + jnp.dot(xb_ref[...], cp_ref[:, cols], preferred_element_type=_F32))
        z = jax.nn.gelu(y, approximate=True).astype(_BF16)
        for pp in range(steps // 2):
            gt = jnp.dot(z[:, pp * pair_w:(pp + 1) * pair_w], wg_ref[...],
                         preferred_element_type=_F32) + bg_ref[...]
            out = gt[:, :pair_w] * jax.nn.sigmoid(gt[:, pair_w:])
            t0 = i * steps + 2 * pp
            o_ref[t0] = out[:, :LANES].astype(_BF16)
            o_ref[t0 + 1] = out[:, LANES:].astype(_BF16)


def _complex_power(re, im, n):
    out_re, out_im = None, None
    while n:
        if n & 1:
            if out_re is None:
                out_re, out_im = re, im
            else:
                out_re, out_im = out_re * re - out_im * im, out_re * im + out_im * re
        n >>= 1
        if n:
            re, im = re * re - im * im, 2.0 * re * im
    return out_re, out_im


def _ssm_weights(log_dt, a_re, a_im, b_re, b_im, c_re, c_im, d_skip, w_glu, b_glu, seg):
    G, P, H, T = SSM_GROUPS, SSM_STATE, SSM_GROUP, SSM_CHUNK
    SG, NS = SLAB_GROUPS, SSM_GROUPS // SLAB_GROUPS
    dt = jnp.exp(log_dt)[:, None]
    x_re = a_re * dt
    x_im = a_im * dt
    mag = jnp.exp(x_re)
    ab_re = mag * jnp.cos(x_im)
    ab_im = mag * jnp.sin(x_im)
    inv_den = 1.0 / (a_re * a_re + a_im * a_im)
    nr = ab_re - 1.0
    coef_re = (nr * a_re + ab_im * a_im) * inv_den
    coef_im = (ab_im * a_re - nr * a_im) * inv_den
    bb_re = coef_re[..., None] * b_re - coef_im[..., None] * b_im
    bb_im = coef_re[..., None] * b_im + coef_im[..., None] * b_re

    k = jnp.arange(T + 1, dtype=_F32)[:, None, None]
    pw_mag = jnp.exp(x_re[None] * k)
    pw_re = (pw_mag * jnp.cos(x_im[None] * k)).reshape(T + 1, NS, SG * P)
    pw_im = (pw_mag * jnp.sin(x_im[None] * k)).reshape(T + 1, NS, SG * P)

    group_of_chan = jnp.arange(SG * H) // H
    group_of_state = jnp.arange(SG * P) // P
    same_cs = group_of_chan[:, None] == group_of_state[None, :]
    same_cc = group_of_chan[:, None] == group_of_chan[None, :]

    def chan_by_state(bb):
        t = bb.reshape(NS, SG, P, H).transpose(0, 3, 1, 2).reshape(NS, H, SG * P)
        return jnp.where(same_cs, jnp.tile(t, (1, SG, 1)), 0.0)

    def state_by_chan(c):
        t = c.reshape(NS, SG, H, P).transpose(0, 1, 3, 2).reshape(NS, SG * P, H)
        return jnp.where(same_cs.T, jnp.tile(t, (1, 1, SG)), 0.0)

    bm_re, bm_im = chan_by_state(bb_re), chan_by_state(bb_im)
    cm_re, cm_im = state_by_chan(c_re), state_by_chan(c_im)

    rev_re = pw_re[T - 1::-1].transpose(1, 0, 2)[:, :, None, :]
    rev_im = pw_im[T - 1::-1].transpose(1, 0, 2)[:, :, None, :]
    wst = jnp.concatenate([bm_re[:, None] * rev_re - bm_im[:, None] * rev_im,
                           bm_re[:, None] * rev_im + bm_im[:, None] * rev_re], axis=-1)
    wst = wst.reshape(NS, T * SG * H, 2 * SG * P)

    col_re = pw_re.transpose(1, 2, 0)[..., None]
    col_im = pw_im.transpose(1, 2, 0)[..., None]
    ck_re = cm_re[:, :, None, :] * col_re - cm_im[:, :, None, :] * col_im
    ck_im = cm_re[:, :, None, :] * col_im + cm_im[:, :, None, :] * col_re

    kw = jnp.einsum('sap,spn->san', jnp.concatenate([bm_re, -bm_im], axis=-1),
                    jnp.concatenate([ck_re[:, :, :T], ck_im[:, :, :T]], axis=1).reshape(NS, 2 * SG * P, -1),
                    precision=_HI)
    d_diag = jnp.where(jnp.eye(SG * H, dtype=bool), d_skip.reshape(NS, 1, SG * H), 0.0)
    kw = kw.at[:, :, :SG * H].add(d_diag)

    cp = jnp.concatenate([ck_re[:, :, 1:], -ck_im[:, :, 1:]], axis=1).reshape(NS, 2 * SG * P, T * SG * H)

    def chan_by_chan(w):
        return jnp.where(same_cc, jnp.tile(w.reshape(NS, SG * H, H), (1, 1, SG)), 0.0)

    w_lin, w_gate = chan_by_chan(w_glu[..., :H]), chan_by_chan(w_glu[..., H:])
    zz = jnp.zeros_like(w_lin)
    wg = jnp.concatenate([jnp.concatenate([w_lin, zz, w_gate, zz], axis=-1),
                          jnp.concatenate([zz, w_lin, zz, w_gate], axis=-1)], axis=1)
    b_lin = b_glu[:, :H].reshape(NS, 1, SG * H)
    b_gate = b_glu[:, H:].reshape(NS, 1, SG * H)
    bg = jnp.concatenate([b_lin, b_lin, b_gate, b_gate], axis=-1)

    a_chunk = jnp.stack([pw_re[T], pw_im[T]], axis=1)
    seg_re, seg_im = _complex_power(pw_re[T], pw_im[T], seg)
    a_seg = jnp.stack([seg_re, seg_im], axis=1)
    return (wst.astype(_BF16), kw.astype(_BF16), cp.astype(_BF16), wg.astype(_BF16), bg, a_chunk, a_seg)


def _ssm(u_t, weights):
    T, n_c, W = u_t.shape
    seg = n_c // SSM_SEGMENTS
    pitch = seg + SUBLANES
    ns = W // LANES
    wst, kw, cp, wg, bg, a_chunk, a_seg = weights
    slab = lambda i: (0, 0, i)
    blk = lambda i: (i, 0, 0)
    once = pl.Buffered(1)
    return pl.pallas_call(
        _ssm_body,
        grid=(ns,),
        in_specs=[
            pl.BlockSpec((T, n_c, LANES), slab, pipeline_mode=once),
            pl.BlockSpec((None, T * LANES, 2 * SLAB_STATES), blk, pipeline_mode=once),
            pl.BlockSpec((None, LANES, T * LANES), blk),
            pl.BlockSpec((None, 2 * SLAB_STATES, T * LANES), blk, pipeline_mode=once),
            pl.BlockSpec((None, 2 * LANES, 4 * LANES), blk),
            pl.BlockSpec((None, 1, 4 * LANES), blk),
            pl.BlockSpec((None, 2, SLAB_STATES), blk),
            pl.BlockSpec((None, 2, SLAB_STATES), blk),
        ],
        out_specs=pl.BlockSpec((T, n_c, LANES), slab),
        out_shape=jax.ShapeDtypeStruct((T, n_c, W), _BF16),
        scratch_shapes=[
            pltpu.VMEM((n_c, T * LANES), _BF16),
            pltpu.VMEM((T * LANES, T * LANES), _BF16),
            pltpu.VMEM((2 * SLAB_STATES // LANES, SSM_SEGMENTS * pitch, LANES), _F32),
            pltpu.VMEM((n_c, 2 * SLAB_STATES), _BF16),
        ],
        compiler_params=pltpu.CompilerParams(
            dimension_semantics=("arbitrary",), vmem_limit_bytes=VMEM_LIMIT),
        name="ssm",
    )(u_t, wst, kw, cp, wg, bg, a_chunk, a_seg)


def _outproj_body(a_ref, s_ref, x_ref, ag_ref, sg_ref, w_ref, o_ref, sn_ref):
    tm = x_ref.shape[0]
    an = _rms(a_ref[...].astype(_F32), ag_ref[...]).astype(_BF16)
    for t in range(SSM_CHUNK):
        sn_t = _rms(s_ref[t].astype(_F32), sg_ref[...])
        for s in range(SSM_WIDTH // LANES):
            sn_ref[s, pl.ds(t, tm // SSM_CHUNK, stride=SSM_CHUNK), :] = sn_t[:, s * LANES:(s + 1) * LANES]
    sn = jnp.concatenate([sn_ref[s] for s in range(SSM_WIDTH // LANES)], axis=1).astype(_BF16)
    mixed = jnp.concatenate([an, sn], axis=1)
    o_ref[...] = x_ref[...] + jnp.dot(mixed, w_ref[...], preferred_element_type=_F32)


def _outproj(attn, ssm_t, x, a_gain, s_gain, w_out, *, tm=512):
    L, D = x.shape
    row = lambda i: (i, 0)
    fixed = lambda i: (0, 0)
    return pl.pallas_call(
        _outproj_body,
        grid=(L // tm,),
        in_specs=[
            pl.BlockSpec((tm, ATTN_WIDTH), row),
            pl.BlockSpec((SSM_CHUNK, tm // SSM_CHUNK, SSM_WIDTH), lambda i: (0, i, 0)),
            pl.BlockSpec((tm, D), row),
            pl.BlockSpec((1, ATTN_WIDTH), fixed),
            pl.BlockSpec((1, SSM_WIDTH), fixed),
            pl.BlockSpec((ATTN_WIDTH + SSM_WIDTH, D), fixed),
        ],
        out_specs=pl.BlockSpec((tm, D), row),
        out_shape=jax.ShapeDtypeStruct((L, D), _F32),
        scratch_shapes=[pltpu.VMEM((SSM_WIDTH // LANES, tm, LANES), _F32)],
        compiler_params=pltpu.CompilerParams(
            dimension_semantics=("arbitrary",), vmem_limit_bytes=VMEM_LIMIT),
        name="outproj",
    )(attn, ssm_t, x, a_gain.reshape(1, -1), s_gain.reshape(1, -1), w_out)


def _layer(x, pos, invf, p):
    L = x.shape[0]
    seg = L // (SSM_SEGMENTS * SSM_CHUNK)
    x = _ffn(x, p['ffn1_norm'], p['ffn1_w_gate'].astype(_BF16), p['ffn1_w_up'].astype(_BF16),
             p['ffn1_w_down'].astype(_BF16))
    q, k2, v2, u_t = _proj(x, p['mix_norm'], p['w_in'].astype(_BF16), pos, invf,
                           p['q_norm'], p['k_norm'])
    attn = _attn(q, k2, v2, p['attn_sinks'])
    weights = _ssm_weights(p['ssm_log_dt'], p['ssm_a_re'], p['ssm_a_im'], p['ssm_b_re'], p['ssm_b_im'],
                           p['ssm_c_re'], p['ssm_c_im'], p['ssm_d'], p['ssm_w_glu'], p['ssm_b_glu'], seg)
    ssm_t = _ssm(u_t, weights)
    x = _outproj(attn, ssm_t, x, p['attn_out_norm'], p['ssm_out_norm'], p['w_out'].astype(_BF16))
    return _ffn(x, p['ffn2_norm'], p['ffn2_w_gate'].astype(_BF16), p['ffn2_w_up'].astype(_BF16),
                p['ffn2_w_down'].astype(_BF16))


def kernel(x, positions, ffn1_norm, ffn1_w_gate, ffn1_w_up, ffn1_w_down, mix_norm, w_in, q_norm, k_norm,
           attn_sinks, ssm_log_dt, ssm_a_re, ssm_a_im, ssm_b_re, ssm_b_im, ssm_c_re, ssm_c_im, ssm_d,
           ssm_w_glu, ssm_b_glu, attn_out_norm, ssm_out_norm, w_out, ffn2_norm, ffn2_w_gate, ffn2_w_up,
           ffn2_w_down):
    params = dict(
        ffn1_norm=ffn1_norm, ffn1_w_gate=ffn1_w_gate, ffn1_w_up=ffn1_w_up, ffn1_w_down=ffn1_w_down,
        mix_norm=mix_norm, w_in=w_in, q_norm=q_norm, k_norm=k_norm, attn_sinks=attn_sinks,
        ssm_log_dt=ssm_log_dt, ssm_a_re=ssm_a_re, ssm_a_im=ssm_a_im, ssm_b_re=ssm_b_re, ssm_b_im=ssm_b_im,
        ssm_c_re=ssm_c_re, ssm_c_im=ssm_c_im, ssm_d=ssm_d, ssm_w_glu=ssm_w_glu, ssm_b_glu=ssm_b_glu,
        attn_out_norm=attn_out_norm, ssm_out_norm=ssm_out_norm, w_out=w_out,
        ffn2_norm=ffn2_norm, ffn2_w_gate=ffn2_w_gate, ffn2_w_up=ffn2_w_up, ffn2_w_down=ffn2_w_down)
    depth = ffn1_norm.shape[0]
    half = HEAD_DIM // 2
    inv_freq = ROPE_THETA ** (-jnp.arange(half, dtype=_F32) * 2.0 / HEAD_DIM)
    invf = jnp.tile(inv_freq, LANES // half).reshape(1, LANES)
    outs = []
    for b in range(x.shape[0]):
        xb = x[b]
        for i in range(depth):
            xb = _layer(xb, positions[b], invf, {name: val[i] for name, val in params.items()})
        outs.append(xb)
    return jnp.stack(outs, axis=0)
```

```python
import functools
import math

import jax
import jax.numpy as jnp
from jax import lax
from jax.experimental import pallas as pl
from jax.experimental.pallas import tpu as pltpu

HEAD_DIM = 64
N_Q_HEADS = 16
N_KV_HEADS = 4
ATTN_WIDTH = N_Q_HEADS * HEAD_DIM
KV_WIDTH = N_KV_HEADS * HEAD_DIM
BLOCK = 128
ROPE_THETA = 10000.0
SSM_GROUP = 16
SSM_GROUPS = 64
SSM_STATE = 64
SSM_WIDTH = SSM_GROUP * SSM_GROUPS
FFN_RESIDUAL = 0.5
EPS = 1e-6

LANES = 128
SUBLANES = 8
MXU_DIM = 256
SSM_CHUNK = 8
SSM_SEGMENTS = SUBLANES
SLAB_GROUPS = LANES // SSM_GROUP
SLAB_STATES = SLAB_GROUPS * SSM_STATE
SCAN_UNROLL = 4
VMEM_LIMIT = 56 * 1024 * 1024

_BF16 = jnp.bfloat16
_F32 = jnp.float32
_U32 = jnp.uint32
_HI = lax.Precision.HIGHEST


def _rms(x, gain):
    ms = jnp.mean(x * x, axis=-1, keepdims=True)
    return x * lax.rsqrt(ms + EPS) * gain


def _ffn_up_body(cast_every_step, x_ref, gain_ref, wg_ref, wu_ref, *refs):
    n = len(cast_every_step)
    src_refs, a_ref, dst_refs, h_ref = refs[:n], refs[n], refs[n + 1:2 * n + 1], refs[2 * n + 1]
    first = pl.program_id(1) == 0

    @pl.when(first)
    def _():
        h_ref[...] = _rms(x_ref[...], gain_ref[...]).astype(_BF16)

    h = h_ref[...]
    g = jnp.dot(h, wg_ref[...], preferred_element_type=_F32)
    u = jnp.dot(h, wu_ref[...], preferred_element_type=_F32)
    a_ref[...] = (g * jax.nn.sigmoid(g) * u).astype(_BF16)

    for every_step, src, dst in zip(cast_every_step, src_refs, dst_refs):
        if every_step:
            dst[...] = src[...].astype(_BF16)
        else:
            @pl.when(first)
            def _(src=src, dst=dst):
                dst[...] = src[...].astype(_BF16)


def _ffn_down_body(a_ref, wd_ref, x_ref, o_ref):
    d = jnp.dot(a_ref[...], wd_ref[...], preferred_element_type=_F32)
    o_ref[...] = x_ref[...] + FFN_RESIDUAL * d


def _cast_spec(shape, ni, nj):
    R, C = shape
    pack = 2 * SUBLANES
    if R % ni == 0 and (R // ni) % pack == 0 and C % nj == 0 and (C // nj) % LANES == 0:
        return pl.BlockSpec((R // ni, C // nj), lambda i, j: (i, j)), True
    if R % (ni * nj) == 0 and (R // (ni * nj)) % pack == 0:
        return pl.BlockSpec((R // (ni * nj), C), lambda i, j: (i * nj + j, 0)), True
    assert R % ni == 0 and (R // ni) % pack == 0, shape
    return pl.BlockSpec((R // ni, C), lambda i, j: (i, 0)), False


def _ffn_up(x, gain, wg, wu, to_cast=(), *, tm=1024, tf=512):
    L, D = x.shape
    F = wg.shape[1]
    tm = min(tm, L)
    ni, nj = L // tm, F // tf
    cast_specs = [_cast_spec(w.shape, ni, nj) for w in to_cast]
    out = pl.pallas_call(
        functools.partial(_ffn_up_body, tuple(every for _, every in cast_specs)),
        grid=(ni, nj),
        in_specs=[
            pl.BlockSpec((tm, D), lambda i, j: (i, 0)),
            pl.BlockSpec((1, D), lambda i, j: (0, 0)),
            pl.BlockSpec((D, tf), lambda i, j: (0, j)),
            pl.BlockSpec((D, tf), lambda i, j: (0, j)),
        ] + [spec for spec, _ in cast_specs],
        out_specs=[pl.BlockSpec((tm, tf), lambda i, j: (i, j))] + [spec for spec, _ in cast_specs],
        out_shape=[jax.ShapeDtypeStruct((L, F), _BF16)]
        + [jax.ShapeDtypeStruct(w.shape, _BF16) for w in to_cast],
        scratch_shapes=[pltpu.VMEM((tm, D), _BF16)],
        compiler_params=pltpu.CompilerParams(
            dimension_semantics=("arbitrary", "arbitrary"), vmem_limit_bytes=VMEM_LIMIT),
        name="ffn_up",
    )(x, gain.reshape(1, D), wg, wu, *to_cast)
    return out[0], out[1:]


def _ffn_down(act, wd, x, *, tm=1024, tn=512):
    L, D = x.shape
    F = act.shape[1]
    tm = min(tm, L)
    return pl.pallas_call(
        _ffn_down_body,
        grid=(L // tm, D // tn),
        in_specs=[
            pl.BlockSpec((tm, F), lambda i, j: (i, 0)),
            pl.BlockSpec((F, tn), lambda i, j: (0, j)),
            pl.BlockSpec((tm, tn), lambda i, j: (i, j)),
        ],
        out_specs=pl.BlockSpec((tm, tn), lambda i, j: (i, j)),
        out_shape=jax.ShapeDtypeStruct((L, D), _F32),
        compiler_params=pltpu.CompilerParams(
            dimension_semantics=("arbitrary", "arbitrary"), vmem_limit_bytes=VMEM_LIMIT),
        name="ffn_down",
    )(act, wd, x)


def _proj_body(x_ref, gain_ref, w_ref, pos_ref, invf_ref, qg_ref, kg_ref, pn_ref,
               q_ref, k_ref, v_ref, u_ref, us_ref):
    tm = x_ref.shape[0]
    h = _rms(x_ref[...], gain_ref[...]).astype(_BF16)
    proj = jnp.dot(h, w_ref[...], preferred_element_type=_F32)

    half = HEAD_DIM // 2
    ang = pos_ref[...] * invf_ref[...]
    lane_q = lax.broadcasted_iota(jnp.int32, ang.shape, 1) // half

    def spread(table):
        parts = []
        for qtr in range(LANES // half):
            m = jnp.where(lane_q == qtr, table, 0.0)
            parts.append(m + pltpu.roll(m, half, 1) + pltpu.roll(m, 2 * half, 1) + pltpu.roll(m, 3 * half, 1))
        return jnp.concatenate(parts, axis=0)

    cos = spread(jnp.cos(ang))
    sin = spread(jnp.sin(ang))
    lane = lax.broadcasted_iota(jnp.int32, (tm, LANES), 1)
    first_half = (lane & (HEAD_DIM // 2)) == 0
    low_head = lane < HEAD_DIM
    sin_signed = jnp.where(first_half, -sin, sin)

    def norm_rotary(x4, gain):
        ms4 = jnp.dot((x4 * x4).astype(_BF16), pn_ref[...], preferred_element_type=_F32)
        out = []
        for part in range(2):
            lanes = slice(part * LANES, (part + 1) * LANES)
            y = x4[:, lanes] * lax.rsqrt(ms4[:, lanes] + EPS) * gain
            swapped = jnp.where(first_half, pltpu.roll(y, LANES - HEAD_DIM // 2, 1),
                                pltpu.roll(y, HEAD_DIM // 2, 1))
            out.append(y * cos + swapped * sin_signed)
        return out

    def dup_heads(xc):
        r = pltpu.roll(xc, HEAD_DIM, 1)
        return jnp.where(low_head, xc, r), jnp.where(low_head, r, xc)

    scale = 1.0 / math.sqrt(HEAD_DIM)
    for c in range(ATTN_WIDTH // MXU_DIM):
        for part, qc in enumerate(norm_rotary(proj[:, c * MXU_DIM:(c + 1) * MXU_DIM], qg_ref[...])):
            lanes = slice((2 * c + part) * LANES, (2 * c + part + 1) * LANES)
            q_ref[:, lanes] = (qc * scale).astype(_BF16)
    for c in range(KV_WIDTH // MXU_DIM):
        base = ATTN_WIDTH + c * MXU_DIM
        for part, kc in enumerate(norm_rotary(proj[:, base:base + MXU_DIM], kg_ref[...])):
            ka, kb = dup_heads(kc)
            at = (4 * c + 2 * part) * LANES
            k_ref[:, at:at + LANES] = ka.astype(_BF16)
            k_ref[:, at + LANES:at + 2 * LANES] = kb.astype(_BF16)
    for c in range(KV_WIDTH // LANES):
        base = ATTN_WIDTH + KV_WIDTH + c * LANES
        va, vb = dup_heads(proj[:, base:base + LANES])
        v_ref[:, 2 * c * LANES:(2 * c + 1) * LANES] = va.astype(_BF16)
        v_ref[:, (2 * c + 1) * LANES:(2 * c + 2) * LANES] = vb.astype(_BF16)
    base = ATTN_WIDTH + 2 * KV_WIDTH
    for s in range(SSM_WIDTH // LANES):
        us_ref[s] = proj[:, base + s * LANES:base + (s + 1) * LANES]
    for t in range(SSM_CHUNK):
        for s in range(SSM_WIDTH // LANES):
            rows = us_ref[s, pl.ds(t, tm // SSM_CHUNK, stride=SSM_CHUNK), :]
            u_ref[t, :, s * LANES:(s + 1) * LANES] = rows.astype(_BF16)


def _proj(x, gain, w_in, pos, invf, q_gain, k_gain, *, tm=512):
    L, D = x.shape
    C = w_in.shape[1]
    quarters = LANES // (HEAD_DIM // 2)
    pos_f = pos.astype(_F32).reshape(L // tm, quarters, tm // quarters).transpose(0, 2, 1)
    pos_f = jnp.repeat(pos_f, HEAD_DIM // 2, axis=2)
    head_of_lane = jnp.arange(MXU_DIM) // HEAD_DIM
    pn = jnp.where(head_of_lane[:, None] == head_of_lane[None, :], 1.0 / HEAD_DIM, 0.0).astype(_BF16)
    qg = jnp.tile(q_gain, LANES // HEAD_DIM).reshape(1, LANES)
    kg = jnp.tile(k_gain, LANES // HEAD_DIM).reshape(1, LANES)
    row = lambda i: (i, 0)
    fixed = lambda i: (0, 0)
    return pl.pallas_call(
        _proj_body,
        grid=(L // tm,),
        in_specs=[
            pl.BlockSpec((tm, D), row),
            pl.BlockSpec((1, D), fixed),
            pl.BlockSpec((D, C), fixed),
            pl.BlockSpec((None, tm // quarters, LANES), lambda i: (i, 0, 0)),
            pl.BlockSpec((1, LANES), fixed),
            pl.BlockSpec((1, LANES), fixed),
            pl.BlockSpec((1, LANES), fixed),
            pl.BlockSpec((MXU_DIM, MXU_DIM), fixed),
        ],
        out_specs=[
            pl.BlockSpec((tm, ATTN_WIDTH), row),
            pl.BlockSpec((tm, 2 * KV_WIDTH), row),
            pl.BlockSpec((tm, 2 * KV_WIDTH), row),
            pl.BlockSpec((SSM_CHUNK, tm // SSM_CHUNK, SSM_WIDTH), lambda i: (0, i, 0)),
        ],
        out_shape=[
            jax.ShapeDtypeStruct((L, ATTN_WIDTH), _BF16),
            jax.ShapeDtypeStruct((L, 2 * KV_WIDTH), _BF16),
            jax.ShapeDtypeStruct((L, 2 * KV_WIDTH), _BF16),
            jax.ShapeDtypeStruct((SSM_CHUNK, L // SSM_CHUNK, SSM_WIDTH), _BF16),
        ],
        scratch_shapes=[pltpu.VMEM((SSM_WIDTH // LANES, tm, LANES), _F32)],
        compiler_params=pltpu.CompilerParams(
            dimension_semantics=("arbitrary",), vmem_limit_bytes=VMEM_LIMIT),
        name="proj",
    )(x, gain.reshape(1, D), w_in, pos_f, invf, qg, kg, pn)


def _attn_body(sink_ref, q_ref, kc_ref, vc_ref, kp_ref, vp_ref, o_ref):
    tq = q_ref.shape[0]
    qi = lax.broadcasted_iota(jnp.int32, (BLOCK, BLOCK), 0)
    kj = lax.broadcasted_iota(jnp.int32, (BLOCK, BLOCK), 1)
    from_prev = kj > qi
    has_prev = pl.program_id(0) > 0
    low_head = lax.broadcasted_iota(jnp.int32, (2 * BLOCK, LANES), 1) < HEAD_DIM
    nt = (((1,), (1,)), ((), ()))

    for b in range(tq // BLOCK):
        rows = slice(b * BLOCK, (b + 1) * BLOCK)
        for hk in range(N_KV_HEADS):
            cols = slice(hk * LANES, (hk + 1) * LANES)
            if b == 0:
                kd = jnp.concatenate([kp_ref[:, cols], kc_ref[0:BLOCK, cols]], axis=0)
                vd = jnp.concatenate([vp_ref[:, cols], vc_ref[0:BLOCK, cols]], axis=0)
            else:
                kd = kc_ref[(b - 1) * BLOCK:(b + 1) * BLOCK, cols]
                vd = vc_ref[(b - 1) * BLOCK:(b + 1) * BLOCK, cols]
            zero = jnp.zeros_like(kd)
            k_half = (jnp.where(low_head, kd, zero), jnp.where(low_head, zero, kd))
            v_half = (jnp.where(low_head, vd, zero), jnp.where(low_head, zero, vd))
            for pp in range(2):
                pair = hk * 2 + pp
                qp = q_ref[rows, pair * LANES:(pair + 1) * LANES]
                acc = None
                for half in range(2):
                    sink = sink_ref[pair * 2 + half]
                    s2 = lax.dot_general(qp, k_half[half], nt, preferred_element_type=_F32)
                    s_prev = s2[:, :BLOCK]
                    if b == 0:
                        s_prev = jnp.where(has_prev, s_prev, -jnp.inf)
                    s = jnp.where(from_prev, s_prev, s2[:, BLOCK:])
                    m = jnp.maximum(jnp.max(s, axis=-1, keepdims=True), sink)
                    p = jnp.exp(s - m)
                    den = jnp.sum(p, axis=-1, keepdims=True) + jnp.exp(sink - m)
                    pb = p.astype(_BF16)
                    pz = jnp.zeros_like(pb)
                    p2 = jnp.concatenate([jnp.where(from_prev, pb, pz), jnp.where(from_prev, pz, pb)], axis=1)
                    o = jnp.dot(p2, v_half[half], preferred_element_type=_F32) / den
                    acc = o if acc is None else acc + o
                o_ref[rows, pair * LANES:(pair + 1) * LANES] = acc.astype(_BF16)


def _attn(q, k2, v2, sinks, *, tq=512):
    L = q.shape[0]
    per = tq // BLOCK
    row = lambda i: (i, 0)
    prev = lambda i: (jnp.maximum(i * per - 1, 0), 0)
    return pl.pallas_call(
        _attn_body,
        grid=(L // tq,),
        in_specs=[
            pl.BlockSpec(memory_space=pltpu.SMEM),
            pl.BlockSpec((tq, ATTN_WIDTH), row),
            pl.BlockSpec((tq, 2 * KV_WIDTH), row),
            pl.BlockSpec((tq, 2 * KV_WIDTH), row),
            pl.BlockSpec((BLOCK, 2 * KV_WIDTH), prev),
            pl.BlockSpec((BLOCK, 2 * KV_WIDTH), prev),
        ],
        out_specs=pl.BlockSpec((tq, ATTN_WIDTH), row),
        out_shape=jax.ShapeDtypeStruct((L, ATTN_WIDTH), _BF16),
        compiler_params=pltpu.CompilerParams(
            dimension_semantics=("arbitrary",), vmem_limit_bytes=VMEM_LIMIT),
        name="attn",
    )(sinks, q, k2, v2, k2, v2)


def _ssm_body(u_ref, wst_ref, kw_ref, cp_ref, wg_ref, bg_ref, a_ref, aseg_ref, o_ref,
              lhs_ref, toep_ref, s_ref, xb_ref):
    T = SSM_CHUNK
    n_c = u_ref.shape[1]
    seg = n_c // SSM_SEGMENTS
    pitch = s_ref.shape[1] // SSM_SEGMENTS
    n_state_slabs = SLAB_STATES // LANES
    pair_w = 2 * LANES
    out_w = 2 * MXU_DIM

    for t in range(T):
        lhs_ref[:, t * LANES:(t + 1) * LANES] = u_ref[t]

    toep_ref[...] = jnp.zeros(toep_ref.shape, _BF16)
    for t in range(T):
        for tp in range(t, T):
            toep_ref[t * LANES:(t + 1) * LANES, tp * LANES:(tp + 1) * LANES] = (
                kw_ref[:, (tp - t) * LANES:(tp - t + 1) * LANES])

    per_dot = out_w // LANES
    for nb in range(2 * n_state_slabs // per_dot):
        res = jnp.dot(lhs_ref[...], wst_ref[:, nb * out_w:(nb + 1) * out_w], preferred_element_type=_F32)
        for part in range(per_dot):
            for j in range(SSM_SEGMENTS):
                s_ref[per_dot * nb + part, j * pitch:j * pitch + seg, :] = (
                    res[j * seg:(j + 1) * seg, part * LANES:(part + 1) * LANES])

    shape = (SSM_SEGMENTS, LANES)
    a_re = [jnp.broadcast_to(a_ref[0:1, k * LANES:(k + 1) * LANES], shape) for k in range(n_state_slabs)]
    a_im = [jnp.broadcast_to(a_ref[1:2, k * LANES:(k + 1) * LANES], shape) for k in range(n_state_slabs)]
    g_re = [jnp.broadcast_to(aseg_ref[0:1, k * LANES:(k + 1) * LANES], shape) for k in range(n_state_slabs)]
    g_im = [jnp.broadcast_to(aseg_ref[1:2, k * LANES:(k + 1) * LANES], shape) for k in range(n_state_slabs)]

    def seg_rows(w):
        return pl.ds(w, SSM_SEGMENTS, stride=pitch)

    def advance(w, carry, store):
        out = []
        for k in range(n_state_slabs):
            z_re, z_im = carry[2 * k], carry[2 * k + 1]
            s_re = s_ref[k, seg_rows(w), :]
            s_im = s_ref[n_state_slabs + k, seg_rows(w), :]
            if store:
                s_ref[k, seg_rows(w), :] = z_re
                s_ref[n_state_slabs + k, seg_rows(w), :] = z_im
            out.append(a_re[k] * z_re - a_im[k] * z_im + s_re)
            out.append(a_re[k] * z_im + a_im[k] * z_re + s_im)
        return tuple(out)

    zero = jnp.zeros(shape, _F32)
    ends = lax.fori_loop(0, seg, lambda w, c: advance(w, c, False), (zero,) * (2 * n_state_slabs),
                         unroll=SCAN_UNROLL)

    segidx = lax.broadcasted_iota(jnp.int32, shape, 0)
    init = []
    for k in range(n_state_slabs):
        i_re, i_im = zero, zero
        f_re, f_im = ends[2 * k], ends[2 * k + 1]
        for j in range(SSM_SEGMENTS - 1):
            c_re = g_re[k] * i_re - g_im[k] * i_im + f_re
            c_im = g_re[k] * i_im + g_im[k] * i_re + f_im
            i_re = jnp.where(segidx == j + 1, pltpu.roll(c_re, 1, 0), i_re)
            i_im = jnp.where(segidx == j + 1, pltpu.roll(c_im, 1, 0), i_im)
        init += [i_re, i_im]

    lax.fori_loop(0, seg, lambda w, c: advance(w, c, True), tuple(init), unroll=SCAN_UNROLL)

    for col in range(2 * n_state_slabs):
        for j in range(SSM_SEGMENTS):
            xb_ref[j * seg:(j + 1) * seg, col * LANES:(col + 1) * LANES] = (
                s_ref[col, j * pitch:j * pitch + seg, :].astype(_BF16))

    steps = out_w // LANES
    for i in range(T // steps):
        kk = (i + 1) * out_w
        cols = slice(i * out_w, (i + 1) * out_w)
        y = (jnp.dot(lhs_ref[:, :kk], toep_ref[:kk, cols], preferred_element_type=_F32)
             + jnp.dot(xb_ref[...], cp_ref[:, cols], preferred_element_type=_F32))
        z = jax.nn.gelu(y, approximate=True).astype(_BF16)
        for pp in range(steps // 2):
            gt = jnp.dot(z[:, pp * pair_w:(pp + 1) * pair_w], wg_ref[...],
                         preferred_element_type=_F32) + bg_ref[...]
            out = gt[:, :pair_w] * jax.nn.sigmoid(gt[:, pair_w:])
            t0 = i * steps + 2 * pp
            o_ref[t0] = out[:, :LANES].astype(_BF16)
            o_ref[t0 + 1] = out[:, LANES:].astype(_BF16)


def _complex_power(re, im, n):
    out_re, out_im = None, None
    while n:
        if n & 1:
            if out_re is None:
                out_re, out_im = re, im
            else:
                out_re, out_im = out_re * re - out_im * im, out_re * im + out_im * re
        n >>= 1
        if n:
            re, im = re * re - im * im, 2.0 * re * im
    return out_re, out_im


def _ssm_prep_body(seg, rows_ref, bt_ref, ct_ref, d_ref, wt_ref, bglu_ref,
                   wst_ref, kw_ref, cp_ref, wg_ref, bg_ref, ach_ref, aseg_ref):
    T, H = SSM_CHUNK, SSM_GROUP
    a_re, a_im = rows_ref[0:1, :], rows_ref[1:2, :]
    dt = jnp.exp(rows_ref[2:3, :])
    mag = jnp.exp(a_re * dt)
    ab_re = mag * jnp.cos(a_im * dt)
    ab_im = mag * jnp.sin(a_im * dt)
    inv_den = 1.0 / (a_re * a_re + a_im * a_im)
    nr = ab_re - 1.0
    coef_re = (nr * a_re + ab_im * a_im) * inv_den
    coef_im = (ab_im * a_re - nr * a_im) * inv_den

    def block_diag(t16, shape):
        row_g = lax.broadcasted_iota(jnp.int32, shape, 0) // H
        col_g = lax.broadcasted_iota(jnp.int32, shape, 1) // (shape[1] // SLAB_GROUPS)
        return jnp.where(row_g == col_g, jnp.concatenate([t16] * SLAB_GROUPS, axis=0), 0.0)

    cs = (LANES, SLAB_STATES)
    b_re, b_im = block_diag(bt_ref[0], cs), block_diag(bt_ref[1], cs)
    bb_re = coef_re * b_re - coef_im * b_im
    bb_im = coef_re * b_im + coef_im * b_re
    ck_re = jnp.transpose(block_diag(ct_ref[0], cs))
    ck_im = jnp.transpose(block_diag(ct_ref[1], cs))
    col_re = jnp.transpose(jnp.broadcast_to(ab_re, cs))
    col_im = jnp.transpose(jnp.broadcast_to(ab_im, cs))

    pw = [(jnp.ones_like(ab_re), jnp.zeros_like(ab_im))]
    for _ in range(T):
        pr, pi = pw[-1]
        pw.append((pr * ab_re - pi * ab_im, pr * ab_im + pi * ab_re))

    for t in range(T):
        pr, pi = pw[T - 1 - t]
        wst_ref[t * LANES:(t + 1) * LANES, :SLAB_STATES] = (bb_re * pr - bb_im * pi).astype(_BF16)
        wst_ref[t * LANES:(t + 1) * LANES, SLAB_STATES:] = (bb_re * pi + bb_im * pr).astype(_BF16)

    lhs = jnp.concatenate([bb_re, -bb_im], axis=1)
    eye = (lax.broadcasted_iota(jnp.int32, (LANES, LANES), 0)
           == lax.broadcasted_iota(jnp.int32, (LANES, LANES), 1))
    for k in range(T + 1):
        if k < T:
            kern = jnp.dot(lhs, jnp.concatenate([ck_re, ck_im], axis=0), precision=_HI,
                           preferred_element_type=_F32)
            if k == 0:
                kern = kern + jnp.where(eye, d_ref[...], 0.0)
            kw_ref[:, k * LANES:(k + 1) * LANES] = kern.astype(_BF16)
        if k > 0:
            cp_ref[:SLAB_STATES, (k - 1) * LANES:k * LANES] = ck_re.astype(_BF16)
            cp_ref[SLAB_STATES:, (k - 1) * LANES:k * LANES] = (-ck_im).astype(_BF16)
        ck_re, ck_im = ck_re * col_re - ck_im * col_im, ck_re * col_im + ck_im * col_re

    cc = (LANES, LANES)
    w_lin = jnp.transpose(block_diag(wt_ref[0], cc)).astype(_BF16)
    w_gate = jnp.transpose(block_diag(wt_ref[1], cc)).astype(_BF16)
    wg_ref[...] = jnp.zeros(wg_ref.shape, _BF16)
    for t in range(2):
        wg_ref[t * LANES:(t + 1) * LANES, t * LANES:(t + 1) * LANES] = w_lin
        wg_ref[t * LANES:(t + 1) * LANES, (2 + t) * LANES:(3 + t) * LANES] = w_gate
    bg_ref[...] = jnp.concatenate([bglu_ref[0:1, :], bglu_ref[0:1, :], bglu_ref[1:2, :], bglu_ref[1:2, :]],
                                  axis=1)

    ach_ref[0:1, :], ach_ref[1:2, :] = pw[T]
    aseg_ref[0:1, :], aseg_ref[1:2, :] = _complex_power(pw[T][0], pw[T][1], seg)


def _ssm_weights(log_dt, a_re, a_im, b_re, b_im, c_re, c_im, d_skip, w_glu, b_glu, seg):
    P, H, T = SSM_STATE, SSM_GROUP, SSM_CHUNK
    SG, NS = SLAB_GROUPS, SSM_GROUPS // SLAB_GROUPS
    rows = jnp.stack([a_re.reshape(NS, SG * P), a_im.reshape(NS, SG * P),
                      jnp.repeat(log_dt, P).reshape(NS, SG * P)], axis=1)

    def per_group_rows(w, lead):
        t = w.reshape((NS, SG) + w.shape[1:])
        t = jnp.moveaxis(t, 2 + lead, 1)
        return t.reshape(NS, t.shape[1], -1)

    bt = jnp.stack([per_group_rows(b_re, 1), per_group_rows(b_im, 1)], axis=1)
    ct = jnp.stack([per_group_rows(c_re, 0), per_group_rows(c_im, 0)], axis=1)
    wt = jnp.stack([per_group_rows(w_glu[..., :H], 1), per_group_rows(w_glu[..., H:], 1)], axis=1)
    bglu = jnp.stack([b_glu[:, :H].reshape(NS, SG * H), b_glu[:, H:].reshape(NS, SG * H)], axis=1)
    d = d_skip.reshape(NS, 1, SG * H)

    blk3 = lambda i: (i, 0, 0)
    blk4 = lambda i: (i, 0, 0, 0)
    return pl.pallas_call(
        functools.partial(_ssm_prep_body, seg),
        grid=(NS,),
        in_specs=[
            pl.BlockSpec((None, 3, SG * P), blk3),
            pl.BlockSpec((None, 2, H, SG * P), blk4),
            pl.BlockSpec((None, 2, H, SG * P), blk4),
            pl.BlockSpec((None, 1, SG * H), blk3),
            pl.BlockSpec((None, 2, H, SG * H), blk4),
            pl.BlockSpec((None, 2, SG * H), blk3),
        ],
        out_specs=[
            pl.BlockSpec((None, T * LANES, 2 * SLAB_STATES), blk3),
            pl.BlockSpec((None, LANES, T * LANES), blk3),
            pl.BlockSpec((None, 2 * SLAB_STATES, T * LANES), blk3),
            pl.BlockSpec((None, 2 * LANES, 4 * LANES), blk3),
            pl.BlockSpec((None, 1, 4 * LANES), blk3),
            pl.BlockSpec((None, 2, SLAB_STATES), blk3),
            pl.BlockSpec((None, 2, SLAB_STATES), blk3),
        ],
        out_shape=[
            jax.ShapeDtypeStruct((NS, T * LANES, 2 * SLAB_STATES), _BF16),
            jax.ShapeDtypeStruct((NS, LANES, T * LANES), _BF16),
            jax.ShapeDtypeStruct((NS, 2 * SLAB_STATES, T * LANES), _BF16),
            jax.ShapeDtypeStruct((NS, 2 * LANES, 4 * LANES), _BF16),
            jax.ShapeDtypeStruct((NS, 1, 4 * LANES), _F32),
            jax.ShapeDtypeStruct((NS, 2, SLAB_STATES), _F32),
            jax.ShapeDtypeStruct((NS, 2, SLAB_STATES), _F32),
        ],
        compiler_params=pltpu.CompilerParams(
            dimension_semantics=("arbitrary",), vmem_limit_bytes=VMEM_LIMIT),
        name="ssm_prep",
    )(rows, bt, ct, d, wt, bglu)


def _ssm(u_t, weights):
    T, n_c, W = u_t.shape
    seg = n_c // SSM_SEGMENTS
    pitch = seg + SUBLANES
    ns = W // LANES
    wst, kw, cp, wg, bg, a_chunk, a_seg = weights
    slab = lambda i: (0, 0, i)
    blk = lambda i: (i, 0, 0)
    once = pl.Buffered(1)
    return pl.pallas_call(
        _ssm_body,
        grid=(ns,),
        in_specs=[
            pl.BlockSpec((T, n_c, LANES), slab, pipeline_mode=once),
            pl.BlockSpec((None, T * LANES, 2 * SLAB_STATES), blk, pipeline_mode=once),
            pl.BlockSpec((None, LANES, T * LANES), blk),
            pl.BlockSpec((None, 2 * SLAB_STATES, T * LANES), blk, pipeline_mode=once),
            pl.BlockSpec((None, 2 * LANES, 4 * LANES), blk),
            pl.BlockSpec((None, 1, 4 * LANES), blk),
            pl.BlockSpec((None, 2, SLAB_STATES), blk),
            pl.BlockSpec((None, 2, SLAB_STATES), blk),
        ],
        out_specs=pl.BlockSpec((T, n_c, LANES), slab),
        out_shape=jax.ShapeDtypeStruct((T, n_c, W), _BF16),
        scratch_shapes=[
            pltpu.VMEM((n_c, T * LANES), _BF16),
            pltpu.VMEM((T * LANES, T * LANES), _BF16),
            pltpu.VMEM((2 * SLAB_STATES // LANES, SSM_SEGMENTS * pitch, LANES), _F32),
            pltpu.VMEM((n_c, 2 * SLAB_STATES), _BF16),
        ],
        compiler_params=pltpu.CompilerParams(
            dimension_semantics=("arbitrary",), vmem_limit_bytes=VMEM_LIMIT),
        name="ssm",
    )(u_t, wst, kw, cp, wg, bg, a_chunk, a_seg)


def _outproj_body(a_ref, s_ref, x_ref, ag_ref, sg_ref, w_ref, o_ref, sn_ref):
    tm = x_ref.shape[0]
    an = _rms(a_ref[...].astype(_F32), ag_ref[...]).astype(_BF16)
    for t in range(SSM_CHUNK):
        sn_t = _rms(s_ref[t].astype(_F32), sg_ref[...])
        for s in range(SSM_WIDTH // LANES):
            sn_ref[s, pl.ds(t, tm // SSM_CHUNK, stride=SSM_CHUNK), :] = sn_t[:, s * LANES:(s + 1) * LANES]
    sn = jnp.concatenate([sn_ref[s] for s in range(SSM_WIDTH // LANES)], axis=1).astype(_BF16)
    mixed = jnp.concatenate([an, sn], axis=1)
    o_ref[...] = x_ref[...] + jnp.dot(mixed, w_ref[...], preferred_element_type=_F32)


def _outproj(attn, ssm_t, x, a_gain, s_gain, w_out, *, tm=512):
    L, D = x.shape
    row = lambda i: (i, 0)
    fixed = lambda i: (0, 0)
    return pl.pallas_call(
        _outproj_body,
        grid=(L // tm,),
        in_specs=[
            pl.BlockSpec((tm, ATTN_WIDTH), row),
            pl.BlockSpec((SSM_CHUNK, tm // SSM_CHUNK, SSM_WIDTH), lambda i: (0, i, 0)),
            pl.BlockSpec((tm, D), row),
            pl.BlockSpec((1, ATTN_WIDTH), fixed),
            pl.BlockSpec((1, SSM_WIDTH), fixed),
            pl.BlockSpec((ATTN_WIDTH + SSM_WIDTH, D), fixed),
        ],
        out_specs=pl.BlockSpec((tm, D), row),
        out_shape=jax.ShapeDtypeStruct((L, D), _F32),
        scratch_shapes=[pltpu.VMEM((SSM_WIDTH // LANES, tm, LANES), _F32)],
        compiler_params=pltpu.CompilerParams(
            dimension_semantics=("arbitrary",), vmem_limit_bytes=VMEM_LIMIT),
        name="outproj",
    )(attn, ssm_t, x, a_gain.reshape(1, -1), s_gain.reshape(1, -1), w_out)


def _layer(x, pos, invf, p):
    L = x.shape[0]
    seg = L // (SSM_SEGMENTS * SSM_CHUNK)
    later = ('ffn1_w_down', 'w_in', 'w_out', 'ffn2_w_gate', 'ffn2_w_up', 'ffn2_w_down')
    act, casted = _ffn_up(x, p['ffn1_norm'], p['ffn1_w_gate'].astype(_BF16), p['ffn1_w_up'].astype(_BF16),
                          [p[name] for name in later])
    w = dict(zip(later, casted))
    x = _ffn_down(act, w['ffn1_w_down'], x)
    q, k2, v2, u_t = _proj(x, p['mix_norm'], w['w_in'], pos, invf, p['q_norm'], p['k_norm'])
    attn = _attn(q, k2, v2, p['attn_sinks'])
    weights = _ssm_weights(p['ssm_log_dt'], p['ssm_a_re'], p['ssm_a_im'], p['ssm_b_re'], p['ssm_b_im'],
                           p['ssm_c_re'], p['ssm_c_im'], p['ssm_d'], p['ssm_w_glu'], p['ssm_b_glu'], seg)
    ssm_t = _ssm(u_t, weights)
    x = _outproj(attn, ssm_t, x, p['attn_out_norm'], p['ssm_out_norm'], w['w_out'])
    act, _ = _ffn_up(x, p['ffn2_norm'], w['ffn2_w_gate'], w['ffn2_w_up'])
    return _ffn_down(act, w['ffn2_w_down'], x)


def kernel(x, positions, ffn1_norm, ffn1_w_gate, ffn1_w_up, ffn1_w_down, mix_norm, w_in, q_norm, k_norm,
           attn_sinks, ssm_log_dt, ssm_a_re, ssm_a_im, ssm_b_re, ssm_b_im, ssm_c_re, ssm_c_im, ssm_d,
           ssm_w_glu, ssm_b_glu, attn_out_norm, ssm_out_norm, w_out, ffn2_norm, ffn2_w_gate, ffn2_w_up,
           ffn2_w_down):
    params = dict(
        ffn1_norm=ffn1_norm, ffn1_w_gate=ffn1_w_gate, ffn1_w_up=ffn1_w_up, ffn1_w_down=ffn1_w_down,
        mix_norm=mix_norm, w_in=w_in, q_norm=q_norm, k_norm=k_norm, attn_sinks=attn_sinks,
        ssm_log_dt=ssm_log_dt, ssm_a_re=ssm_a_re, ssm_a_im=ssm_a_im, ssm_b_re=ssm_b_re, ssm_b_im=ssm_b_im,
        ssm_c_re=ssm_c_re, ssm_c_im=ssm_c_im, ssm_d=ssm_d, ssm_w_glu=ssm_w_glu, ssm_b_glu=ssm_b_glu,
        attn_out_norm=attn_out_norm, ssm_out_norm=ssm_out_norm, w_out=w_out,
        ffn2_norm=ffn2_norm, ffn2_w_gate=ffn2_w_gate, ffn2_w_up=ffn2_w_up, ffn2_w_down=ffn2_w_down)
    depth = ffn1_norm.shape[0]
    half = HEAD_DIM // 2
    inv_freq = ROPE_THETA ** (-jnp.arange(half, dtype=_F32) * 2.0 / HEAD_DIM)
    invf = jnp.tile(inv_freq, LANES // half).reshape(1, LANES)
    outs = []
    for b in range(x.shape[0]):
        xb = x[b]
        for i in range(depth):
            xb = _layer(xb, positions[b], invf, {name: val[i] for name, val in params.items()})
        outs.append(xb)
    return jnp.stack(outs, axis=0)
```

```python
import functools
import math

import jax
import jax.numpy as jnp
from jax import lax
from jax.experimental import pallas as pl
from jax.experimental.pallas import tpu as pltpu

HEAD_DIM = 64
N_Q_HEADS = 16
N_KV_HEADS = 4
ATTN_WIDTH = N_Q_HEADS * HEAD_DIM
KV_WIDTH = N_KV_HEADS * HEAD_DIM
BLOCK = 128
ROPE_THETA = 10000.0
SSM_GROUP = 16
SSM_GROUPS = 64
SSM_STATE = 64
SSM_WIDTH = SSM_GROUP * SSM_GROUPS
FFN_RESIDUAL = 0.5
EPS = 1e-6

LANES = 128
SUBLANES = 8
MXU_DIM = 256
SSM_CHUNK = 8
SSM_SEGMENTS = SUBLANES
SLAB_GROUPS = LANES // SSM_GROUP
SLAB_STATES = SLAB_GROUPS * SSM_STATE
SCAN_UNROLL = 4
VMEM_LIMIT = 56 * 1024 * 1024

_BF16 = jnp.bfloat16
_F32 = jnp.float32
_U32 = jnp.uint32
_HI = lax.Precision.HIGHEST


def _rms(x, gain):
    ms = jnp.mean(x * x, axis=-1, keepdims=True)
    return x * lax.rsqrt(ms + EPS) * gain


def _ffn_up_body(cast_every_step, x_ref, gain_ref, wg_ref, wu_ref, *refs):
    n = len(cast_every_step)
    src_refs, a_ref, dst_refs, h_ref = refs[:n], refs[n], refs[n + 1:2 * n + 1], refs[2 * n + 1]
    first = pl.program_id(1) == 0

    @pl.when(first)
    def _():
        h_ref[...] = _rms(x_ref[...], gain_ref[...]).astype(_BF16)

    h = h_ref[...]
    g = jnp.dot(h, wg_ref[...], preferred_element_type=_F32)
    u = jnp.dot(h, wu_ref[...], preferred_element_type=_F32)
    a_ref[...] = (g * jax.nn.sigmoid(g) * u).astype(_BF16)

    for every_step, src, dst in zip(cast_every_step, src_refs, dst_refs):
        if every_step:
            dst[...] = src[...].astype(_BF16)
        else:
            @pl.when(first)
            def _(src=src, dst=dst):
                dst[...] = src[...].astype(_BF16)


def _ffn_down_body(a_ref, wd_ref, x_ref, o_ref):
    d = jnp.dot(a_ref[...], wd_ref[...], preferred_element_type=_F32)
    o_ref[...] = x_ref[...] + FFN_RESIDUAL * d


def _cast_spec(shape, ni, nj):
    R, C = shape
    pack = 2 * SUBLANES
    if R % ni == 0 and (R // ni) % pack == 0 and C % nj == 0 and (C // nj) % LANES == 0:
        return pl.BlockSpec((R // ni, C // nj), lambda i, j: (i, j)), True
    if R % (ni * nj) == 0 and (R // (ni * nj)) % pack == 0:
        return pl.BlockSpec((R // (ni * nj), C), lambda i, j: (i * nj + j, 0)), True
    assert R % ni == 0 and (R // ni) % pack == 0, shape
    return pl.BlockSpec((R // ni, C), lambda i, j: (i, 0)), False


def _ffn_up(x, gain, wg, wu, to_cast=(), *, tm=1024, tf=512):
    L, D = x.shape
    F = wg.shape[1]
    tm = min(tm, L)
    ni, nj = L // tm, F // tf
    cast_specs = [_cast_spec(w.shape, ni, nj) for w in to_cast]
    out = pl.pallas_call(
        functools.partial(_ffn_up_body, tuple(every for _, every in cast_specs)),
        grid=(ni, nj),
        in_specs=[
            pl.BlockSpec((tm, D), lambda i, j: (i, 0)),
            pl.BlockSpec((1, D), lambda i, j: (0, 0)),
            pl.BlockSpec((D, tf), lambda i, j: (0, j)),
            pl.BlockSpec((D, tf), lambda i, j: (0, j)),
        ] + [spec for spec, _ in cast_specs],
        out_specs=[pl.BlockSpec((tm, tf), lambda i, j: (i, j))] + [spec for spec, _ in cast_specs],
        out_shape=[jax.ShapeDtypeStruct((L, F), _BF16)]
        + [jax.ShapeDtypeStruct(w.shape, _BF16) for w in to_cast],
        scratch_shapes=[pltpu.VMEM((tm, D), _BF16)],
        compiler_params=pltpu.CompilerParams(
            dimension_semantics=("arbitrary", "arbitrary"), vmem_limit_bytes=VMEM_LIMIT),
        name="ffn_up",
    )(x, gain.reshape(1, D), wg, wu, *to_cast)
    return out[0], out[1:]


def _ffn_down(act, wd, x, *, tm=1024, tn=512):
    L, D = x.shape
    F = act.shape[1]
    tm = min(tm, L)
    return pl.pallas_call(
        _ffn_down_body,
        grid=(L // tm, D // tn),
        in_specs=[
            pl.BlockSpec((tm, F), lambda i, j: (i, 0)),
            pl.BlockSpec((F, tn), lambda i, j: (0, j)),
            pl.BlockSpec((tm, tn), lambda i, j: (i, j)),
        ],
        out_specs=pl.BlockSpec((tm, tn), lambda i, j: (i, j)),
        out_shape=jax.ShapeDtypeStruct((L, D), _F32),
        compiler_params=pltpu.CompilerParams(
            dimension_semantics=("arbitrary", "arbitrary"), vmem_limit_bytes=VMEM_LIMIT),
        name="ffn_down",
    )(act, wd, x)


def _proj_body(x_ref, gain_ref, w_ref, pos_ref, invf_ref, qg_ref, kg_ref, pn_ref,
               q_ref, k_ref, v_ref, u_ref, us_ref):
    tm = x_ref.shape[0]
    h = _rms(x_ref[...], gain_ref[...]).astype(_BF16)
    group = 2 * MXU_DIM

    def project(g):
        return jnp.dot(h, w_ref[:, g * group:(g + 1) * group], preferred_element_type=_F32)

    half = HEAD_DIM // 2
    ang = pos_ref[...] * invf_ref[...]
    lane_q = lax.broadcasted_iota(jnp.int32, ang.shape, 1) // half

    def spread(table):
        parts = []
        for qtr in range(LANES // half):
            m = jnp.where(lane_q == qtr, table, 0.0)
            parts.append(m + pltpu.roll(m, half, 1) + pltpu.roll(m, 2 * half, 1) + pltpu.roll(m, 3 * half, 1))
        return jnp.concatenate(parts, axis=0)

    cos = spread(jnp.cos(ang))
    sin = spread(jnp.sin(ang))
    lane = lax.broadcasted_iota(jnp.int32, (tm, LANES), 1)
    first_half = (lane & (HEAD_DIM // 2)) == 0
    low_head = lane < HEAD_DIM
    sin_signed = jnp.where(first_half, -sin, sin)

    def norm_rotary(x4, gain):
        ms4 = jnp.dot((x4 * x4).astype(_BF16), pn_ref[...], preferred_element_type=_F32)
        out = []
        for part in range(2):
            lanes = slice(part * LANES, (part + 1) * LANES)
            y = x4[:, lanes] * lax.rsqrt(ms4[:, lanes] + EPS) * gain
            swapped = jnp.where(first_half, pltpu.roll(y, LANES - HEAD_DIM // 2, 1),
                                pltpu.roll(y, HEAD_DIM // 2, 1))
            out.append(y * cos + swapped * sin_signed)
        return out

    def dup_heads(xc):
        r = pltpu.roll(xc, HEAD_DIM, 1)
        return jnp.where(low_head, xc, r), jnp.where(low_head, r, xc)

    scale = 1.0 / math.sqrt(HEAD_DIM)
    assert KV_WIDTH == MXU_DIM and 2 * KV_WIDTH == group
    for g in range(ATTN_WIDTH // group):
        pg = project(g)
        for c in range(group // MXU_DIM):
            for part, qc in enumerate(norm_rotary(pg[:, c * MXU_DIM:(c + 1) * MXU_DIM], qg_ref[...])):
                at = g * group + c * MXU_DIM + part * LANES
                q_ref[:, at:at + LANES] = (qc * scale).astype(_BF16)
    pg = project(ATTN_WIDTH // group)
    for part, kc in enumerate(norm_rotary(pg[:, :KV_WIDTH], kg_ref[...])):
        ka, kb = dup_heads(kc)
        k_ref[:, 2 * part * LANES:(2 * part + 1) * LANES] = ka.astype(_BF16)
        k_ref[:, (2 * part + 1) * LANES:(2 * part + 2) * LANES] = kb.astype(_BF16)
    for part in range(KV_WIDTH // LANES):
        va, vb = dup_heads(pg[:, KV_WIDTH + part * LANES:KV_WIDTH + (part + 1) * LANES])
        v_ref[:, 2 * part * LANES:(2 * part + 1) * LANES] = va.astype(_BF16)
        v_ref[:, (2 * part + 1) * LANES:(2 * part + 2) * LANES] = vb.astype(_BF16)
    first_u = (ATTN_WIDTH + 2 * KV_WIDTH) // group
    per_group = group // LANES
    for g in range(SSM_WIDTH // group):
        pg = project(first_u + g)
        for s in range(per_group):
            us_ref[g * per_group + s] = pg[:, s * LANES:(s + 1) * LANES]
        for t in range(SSM_CHUNK):
            for s in range(g * per_group, (g + 1) * per_group):
                rows = us_ref[s, pl.ds(t, tm // SSM_CHUNK, stride=SSM_CHUNK), :]
                u_ref[t, :, s * LANES:(s + 1) * LANES] = rows.astype(_BF16)


def _proj(x, gain, w_in, pos, invf, q_gain, k_gain, *, tm=512):
    L, D = x.shape
    C = w_in.shape[1]
    quarters = LANES // (HEAD_DIM // 2)
    pos_f = pos.astype(_F32).reshape(L // tm, quarters, tm // quarters).transpose(0, 2, 1)
    pos_f = jnp.repeat(pos_f, HEAD_DIM // 2, axis=2)
    head_of_lane = jnp.arange(MXU_DIM) // HEAD_DIM
    pn = jnp.where(head_of_lane[:, None] == head_of_lane[None, :], 1.0 / HEAD_DIM, 0.0).astype(_BF16)
    qg = jnp.tile(q_gain, LANES // HEAD_DIM).reshape(1, LANES)
    kg = jnp.tile(k_gain, LANES // HEAD_DIM).reshape(1, LANES)
    row = lambda i: (i, 0)
    fixed = lambda i: (0, 0)
    return pl.pallas_call(
        _proj_body,
        grid=(L // tm,),
        in_specs=[
            pl.BlockSpec((tm, D), row),
            pl.BlockSpec((1, D), fixed),
            pl.BlockSpec((D, C), fixed),
            pl.BlockSpec((None, tm // quarters, LANES), lambda i: (i, 0, 0)),
            pl.BlockSpec((1, LANES), fixed),
            pl.BlockSpec((1, LANES), fixed),
            pl.BlockSpec((1, LANES), fixed),
            pl.BlockSpec((MXU_DIM, MXU_DIM), fixed),
        ],
        out_specs=[
            pl.BlockSpec((tm, ATTN_WIDTH), row),
            pl.BlockSpec((tm, 2 * KV_WIDTH), row),
            pl.BlockSpec((tm, 2 * KV_WIDTH), row),
            pl.BlockSpec((SSM_CHUNK, tm // SSM_CHUNK, SSM_WIDTH), lambda i: (0, i, 0)),
        ],
        out_shape=[
            jax.ShapeDtypeStruct((L, ATTN_WIDTH), _BF16),
            jax.ShapeDtypeStruct((L, 2 * KV_WIDTH), _BF16),
            jax.ShapeDtypeStruct((L, 2 * KV_WIDTH), _BF16),
            jax.ShapeDtypeStruct((SSM_CHUNK, L // SSM_CHUNK, SSM_WIDTH), _BF16),
        ],
        scratch_shapes=[pltpu.VMEM((SSM_WIDTH // LANES, tm, LANES), _F32)],
        compiler_params=pltpu.CompilerParams(
            dimension_semantics=("arbitrary",), vmem_limit_bytes=VMEM_LIMIT),
        name="proj",
    )(x, gain.reshape(1, D), w_in, pos_f, invf, qg, kg, pn)


def _row_cast_specs(arrays, n):
    pack = 2 * SUBLANES
    for w in arrays:
        assert w.shape[0] % (n * pack) == 0, w.shape
    specs = [pl.BlockSpec((w.shape[0] // n, w.shape[1]), lambda i: (i, 0)) for w in arrays]
    return specs, [jax.ShapeDtypeStruct(w.shape, _BF16) for w in arrays]


def _attn_body(n_cast, sink_ref, q_ref, kc_ref, vc_ref, kp_ref, vp_ref, *refs):
    o_ref = refs[n_cast]
    for src, dst in zip(refs[:n_cast], refs[n_cast + 1:]):
        dst[...] = src[...].astype(_BF16)
    tq = q_ref.shape[0]
    qi = lax.broadcasted_iota(jnp.int32, (BLOCK, BLOCK), 0)
    kj = lax.broadcasted_iota(jnp.int32, (BLOCK, BLOCK), 1)
    from_prev = kj > qi
    has_prev = pl.program_id(0) > 0
    low_head = lax.broadcasted_iota(jnp.int32, (2 * BLOCK, LANES), 1) < HEAD_DIM
    nt = (((1,), (1,)), ((), ()))

    for b in range(tq // BLOCK):
        rows = slice(b * BLOCK, (b + 1) * BLOCK)
        for hk in range(N_KV_HEADS):
            cols = slice(hk * LANES, (hk + 1) * LANES)
            if b == 0:
                kd = jnp.concatenate([kp_ref[:, cols], kc_ref[0:BLOCK, cols]], axis=0)
                vd = jnp.concatenate([vp_ref[:, cols], vc_ref[0:BLOCK, cols]], axis=0)
            else:
                kd = kc_ref[(b - 1) * BLOCK:(b + 1) * BLOCK, cols]
                vd = vc_ref[(b - 1) * BLOCK:(b + 1) * BLOCK, cols]
            zero = jnp.zeros_like(kd)
            k_half = (jnp.where(low_head, kd, zero), jnp.where(low_head, zero, kd))
            v_half = (jnp.where(low_head, vd, zero), jnp.where(low_head, zero, vd))
            for pp in range(2):
                pair = hk * 2 + pp
                qp = q_ref[rows, pair * LANES:(pair + 1) * LANES]
                acc = None
                for half in range(2):
                    sink = sink_ref[pair * 2 + half]
                    s2 = lax.dot_general(qp, k_half[half], nt, preferred_element_type=_F32)
                    s_prev = s2[:, :BLOCK]
                    if b == 0:
                        s_prev = jnp.where(has_prev, s_prev, -jnp.inf)
                    s = jnp.where(from_prev, s_prev, s2[:, BLOCK:])
                    m = jnp.maximum(jnp.max(s, axis=-1, keepdims=True), sink)
                    p = jnp.exp(s - m)
                    den = jnp.sum(p, axis=-1, keepdims=True) + jnp.exp(sink - m)
                    pb = p.astype(_BF16)
                    pz = jnp.zeros_like(pb)
                    p2 = jnp.concatenate([jnp.where(from_prev, pb, pz), jnp.where(from_prev, pz, pb)], axis=1)
                    o = jnp.dot(p2, v_half[half], preferred_element_type=_F32) * (1.0 / den)
                    acc = o if acc is None else acc + o
                o_ref[rows, pair * LANES:(pair + 1) * LANES] = acc.astype(_BF16)


def _attn(q, k2, v2, sinks, to_cast=(), *, tq=512):
    L = q.shape[0]
    per = tq // BLOCK
    row = lambda i: (i, 0)
    prev = lambda i: (jnp.maximum(i * per - 1, 0), 0)
    cast_specs, cast_shapes = _row_cast_specs(to_cast, L // tq)
    out = pl.pallas_call(
        functools.partial(_attn_body, len(to_cast)),
        grid=(L // tq,),
        in_specs=[
            pl.BlockSpec(memory_space=pltpu.SMEM),
            pl.BlockSpec((tq, ATTN_WIDTH), row),
            pl.BlockSpec((tq, 2 * KV_WIDTH), row),
            pl.BlockSpec((tq, 2 * KV_WIDTH), row),
            pl.BlockSpec((BLOCK, 2 * KV_WIDTH), prev),
            pl.BlockSpec((BLOCK, 2 * KV_WIDTH), prev),
        ] + cast_specs,
        out_specs=[pl.BlockSpec((tq, ATTN_WIDTH), row)] + cast_specs,
        out_shape=[jax.ShapeDtypeStruct((L, ATTN_WIDTH), _BF16)] + cast_shapes,
        compiler_params=pltpu.CompilerParams(
            dimension_semantics=("arbitrary",), vmem_limit_bytes=VMEM_LIMIT),
        name="attn",
    )(sinks, q, k2, v2, k2, v2, *to_cast)
    return out[0], out[1:]


def _ssm_body(u_ref, wst_ref, kw_ref, cp_ref, wg_ref, bg_ref, a_ref, aseg_ref, o_ref,
              lhs_ref, toep_ref, s_ref, xb_ref):
    T = SSM_CHUNK
    n_c = u_ref.shape[1]
    seg = n_c // SSM_SEGMENTS
    pitch = s_ref.shape[1] // SSM_SEGMENTS
    n_state_slabs = SLAB_STATES // LANES
    pair_w = 2 * LANES
    out_w = 2 * MXU_DIM

    for t in range(T):
        lhs_ref[:, t * LANES:(t + 1) * LANES] = u_ref[t]

    toep_ref[...] = jnp.zeros(toep_ref.shape, _BF16)
    for t in range(T):
        for tp in range(t, T):
            toep_ref[t * LANES:(t + 1) * LANES, tp * LANES:(tp + 1) * LANES] = (
                kw_ref[:, (tp - t) * LANES:(tp - t + 1) * LANES])

    per_dot = out_w // LANES
    for nb in range(2 * n_state_slabs // per_dot):
        res = jnp.dot(lhs_ref[...], wst_ref[:, nb * out_w:(nb + 1) * out_w], preferred_element_type=_F32)
        for part in range(per_dot):
            for j in range(SSM_SEGMENTS):
                s_ref[per_dot * nb + part, j * pitch:j * pitch + seg, :] = (
                    res[j * seg:(j + 1) * seg, part * LANES:(part + 1) * LANES])

    shape = (SSM_SEGMENTS, LANES)
    a_re = [jnp.broadcast_to(a_ref[0:1, k * LANES:(k + 1) * LANES], shape) for k in range(n_state_slabs)]
    a_im = [jnp.broadcast_to(a_ref[1:2, k * LANES:(k + 1) * LANES], shape) for k in range(n_state_slabs)]
    g_re = [jnp.broadcast_to(aseg_ref[0:1, k * LANES:(k + 1) * LANES], shape) for k in range(n_state_slabs)]
    g_im = [jnp.broadcast_to(aseg_ref[1:2, k * LANES:(k + 1) * LANES], shape) for k in range(n_state_slabs)]

    def seg_rows(w):
        return pl.ds(w, SSM_SEGMENTS, stride=pitch)

    def advance(w, carry, store):
        out = []
        for k in range(n_state_slabs):
            z_re, z_im = carry[2 * k], carry[2 * k + 1]
            s_re = s_ref[k, seg_rows(w), :]
            s_im = s_ref[n_state_slabs + k, seg_rows(w), :]
            if store:
                s_ref[k, seg_rows(w), :] = z_re
                s_ref[n_state_slabs + k, seg_rows(w), :] = z_im
            out.append(a_re[k] * z_re - a_im[k] * z_im + s_re)
            out.append(a_re[k] * z_im + a_im[k] * z_re + s_im)
        return tuple(out)

    zero = jnp.zeros(shape, _F32)
    ends = lax.fori_loop(0, seg, lambda w, c: advance(w, c, False), (zero,) * (2 * n_state_slabs),
                         unroll=SCAN_UNROLL)

    segidx = lax.broadcasted_iota(jnp.int32, shape, 0)
    init = []
    for k in range(n_state_slabs):
        i_re, i_im = zero, zero
        f_re, f_im = ends[2 * k], ends[2 * k + 1]
        for j in range(SSM_SEGMENTS - 1):
            c_re = g_re[k] * i_re - g_im[k] * i_im + f_re
            c_im = g_re[k] * i_im + g_im[k] * i_re + f_im
            i_re = jnp.where(segidx == j + 1, pltpu.roll(c_re, 1, 0), i_re)
            i_im = jnp.where(segidx == j + 1, pltpu.roll(c_im, 1, 0), i_im)
        init += [i_re, i_im]

    lax.fori_loop(0, seg, lambda w, c: advance(w, c, True), tuple(init), unroll=SCAN_UNROLL)

    for col in range(2 * n_state_slabs):
        for j in range(SSM_SEGMENTS):
            xb_ref[j * seg:(j + 1) * seg, col * LANES:(col + 1) * LANES] = (
                s_ref[col, j * pitch:j * pitch + seg, :].astype(_BF16))

    steps = out_w // LANES
    for i in range(T // steps):
        kk = (i + 1) * out_w
        cols = slice(i * out_w, (i + 1) * out_w)
        y = (jnp.dot(lhs_ref[:, :kk], toep_ref[:kk, cols], preferred_element_type=_F32)
             + jnp.dot(xb_ref[...], cp_ref[:, cols], preferred_element_type=_F32))
        z = jax.nn.gelu(y, approximate=True).astype(_BF16)
        for pp in range(steps // 2):
            gt = jnp.dot(z[:, pp * pair_w:(pp + 1) * pair_w], wg_ref[...],
                         preferred_element_type=_F32) + bg_ref[...]
            out = gt[:, :pair_w] * jax.nn.sigmoid(gt[:, pair_w:])
            t0 = i * steps + 2 * pp
            o_ref[t0] = out[:, :LANES].astype(_BF16)
            o_ref[t0 + 1] = out[:, LANES:].astype(_BF16)


def _complex_power(re, im, n):
    out_re, out_im = None, None
    while n:
        if n & 1:
            if out_re is None:
                out_re, out_im = re, im
            else:
                out_re, out_im = out_re * re - out_im * im, out_re * im + out_im * re
        n >>= 1
        if n:
            re, im = re * re - im * im, 2.0 * re * im
    return out_re, out_im


def _ssm_prep_body(seg, rows_ref, bt_ref, ct_ref, d_ref, wt_ref, bglu_ref,
                   wst_ref, kw_ref, cp_ref, wg_ref, bg_ref, ach_ref, aseg_ref):
    T, H = SSM_CHUNK, SSM_GROUP
    a_re, a_im = rows_ref[0:1, :], rows_ref[1:2, :]
    dt = jnp.exp(rows_ref[2:3, :])
    mag = jnp.exp(a_re * dt)
    ab_re = mag * jnp.cos(a_im * dt)
    ab_im = mag * jnp.sin(a_im * dt)
    inv_den = 1.0 / (a_re * a_re + a_im * a_im)
    nr = ab_re - 1.0
    coef_re = (nr * a_re + ab_im * a_im) * inv_den
    coef_im = (ab_im * a_re - nr * a_im) * inv_den

    def block_diag(t16, shape):
        row_g = lax.broadcasted_iota(jnp.int32, shape, 0) // H
        col_g = lax.broadcasted_iota(jnp.int32, shape, 1) // (shape[1] // SLAB_GROUPS)
        return jnp.where(row_g == col_g, jnp.concatenate([t16] * SLAB_GROUPS, axis=0), 0.0)

    cs = (LANES, SLAB_STATES)
    b_re, b_im = block_diag(bt_ref[0], cs), block_diag(bt_ref[1], cs)
    bb_re = coef_re * b_re - coef_im * b_im
    bb_im = coef_re * b_im + coef_im * b_re
    ck_re = jnp.transpose(block_diag(ct_ref[0], cs))
    ck_im = jnp.transpose(block_diag(ct_ref[1], cs))
    col_re = jnp.transpose(jnp.broadcast_to(ab_re, cs))
    col_im = jnp.transpose(jnp.broadcast_to(ab_im, cs))

    pw = [(jnp.ones_like(ab_re), jnp.zeros_like(ab_im))]
    for _ in range(T):
        pr, pi = pw[-1]
        pw.append((pr * ab_re - pi * ab_im, pr * ab_im + pi * ab_re))

    for t in range(T):
        pr, pi = pw[T - 1 - t]
        wst_ref[t * LANES:(t + 1) * LANES, :SLAB_STATES] = (bb_re * pr - bb_im * pi).astype(_BF16)
        wst_ref[t * LANES:(t + 1) * LANES, SLAB_STATES:] = (bb_re * pi + bb_im * pr).astype(_BF16)

    lhs = jnp.concatenate([bb_re, -bb_im], axis=1)
    eye = (lax.broadcasted_iota(jnp.int32, (LANES, LANES), 0)
           == lax.broadcasted_iota(jnp.int32, (LANES, LANES), 1))
    for k in range(T + 1):
        if k < T:
            kern = jnp.dot(lhs, jnp.concatenate([ck_re, ck_im], axis=0), precision=_HI,
                           preferred_element_type=_F32)
            if k == 0:
                kern = kern + jnp.where(eye, d_ref[...], 0.0)
            kw_ref[:, k * LANES:(k + 1) * LANES] = kern.astype(_BF16)
        if k > 0:
            cp_ref[:SLAB_STATES, (k - 1) * LANES:k * LANES] = ck_re.astype(_BF16)
            cp_ref[SLAB_STATES:, (k - 1) * LANES:k * LANES] = (-ck_im).astype(_BF16)
        ck_re, ck_im = ck_re * col_re - ck_im * col_im, ck_re * col_im + ck_im * col_re

    cc = (LANES, LANES)
    w_lin = jnp.transpose(block_diag(wt_ref[0], cc)).astype(_BF16)
    w_gate = jnp.transpose(block_diag(wt_ref[1], cc)).astype(_BF16)
    wg_ref[...] = jnp.zeros(wg_ref.shape, _BF16)
    for t in range(2):
        wg_ref[t * LANES:(t + 1) * LANES, t * LANES:(t + 1) * LANES] = w_lin
        wg_ref[t * LANES:(t + 1) * LANES, (2 + t) * LANES:(3 + t) * LANES] = w_gate
    bg_ref[...] = jnp.concatenate([bglu_ref[0:1, :], bglu_ref[0:1, :], bglu_ref[1:2, :], bglu_ref[1:2, :]],
                                  axis=1)

    ach_ref[0:1, :], ach_ref[1:2, :] = pw[T]
    aseg_ref[0:1, :], aseg_ref[1:2, :] = _complex_power(pw[T][0], pw[T][1], seg)


def _ssm_weights(log_dt, a_re, a_im, b_re, b_im, c_re, c_im, d_skip, w_glu, b_glu, seg):
    P, H, T = SSM_STATE, SSM_GROUP, SSM_CHUNK
    SG, NS = SLAB_GROUPS, SSM_GROUPS // SLAB_GROUPS
    rows = jnp.stack([a_re.reshape(NS, SG * P), a_im.reshape(NS, SG * P),
                      jnp.repeat(log_dt, P).reshape(NS, SG * P)], axis=1)

    def per_group_rows(w, lead):
        t = w.reshape((NS, SG) + w.shape[1:])
        t = jnp.moveaxis(t, 2 + lead, 1)
        return t.reshape(NS, t.shape[1], -1)

    bt = jnp.stack([per_group_rows(b_re, 1), per_group_rows(b_im, 1)], axis=1)
    ct = jnp.stack([per_group_rows(c_re, 0), per_group_rows(c_im, 0)], axis=1)
    wt = jnp.stack([per_group_rows(w_glu[..., :H], 1), per_group_rows(w_glu[..., H:], 1)], axis=1)
    bglu = jnp.stack([b_glu[:, :H].reshape(NS, SG * H), b_glu[:, H:].reshape(NS, SG * H)], axis=1)
    d = d_skip.reshape(NS, 1, SG * H)

    blk3 = lambda i: (i, 0, 0)
    blk4 = lambda i: (i, 0, 0, 0)
    return pl.pallas_call(
        functools.partial(_ssm_prep_body, seg),
        grid=(NS,),
        in_specs=[
            pl.BlockSpec((None, 3, SG * P), blk3),
            pl.BlockSpec((None, 2, H, SG * P), blk4),
            pl.BlockSpec((None, 2, H, SG * P), blk4),
            pl.BlockSpec((None, 1, SG * H), blk3),
            pl.BlockSpec((None, 2, H, SG * H), blk4),
            pl.BlockSpec((None, 2, SG * H), blk3),
        ],
        out_specs=[
            pl.BlockSpec((None, T * LANES, 2 * SLAB_STATES), blk3),
            pl.BlockSpec((None, LANES, T * LANES), blk3),
            pl.BlockSpec((None, 2 * SLAB_STATES, T * LANES), blk3),
            pl.BlockSpec((None, 2 * LANES, 4 * LANES), blk3),
            pl.BlockSpec((None, 1, 4 * LANES), blk3),
            pl.BlockSpec((None, 2, SLAB_STATES), blk3),
            pl.BlockSpec((None, 2, SLAB_STATES), blk3),
        ],
        out_shape=[
            jax.ShapeDtypeStruct((NS, T * LANES, 2 * SLAB_STATES), _BF16),
            jax.ShapeDtypeStruct((NS, LANES, T * LANES), _BF16),
            jax.ShapeDtypeStruct((NS, 2 * SLAB_STATES, T * LANES), _BF16),
            jax.ShapeDtypeStruct((NS, 2 * LANES, 4 * LANES), _BF16),
            jax.ShapeDtypeStruct((NS, 1, 4 * LANES), _F32),
            jax.ShapeDtypeStruct((NS, 2, SLAB_STATES), _F32),
            jax.ShapeDtypeStruct((NS, 2, SLAB_STATES), _F32),
        ],
        compiler_params=pltpu.CompilerParams(
            dimension_semantics=("arbitrary",), vmem_limit_bytes=VMEM_LIMIT),
        name="ssm_prep",
    )(rows, bt, ct, d, wt, bglu)


def _ssm(u_t, weights):
    T, n_c, W = u_t.shape
    seg = n_c // SSM_SEGMENTS
    pitch = seg + SUBLANES
    ns = W // LANES
    wst, kw, cp, wg, bg, a_chunk, a_seg = weights
    slab = lambda i: (0, 0, i)
    blk = lambda i: (i, 0, 0)
    once = pl.Buffered(1)
    return pl.pallas_call(
        _ssm_body,
        grid=(ns,),
        in_specs=[
            pl.BlockSpec((T, n_c, LANES), slab, pipeline_mode=once),
            pl.BlockSpec((None, T * LANES, 2 * SLAB_STATES), blk, pipeline_mode=once),
            pl.BlockSpec((None, LANES, T * LANES), blk),
            pl.BlockSpec((None, 2 * SLAB_STATES, T * LANES), blk, pipeline_mode=once),
            pl.BlockSpec((None, 2 * LANES, 4 * LANES), blk),
            pl.BlockSpec((None, 1, 4 * LANES), blk),
            pl.BlockSpec((None, 2, SLAB_STATES), blk),
            pl.BlockSpec((None, 2, SLAB_STATES), blk),
        ],
        out_specs=pl.BlockSpec((T, n_c, LANES), slab),
        out_shape=jax.ShapeDtypeStruct((T, n_c, W), _BF16),
        scratch_shapes=[
            pltpu.VMEM((n_c, T * LANES), _BF16),
            pltpu.VMEM((T * LANES, T * LANES), _BF16),
            pltpu.VMEM((2 * SLAB_STATES // LANES, SSM_SEGMENTS * pitch, LANES), _F32),
            pltpu.VMEM((n_c, 2 * SLAB_STATES), _BF16),
        ],
        compiler_params=pltpu.CompilerParams(
            dimension_semantics=("arbitrary",), vmem_limit_bytes=VMEM_LIMIT),
        name="ssm",
    )(u_t, wst, kw, cp, wg, bg, a_chunk, a_seg)


def _outproj_body(n_cast, a_ref, s_ref, x_ref, ag_ref, sg_ref, w_ref, *refs):
    o_ref, sn_ref = refs[n_cast], refs[-1]
    for src, dst in zip(refs[:n_cast], refs[n_cast + 1:-1]):
        dst[...] = src[...].astype(_BF16)
    tm = x_ref.shape[0]
    an = _rms(a_ref[...].astype(_F32), ag_ref[...]).astype(_BF16)
    for t in range(SSM_CHUNK):
        sn_t = _rms(s_ref[t].astype(_F32), sg_ref[...])
        for s in range(SSM_WIDTH // LANES):
            sn_ref[s, pl.ds(t, tm // SSM_CHUNK, stride=SSM_CHUNK), :] = sn_t[:, s * LANES:(s + 1) * LANES]
    sn = jnp.concatenate([sn_ref[s] for s in range(SSM_WIDTH // LANES)], axis=1).astype(_BF16)
    mixed = jnp.concatenate([an, sn], axis=1)
    o_ref[...] = x_ref[...] + jnp.dot(mixed, w_ref[...], preferred_element_type=_F32)


def _outproj(attn, ssm_t, x, a_gain, s_gain, w_out, to_cast=(), *, tm=512):
    L, D = x.shape
    row = lambda i: (i, 0)
    fixed = lambda i: (0, 0)
    cast_specs, cast_shapes = _row_cast_specs(to_cast, L // tm)
    out = pl.pallas_call(
        functools.partial(_outproj_body, len(to_cast)),
        grid=(L // tm,),
        in_specs=[
            pl.BlockSpec((tm, ATTN_WIDTH), row),
            pl.BlockSpec((SSM_CHUNK, tm // SSM_CHUNK, SSM_WIDTH), lambda i: (0, i, 0)),
            pl.BlockSpec((tm, D), row),
            pl.BlockSpec((1, ATTN_WIDTH), fixed),
            pl.BlockSpec((1, SSM_WIDTH), fixed),
            pl.BlockSpec((ATTN_WIDTH + SSM_WIDTH, D), fixed),
        ] + cast_specs,
        out_specs=[pl.BlockSpec((tm, D), row)] + cast_specs,
        out_shape=[jax.ShapeDtypeStruct((L, D), _F32)] + cast_shapes,
        scratch_shapes=[pltpu.VMEM((SSM_WIDTH // LANES, tm, LANES), _F32)],
        compiler_params=pltpu.CompilerParams(
            dimension_semantics=("arbitrary",), vmem_limit_bytes=VMEM_LIMIT),
        name="outproj",
    )(attn, ssm_t, x, a_gain.reshape(1, -1), s_gain.reshape(1, -1), w_out, *to_cast)
    return out[0], out[1:]


def _layer(x, pos, invf, p):
    L = x.shape[0]
    seg = L // (SSM_SEGMENTS * SSM_CHUNK)
    act, (w_down1, w_in, w_out) = _ffn_up(
        x, p['ffn1_norm'], p['ffn1_w_gate'].astype(_BF16), p['ffn1_w_up'].astype(_BF16),
        [p['ffn1_w_down'], p['w_in'], p['w_out']])
    x = _ffn_down(act, w_down1, x)
    q, k2, v2, u_t = _proj(x, p['mix_norm'], w_in, pos, invf, p['q_norm'], p['k_norm'])
    attn, (w_gate2, w_up2) = _attn(q, k2, v2, p['attn_sinks'], [p['ffn2_w_gate'], p['ffn2_w_up']])
    weights = _ssm_weights(p['ssm_log_dt'], p['ssm_a_re'], p['ssm_a_im'], p['ssm_b_re'], p['ssm_b_im'],
                           p['ssm_c_re'], p['ssm_c_im'], p['ssm_d'], p['ssm_w_glu'], p['ssm_b_glu'], seg)
    ssm_t = _ssm(u_t, weights)
    x, (w_down2,) = _outproj(attn, ssm_t, x, p['attn_out_norm'], p['ssm_out_norm'], w_out,
                             [p['ffn2_w_down']])
    act, _ = _ffn_up(x, p['ffn2_norm'], w_gate2, w_up2)
    return _ffn_down(act, w_down2, x)


def kernel(x, positions, ffn1_norm, ffn1_w_gate, ffn1_w_up, ffn1_w_down, mix_norm, w_in, q_norm, k_norm,
           attn_sinks, ssm_log_dt, ssm_a_re, ssm_a_im, ssm_b_re, ssm_b_im, ssm_c_re, ssm_c_im, ssm_d,
           ssm_w_glu, ssm_b_glu, attn_out_norm, ssm_out_norm, w_out, ffn2_norm, ffn2_w_gate, ffn2_w_up,
           ffn2_w_down):
    params = dict(
        ffn1_norm=ffn1_norm, ffn1_w_gate=ffn1_w_gate, ffn1_w_up=ffn1_w_up, ffn1_w_down=ffn1_w_down,
        mix_norm=mix_norm, w_in=w_in, q_norm=q_norm, k_norm=k_norm, attn_sinks=attn_sinks,
        ssm_log_dt=ssm_log_dt, ssm_a_re=ssm_a_re, ssm_a_im=ssm_a_im, ssm_b_re=ssm_b_re, ssm_b_im=ssm_b_im,
        ssm_c_re=ssm_c_re, ssm_c_im=ssm_c_im, ssm_d=ssm_d, ssm_w_glu=ssm_w_glu, ssm_b_glu=ssm_b_glu,
        attn_out_norm=attn_out_norm, ssm_out_norm=ssm_out_norm, w_out=w_out,
        ffn2_norm=ffn2_norm, ffn2_w_gate=ffn2_w_gate, ffn2_w_up=ffn2_w_up, ffn2_w_down=ffn2_w_down)
    depth = ffn1_norm.shape[0]
    half = HEAD_DIM // 2
    inv_freq = ROPE_THETA ** (-jnp.arange(half, dtype=_F32) * 2.0 / HEAD_DIM)
    invf = jnp.tile(inv_freq, LANES // half).reshape(1, LANES)
    outs = []
    for b in range(x.shape[0]):
        xb = x[b]
        for i in range(depth):
            xb = _layer(xb, positions[b], invf, {name: val[i] for name, val in params.items()})
        outs.append(xb)
    return jnp.stack(outs, axis=0)
```

```python
import functools
import math

import jax
import jax.numpy as jnp
from jax import lax
from jax.experimental import pallas as pl
from jax.experimental.pallas import tpu as pltpu

HEAD_DIM = 64
N_Q_HEADS = 16
N_KV_HEADS = 4
ATTN_WIDTH = N_Q_HEADS * HEAD_DIM
KV_WIDTH = N_KV_HEADS * HEAD_DIM
BLOCK = 128
ROPE_THETA = 10000.0
SSM_GROUP = 16
SSM_GROUPS = 64
SSM_STATE = 64
SSM_WIDTH = SSM_GROUP * SSM_GROUPS
FFN_RESIDUAL = 0.5
EPS = 1e-6

LANES = 128
SUBLANES = 8
MXU_DIM = 256
SSM_CHUNK = 8
SSM_SEGMENTS = SUBLANES
SLAB_GROUPS = LANES // SSM_GROUP
SLAB_STATES = SLAB_GROUPS * SSM_STATE
SCAN_UNROLL = 4
VMEM_LIMIT = 56 * 1024 * 1024

_BF16 = jnp.bfloat16
_F32 = jnp.float32
_U32 = jnp.uint32
_HI = lax.Precision.HIGHEST


def _rms(x, gain):
    ms = jnp.mean(x * x, axis=-1, keepdims=True)
    return x * lax.rsqrt(ms + EPS) * gain


def _ffn_up_body(cast_every_step, x_ref, gain_ref, wg_ref, wu_ref, *refs):
    n = len(cast_every_step)
    src_refs, a_ref, dst_refs, h_ref = refs[:n], refs[n], refs[n + 1:2 * n + 1], refs[2 * n + 1]
    first = pl.program_id(1) == 0

    @pl.when(first)
    def _():
        h_ref[...] = _rms(x_ref[...], gain_ref[...]).astype(_BF16)

    h = h_ref[...]
    g = jnp.dot(h, wg_ref[...], preferred_element_type=_F32)
    u = jnp.dot(h, wu_ref[...], preferred_element_type=_F32)
    a_ref[...] = (g * jax.nn.sigmoid(g) * u).astype(_BF16)

    for every_step, src, dst in zip(cast_every_step, src_refs, dst_refs):
        if every_step:
            dst[...] = src[...].astype(_BF16)
        else:
            @pl.when(first)
            def _(src=src, dst=dst):
                dst[...] = src[...].astype(_BF16)


def _ffn_down_body(a_ref, wd_ref, x_ref, o_ref):
    d = jnp.dot(a_ref[...], wd_ref[...], preferred_element_type=_F32)
    o_ref[...] = x_ref[...] + FFN_RESIDUAL * d


def _cast_spec(shape, ni, nj):
    R, C = shape
    pack = 2 * SUBLANES
    if R % ni == 0 and (R // ni) % pack == 0 and C % nj == 0 and (C // nj) % LANES == 0:
        return pl.BlockSpec((R // ni, C // nj), lambda i, j: (i, j)), True
    if R % (ni * nj) == 0 and (R // (ni * nj)) % pack == 0:
        return pl.BlockSpec((R // (ni * nj), C), lambda i, j: (i * nj + j, 0)), True
    assert R % ni == 0 and (R // ni) % pack == 0, shape
    return pl.BlockSpec((R // ni, C), lambda i, j: (i, 0)), False


def _ffn_up(x, gain, wg, wu, to_cast=(), *, tm=1024, tf=512):
    L, D = x.shape
    F = wg.shape[1]
    tm = min(tm, L)
    ni, nj = L // tm, F // tf
    cast_specs = [_cast_spec(w.shape, ni, nj) for w in to_cast]
    out = pl.pallas_call(
        functools.partial(_ffn_up_body, tuple(every for _, every in cast_specs)),
        grid=(ni, nj),
        in_specs=[
            pl.BlockSpec((tm, D), lambda i, j: (i, 0)),
            pl.BlockSpec((1, D), lambda i, j: (0, 0)),
            pl.BlockSpec((D, tf), lambda i, j: (0, j)),
            pl.BlockSpec((D, tf), lambda i, j: (0, j)),
        ] + [spec for spec, _ in cast_specs],
        out_specs=[pl.BlockSpec((tm, tf), lambda i, j: (i, j))] + [spec for spec, _ in cast_specs],
        out_shape=[jax.ShapeDtypeStruct((L, F), _BF16)]
        + [jax.ShapeDtypeStruct(w.shape, _BF16) for w in to_cast],
        scratch_shapes=[pltpu.VMEM((tm, D), _BF16)],
        compiler_params=pltpu.CompilerParams(
            dimension_semantics=("arbitrary", "arbitrary"), vmem_limit_bytes=VMEM_LIMIT),
        name="ffn_up",
    )(x, gain.reshape(1, D), wg, wu, *to_cast)
    return out[0], out[1:]


def _ffn_down(act, wd, x, *, tm=512):
    L, D = x.shape
    F = act.shape[1]
    tm = min(tm, L)
    return pl.pallas_call(
        _ffn_down_body,
        grid=(L // tm,),
        in_specs=[
            pl.BlockSpec((tm, F), lambda i: (i, 0)),
            pl.BlockSpec((F, D), lambda i: (0, 0), pipeline_mode=pl.Buffered(1)),
            pl.BlockSpec((tm, D), lambda i: (i, 0)),
        ],
        out_specs=pl.BlockSpec((tm, D), lambda i: (i, 0)),
        out_shape=jax.ShapeDtypeStruct((L, D), _F32),
        compiler_params=pltpu.CompilerParams(
            dimension_semantics=("arbitrary",), vmem_limit_bytes=VMEM_LIMIT),
        name="ffn_down",
    )(act, wd, x)


def _proj_body(x_ref, gain_ref, w_ref, pos_ref, invf_ref, qg_ref, kg_ref, pn_ref,
               q_ref, k_ref, v_ref, u_ref, us_ref):
    tm = x_ref.shape[0]
    h = _rms(x_ref[...], gain_ref[...]).astype(_BF16)
    group = 2 * MXU_DIM

    def project(g):
        return jnp.dot(h, w_ref[:, g * group:(g + 1) * group], preferred_element_type=_F32)

    half = HEAD_DIM // 2
    ang = pos_ref[...] * invf_ref[...]
    lane_q = lax.broadcasted_iota(jnp.int32, ang.shape, 1) // half

    def spread(table):
        parts = []
        for qtr in range(LANES // half):
            m = jnp.where(lane_q == qtr, table, 0.0)
            parts.append(m + pltpu.roll(m, half, 1) + pltpu.roll(m, 2 * half, 1) + pltpu.roll(m, 3 * half, 1))
        return jnp.concatenate(parts, axis=0)

    cos = spread(jnp.cos(ang))
    sin = spread(jnp.sin(ang))
    lane = lax.broadcasted_iota(jnp.int32, (tm, LANES), 1)
    first_half = (lane & (HEAD_DIM // 2)) == 0
    low_head = lane < HEAD_DIM
    sin_signed = jnp.where(first_half, -sin, sin)

    def norm_rotary(x4, gain):
        ms4 = jnp.dot((x4 * x4).astype(_BF16), pn_ref[...], preferred_element_type=_F32)
        out = []
        for part in range(2):
            lanes = slice(part * LANES, (part + 1) * LANES)
            y = x4[:, lanes] * lax.rsqrt(ms4[:, lanes] + EPS) * gain
            swapped = jnp.where(first_half, pltpu.roll(y, LANES - HEAD_DIM // 2, 1),
                                pltpu.roll(y, HEAD_DIM // 2, 1))
            out.append(y * cos + swapped * sin_signed)
        return out

    def dup_heads(xc):
        r = pltpu.roll(xc, HEAD_DIM, 1)
        return jnp.where(low_head, xc, r), jnp.where(low_head, r, xc)

    scale = 1.0 / math.sqrt(HEAD_DIM)
    assert KV_WIDTH == MXU_DIM and 2 * KV_WIDTH == group
    for g in range(ATTN_WIDTH // group):
        pg = project(g)
        for c in range(group // MXU_DIM):
            for part, qc in enumerate(norm_rotary(pg[:, c * MXU_DIM:(c + 1) * MXU_DIM], qg_ref[...])):
                at = g * group + c * MXU_DIM + part * LANES
                q_ref[:, at:at + LANES] = (qc * scale).astype(_BF16)
    pg = project(ATTN_WIDTH // group)
    for part, kc in enumerate(norm_rotary(pg[:, :KV_WIDTH], kg_ref[...])):
        ka, kb = dup_heads(kc)
        k_ref[:, 2 * part * LANES:(2 * part + 1) * LANES] = ka.astype(_BF16)
        k_ref[:, (2 * part + 1) * LANES:(2 * part + 2) * LANES] = kb.astype(_BF16)
    for part in range(KV_WIDTH // LANES):
        va, vb = dup_heads(pg[:, KV_WIDTH + part * LANES:KV_WIDTH + (part + 1) * LANES])
        v_ref[:, 2 * part * LANES:(2 * part + 1) * LANES] = va.astype(_BF16)
        v_ref[:, (2 * part + 1) * LANES:(2 * part + 2) * LANES] = vb.astype(_BF16)
    first_u = (ATTN_WIDTH + 2 * KV_WIDTH) // group
    per_group = group // LANES
    for g in range(SSM_WIDTH // group):
        pg = project(first_u + g)
        for s in range(per_group):
            us_ref[g * per_group + s] = pg[:, s * LANES:(s + 1) * LANES]
        for t in range(SSM_CHUNK):
            for s in range(g * per_group, (g + 1) * per_group):
                rows = us_ref[s, pl.ds(t, tm // SSM_CHUNK, stride=SSM_CHUNK), :]
                u_ref[t, :, s * LANES:(s + 1) * LANES] = rows.astype(_BF16)


def _proj(x, gain, w_in, pos, invf, q_gain, k_gain, *, tm=512):
    L, D = x.shape
    C = w_in.shape[1]
    quarters = LANES // (HEAD_DIM // 2)
    pos_f = pos.astype(_F32).reshape(L // tm, quarters, tm // quarters).transpose(0, 2, 1)
    pos_f = jnp.repeat(pos_f, HEAD_DIM // 2, axis=2)
    head_of_lane = jnp.arange(MXU_DIM) // HEAD_DIM
    pn = jnp.where(head_of_lane[:, None] == head_of_lane[None, :], 1.0 / HEAD_DIM, 0.0).astype(_BF16)
    qg = jnp.tile(q_gain, LANES // HEAD_DIM).reshape(1, LANES)
    kg = jnp.tile(k_gain, LANES // HEAD_DIM).reshape(1, LANES)
    row = lambda i: (i, 0)
    fixed = lambda i: (0, 0)
    return pl.pallas_call(
        _proj_body,
        grid=(L // tm,),
        in_specs=[
            pl.BlockSpec((tm, D), row),
            pl.BlockSpec((1, D), fixed),
            pl.BlockSpec((D, C), fixed),
            pl.BlockSpec((None, tm // quarters, LANES), lambda i: (i, 0, 0)),
            pl.BlockSpec((1, LANES), fixed),
            pl.BlockSpec((1, LANES), fixed),
            pl.BlockSpec((1, LANES), fixed),
            pl.BlockSpec((MXU_DIM, MXU_DIM), fixed),
        ],
        out_specs=[
            pl.BlockSpec((tm, ATTN_WIDTH), row),
            pl.BlockSpec((tm, 2 * KV_WIDTH), row),
            pl.BlockSpec((tm, 2 * KV_WIDTH), row),
            pl.BlockSpec((SSM_CHUNK, tm // SSM_CHUNK, SSM_WIDTH), lambda i: (0, i, 0)),
        ],
        out_shape=[
            jax.ShapeDtypeStruct((L, ATTN_WIDTH), _BF16),
            jax.ShapeDtypeStruct((L, 2 * KV_WIDTH), _BF16),
            jax.ShapeDtypeStruct((L, 2 * KV_WIDTH), _BF16),
            jax.ShapeDtypeStruct((SSM_CHUNK, L // SSM_CHUNK, SSM_WIDTH), _BF16),
        ],
        scratch_shapes=[pltpu.VMEM((SSM_WIDTH // LANES, tm, LANES), _F32)],
        compiler_params=pltpu.CompilerParams(
            dimension_semantics=("arbitrary",), vmem_limit_bytes=VMEM_LIMIT),
        name="proj",
    )(x, gain.reshape(1, D), w_in, pos_f, invf, qg, kg, pn)


def _row_cast_specs(arrays, n):
    pack = 2 * SUBLANES
    for w in arrays:
        assert w.shape[0] % (n * pack) == 0, w.shape
    specs = [pl.BlockSpec((w.shape[0] // n, w.shape[1]), lambda i: (i, 0)) for w in arrays]
    return specs, [jax.ShapeDtypeStruct(w.shape, _BF16) for w in arrays]


def _attn_body(n_cast, sink_ref, q_ref, kc_ref, vc_ref, kp_ref, vp_ref, *refs):
    o_ref = refs[n_cast]
    for src, dst in zip(refs[:n_cast], refs[n_cast + 1:]):
        dst[...] = src[...].astype(_BF16)
    tq = q_ref.shape[0]
    kj = lax.broadcasted_iota(jnp.int32, (BLOCK, BLOCK), 0)
    qi = lax.broadcasted_iota(jnp.int32, (BLOCK, BLOCK), 1)
    from_prev = kj > qi
    has_prev = pl.program_id(0) > 0
    low_head = lax.broadcasted_iota(jnp.int32, (2 * BLOCK, LANES), 1) < HEAD_DIM
    nt = (((1,), (1,)), ((), ()))
    tn = (((0,), (0,)), ((), ()))

    for b in range(tq // BLOCK):
        rows = slice(b * BLOCK, (b + 1) * BLOCK)
        for hk in range(N_KV_HEADS):
            cols = slice(hk * LANES, (hk + 1) * LANES)
            if b == 0:
                kd = jnp.concatenate([kp_ref[:, cols], kc_ref[0:BLOCK, cols]], axis=0)
                vd = jnp.concatenate([vp_ref[:, cols], vc_ref[0:BLOCK, cols]], axis=0)
            else:
                kd = kc_ref[(b - 1) * BLOCK:(b + 1) * BLOCK, cols]
                vd = vc_ref[(b - 1) * BLOCK:(b + 1) * BLOCK, cols]
            zero = jnp.zeros_like(kd)
            k_half = (jnp.where(low_head, kd, zero), jnp.where(low_head, zero, kd))
            v_half = (jnp.where(low_head, vd, zero), jnp.where(low_head, zero, vd))
            for pp in range(2):
                pair = hk * 2 + pp
                qp = q_ref[rows, pair * LANES:(pair + 1) * LANES]
                acc = None
                for half in range(2):
                    sink = sink_ref[pair * 2 + half]
                    s2 = lax.dot_general(k_half[half], qp, nt, preferred_element_type=_F32)
                    s_prev = s2[:BLOCK]
                    if b == 0:
                        s_prev = jnp.where(has_prev, s_prev, -jnp.inf)
                    s = jnp.where(from_prev, s_prev, s2[BLOCK:])
                    m = jnp.maximum(jnp.max(s, axis=0, keepdims=True), sink)
                    p = jnp.exp(s - m)
                    den = jnp.sum(p, axis=0, keepdims=True) + jnp.exp(sink - m)
                    pb = p.astype(_BF16)
                    pz = jnp.zeros_like(pb)
                    p2 = jnp.concatenate([jnp.where(from_prev, pb, pz), jnp.where(from_prev, pz, pb)], axis=0)
                    o = lax.dot_general(v_half[half], p2, tn, preferred_element_type=_F32) * (1.0 / den)
                    acc = o if acc is None else acc + o
                o_ref[rows, pair * LANES:(pair + 1) * LANES] = jnp.transpose(acc).astype(_BF16)


def _attn(q, k2, v2, sinks, to_cast=(), *, tq=512):
    L = q.shape[0]
    per = tq // BLOCK
    row = lambda i: (i, 0)
    prev = lambda i: (jnp.maximum(i * per - 1, 0), 0)
    cast_specs, cast_shapes = _row_cast_specs(to_cast, L // tq)
    out = pl.pallas_call(
        functools.partial(_attn_body, len(to_cast)),
        grid=(L // tq,),
        in_specs=[
            pl.BlockSpec(memory_space=pltpu.SMEM),
            pl.BlockSpec((tq, ATTN_WIDTH), row),
            pl.BlockSpec((tq, 2 * KV_WIDTH), row),
            pl.BlockSpec((tq, 2 * KV_WIDTH), row),
            pl.BlockSpec((BLOCK, 2 * KV_WIDTH), prev),
            pl.BlockSpec((BLOCK, 2 * KV_WIDTH), prev),
        ] + cast_specs,
        out_specs=[pl.BlockSpec((tq, ATTN_WIDTH), row)] + cast_specs,
        out_shape=[jax.ShapeDtypeStruct((L, ATTN_WIDTH), _BF16)] + cast_shapes,
        compiler_params=pltpu.CompilerParams(
            dimension_semantics=("arbitrary",), vmem_limit_bytes=VMEM_LIMIT),
        name="attn",
    )(sinks, q, k2, v2, k2, v2, *to_cast)
    return out[0], out[1:]


def _ssm_body(u_ref, wst_ref, kw_ref, cp_ref, wg_ref, bg_ref, a_ref, aseg_ref, o_ref,
              lhs_ref, toep_ref, s_ref, xb_ref):
    T = SSM_CHUNK
    n_c = u_ref.shape[1]
    seg = n_c // SSM_SEGMENTS
    pitch = s_ref.shape[1] // SSM_SEGMENTS
    n_state_slabs = SLAB_STATES // LANES
    pair_w = 2 * LANES
    out_w = 2 * MXU_DIM

    for t in range(T):
        lhs_ref[:, t * LANES:(t + 1) * LANES] = u_ref[t]

    toep_ref[...] = jnp.zeros(toep_ref.shape, _BF16)
    for t in range(T):
        for tp in range(t, T):
            toep_ref[t * LANES:(t + 1) * LANES, tp * LANES:(tp + 1) * LANES] = (
                kw_ref[:, (tp - t) * LANES:(tp - t + 1) * LANES])

    per_dot = out_w // LANES
    for nb in range(2 * n_state_slabs // per_dot):
        res = jnp.dot(lhs_ref[...], wst_ref[:, nb * out_w:(nb + 1) * out_w], preferred_element_type=_F32)
        for part in range(per_dot):
            for j in range(SSM_SEGMENTS):
                s_ref[per_dot * nb + part, j * pitch:j * pitch + seg, :] = (
                    res[j * seg:(j + 1) * seg, part * LANES:(part + 1) * LANES])

    shape = (SSM_SEGMENTS, LANES)
    a_re = [jnp.broadcast_to(a_ref[0:1, k * LANES:(k + 1) * LANES], shape) for k in range(n_state_slabs)]
    a_im = [jnp.broadcast_to(a_ref[1:2, k * LANES:(k + 1) * LANES], shape) for k in range(n_state_slabs)]
    g_re = [jnp.broadcast_to(aseg_ref[0:1, k * LANES:(k + 1) * LANES], shape) for k in range(n_state_slabs)]
    g_im = [jnp.broadcast_to(aseg_ref[1:2, k * LANES:(k + 1) * LANES], shape) for k in range(n_state_slabs)]

    def seg_rows(w):
        return pl.ds(w, SSM_SEGMENTS, stride=pitch)

    def advance(w, carry, store):
        out = []
        for k in range(n_state_slabs):
            z_re, z_im = carry[2 * k], carry[2 * k + 1]
            s_re = s_ref[k, seg_rows(w), :]
            s_im = s_ref[n_state_slabs + k, seg_rows(w), :]
            if store:
                s_ref[k, seg_rows(w), :] = z_re
                s_ref[n_state_slabs + k, seg_rows(w), :] = z_im
            out.append(a_re[k] * z_re - a_im[k] * z_im + s_re)
            out.append(a_re[k] * z_im + a_im[k] * z_re + s_im)
        return tuple(out)

    zero = jnp.zeros(shape, _F32)
    ends = lax.fori_loop(0, seg, lambda w, c: advance(w, c, False), (zero,) * (2 * n_state_slabs),
                         unroll=SCAN_UNROLL)

    segidx = lax.broadcasted_iota(jnp.int32, shape, 0)
    init = []
    for k in range(n_state_slabs):
        i_re, i_im = zero, zero
        f_re, f_im = ends[2 * k], ends[2 * k + 1]
        for j in range(SSM_SEGMENTS - 1):
            c_re = g_re[k] * i_re - g_im[k] * i_im + f_re
            c_im = g_re[k] * i_im + g_im[k] * i_re + f_im
            i_re = jnp.where(segidx == j + 1, pltpu.roll(c_re, 1, 0), i_re)
            i_im = jnp.where(segidx == j + 1, pltpu.roll(c_im, 1, 0), i_im)
        init += [i_re, i_im]

    lax.fori_loop(0, seg, lambda w, c: advance(w, c, True), tuple(init), unroll=SCAN_UNROLL)

    for col in range(2 * n_state_slabs):
        for j in range(SSM_SEGMENTS):
            xb_ref[j * seg:(j + 1) * seg, col * LANES:(col + 1) * LANES] = (
                s_ref[col, j * pitch:j * pitch + seg, :].astype(_BF16))

    steps = out_w // LANES
    for i in range(T // steps):
        kk = (i + 1) * out_w
        cols = slice(i * out_w, (i + 1) * out_w)
        y = (jnp.dot(lhs_ref[:, :kk], toep_ref[:kk, cols], preferred_element_type=_F32)
             + jnp.dot(xb_ref[...], cp_ref[:, cols], preferred_element_type=_F32))
        z = jax.nn.gelu(y, approximate=True).astype(_BF16)
        for pp in range(steps // 2):
            gt = jnp.dot(z[:, pp * pair_w:(pp + 1) * pair_w], wg_ref[...],
                         preferred_element_type=_F32) + bg_ref[...]
            out = gt[:, :pair_w] * jax.nn.sigmoid(gt[:, pair_w:])
            t0 = i * steps + 2 * pp
            o_ref[t0] = out[:, :LANES].astype(_BF16)
            o_ref[t0 + 1] = out[:, LANES:].astype(_BF16)


def _complex_power(re, im, n):
    out_re, out_im = None, None
    while n:
        if n & 1:
            if out_re is None:
                out_re, out_im = re, im
            else:
                out_re, out_im = out_re * re - out_im * im, out_re * im + out_im * re
        n >>= 1
        if n:
            re, im = re * re - im * im, 2.0 * re * im
    return out_re, out_im


def _ssm_prep_body(seg, rows_ref, bt_ref, ct_ref, d_ref, wt_ref, bglu_ref,
                   wst_ref, kw_ref, cp_ref, wg_ref, bg_ref, ach_ref, aseg_ref):
    T, H = SSM_CHUNK, SSM_GROUP
    a_re, a_im = rows_ref[0:1, :], rows_ref[1:2, :]
    dt = jnp.exp(rows_ref[2:3, :])
    mag = jnp.exp(a_re * dt)
    ab_re = mag * jnp.cos(a_im * dt)
    ab_im = mag * jnp.sin(a_im * dt)
    inv_den = 1.0 / (a_re * a_re + a_im * a_im)
    nr = ab_re - 1.0
    coef_re = (nr * a_re + ab_im * a_im) * inv_den
    coef_im = (ab_im * a_re - nr * a_im) * inv_den

    def block_diag(t16, shape):
        row_g = lax.broadcasted_iota(jnp.int32, shape, 0) // H
        col_g = lax.broadcasted_iota(jnp.int32, shape, 1) // (shape[1] // SLAB_GROUPS)
        return jnp.where(row_g == col_g, jnp.concatenate([t16] * SLAB_GROUPS, axis=0), 0.0)

    cs = (LANES, SLAB_STATES)
    b_re, b_im = block_diag(bt_ref[0], cs), block_diag(bt_ref[1], cs)
    bb_re = coef_re * b_re - coef_im * b_im
    bb_im = coef_re * b_im + coef_im * b_re
    ck_re = jnp.transpose(block_diag(ct_ref[0], cs))
    ck_im = jnp.transpose(block_diag(ct_ref[1], cs))
    col_re = jnp.transpose(jnp.broadcast_to(ab_re, cs))
    col_im = jnp.transpose(jnp.broadcast_to(ab_im, cs))

    pw = [(jnp.ones_like(ab_re), jnp.zeros_like(ab_im))]
    for _ in range(T):
        pr, pi = pw[-1]
        pw.append((pr * ab_re - pi * ab_im, pr * ab_im + pi * ab_re))

    for t in range(T):
        pr, pi = pw[T - 1 - t]
        wst_ref[t * LANES:(t + 1) * LANES, :SLAB_STATES] = (bb_re * pr - bb_im * pi).astype(_BF16)
        wst_ref[t * LANES:(t + 1) * LANES, SLAB_STATES:] = (bb_re * pi + bb_im * pr).astype(_BF16)

    lhs = jnp.concatenate([bb_re, -bb_im], axis=1)
    eye = (lax.broadcasted_iota(jnp.int32, (LANES, LANES), 0)
           == lax.broadcasted_iota(jnp.int32, (LANES, LANES), 1))
    for k in range(T + 1):
        if k < T:
            kern = jnp.dot(lhs, jnp.concatenate([ck_re, ck_im], axis=0), precision=_HI,
                           preferred_element_type=_F32)
            if k == 0:
                kern = kern + jnp.where(eye, d_ref[...], 0.0)
            kw_ref[:, k * LANES:(k + 1) * LANES] = kern.astype(_BF16)
        if k > 0:
            cp_ref[:SLAB_STATES, (k - 1) * LANES:k * LANES] = ck_re.astype(_BF16)
            cp_ref[SLAB_STATES:, (k - 1) * LANES:k * LANES] = (-ck_im).astype(_BF16)
        ck_re, ck_im = ck_re * col_re - ck_im * col_im, ck_re * col_im + ck_im * col_re

    cc = (LANES, LANES)
    w_lin = jnp.transpose(block_diag(wt_ref[0], cc)).astype(_BF16)
    w_gate = jnp.transpose(block_diag(wt_ref[1], cc)).astype(_BF16)
    wg_ref[...] = jnp.zeros(wg_ref.shape, _BF16)
    for t in range(2):
        wg_ref[t * LANES:(t + 1) * LANES, t * LANES:(t + 1) * LANES] = w_lin
        wg_ref[t * LANES:(t + 1) * LANES, (2 + t) * LANES:(3 + t) * LANES] = w_gate
    bg_ref[...] = jnp.concatenate([bglu_ref[0:1, :], bglu_ref[0:1, :], bglu_ref[1:2, :], bglu_ref[1:2, :]],
                                  axis=1)

    ach_ref[0:1, :], ach_ref[1:2, :] = pw[T]
    aseg_ref[0:1, :], aseg_ref[1:2, :] = _complex_power(pw[T][0], pw[T][1], seg)


def _ssm_weights(log_dt, a_re, a_im, b_re, b_im, c_re, c_im, d_skip, w_glu, b_glu, seg):
    P, H, T = SSM_STATE, SSM_GROUP, SSM_CHUNK
    SG, NS = SLAB_GROUPS, SSM_GROUPS // SLAB_GROUPS
    rows = jnp.stack([a_re.reshape(NS, SG * P), a_im.reshape(NS, SG * P),
                      jnp.repeat(log_dt, P).reshape(NS, SG * P)], axis=1)

    def per_group_rows(w, lead):
        t = w.reshape((NS, SG) + w.shape[1:])
        t = jnp.moveaxis(t, 2 + lead, 1)
        return t.reshape(NS, t.shape[1], -1)

    bt = jnp.stack([per_group_rows(b_re, 1), per_group_rows(b_im, 1)], axis=1)
    ct = jnp.stack([per_group_rows(c_re, 0), per_group_rows(c_im, 0)], axis=1)
    wt = jnp.stack([per_group_rows(w_glu[..., :H], 1), per_group_rows(w_glu[..., H:], 1)], axis=1)
    bglu = jnp.stack([b_glu[:, :H].reshape(NS, SG * H), b_glu[:, H:].reshape(NS, SG * H)], axis=1)
    d = d_skip.reshape(NS, 1, SG * H)

    blk3 = lambda i: (i, 0, 0)
    blk4 = lambda i: (i, 0, 0, 0)
    return pl.pallas_call(
        functools.partial(_ssm_prep_body, seg),
        grid=(NS,),
        in_specs=[
            pl.BlockSpec((None, 3, SG * P), blk3),
            pl.BlockSpec((None, 2, H, SG * P), blk4),
            pl.BlockSpec((None, 2, H, SG * P), blk4),
            pl.BlockSpec((None, 1, SG * H), blk3),
            pl.BlockSpec((None, 2, H, SG * H), blk4),
            pl.BlockSpec((None, 2, SG * H), blk3),
        ],
        out_specs=[
            pl.BlockSpec((None, T * LANES, 2 * SLAB_STATES), blk3),
            pl.BlockSpec((None, LANES, T * LANES), blk3),
            pl.BlockSpec((None, 2 * SLAB_STATES, T * LANES), blk3),
            pl.BlockSpec((None, 2 * LANES, 4 * LANES), blk3),
            pl.BlockSpec((None, 1, 4 * LANES), blk3),
            pl.BlockSpec((None, 2, SLAB_STATES), blk3),
            pl.BlockSpec((None, 2, SLAB_STATES), blk3),
        ],
        out_shape=[
            jax.ShapeDtypeStruct((NS, T * LANES, 2 * SLAB_STATES), _BF16),
            jax.ShapeDtypeStruct((NS, LANES, T * LANES), _BF16),
            jax.ShapeDtypeStruct((NS, 2 * SLAB_STATES, T * LANES), _BF16),
            jax.ShapeDtypeStruct((NS, 2 * LANES, 4 * LANES), _BF16),
            jax.ShapeDtypeStruct((NS, 1, 4 * LANES), _F32),
            jax.ShapeDtypeStruct((NS, 2, SLAB_STATES), _F32),
            jax.ShapeDtypeStruct((NS, 2, SLAB_STATES), _F32),
        ],
        compiler_params=pltpu.CompilerParams(
            dimension_semantics=("arbitrary",), vmem_limit_bytes=VMEM_LIMIT),
        name="ssm_prep",
    )(rows, bt, ct, d, wt, bglu)


def _ssm(u_t, weights):
    T, n_c, W = u_t.shape
    seg = n_c // SSM_SEGMENTS
    pitch = seg + SUBLANES
    ns = W // LANES
    wst, kw, cp, wg, bg, a_chunk, a_seg = weights
    slab = lambda i: (0, 0, i)
    blk = lambda i: (i, 0, 0)
    once = pl.Buffered(1)
    return pl.pallas_call(
        _ssm_body,
        grid=(ns,),
        in_specs=[
            pl.BlockSpec((T, n_c, LANES), slab, pipeline_mode=once),
            pl.BlockSpec((None, T * LANES, 2 * SLAB_STATES), blk, pipeline_mode=once),
            pl.BlockSpec((None, LANES, T * LANES), blk),
            pl.BlockSpec((None, 2 * SLAB_STATES, T * LANES), blk, pipeline_mode=once),
            pl.BlockSpec((None, 2 * LANES, 4 * LANES), blk),
            pl.BlockSpec((None, 1, 4 * LANES), blk),
            pl.BlockSpec((None, 2, SLAB_STATES), blk),
            pl.BlockSpec((None, 2, SLAB_STATES), blk),
        ],
        out_specs=pl.BlockSpec((T, n_c, LANES), slab),
        out_shape=jax.ShapeDtypeStruct((T, n_c, W), _BF16),
        scratch_shapes=[
            pltpu.VMEM((n_c, T * LANES), _BF16),
            pltpu.VMEM((T * LANES, T * LANES), _BF16),
            pltpu.VMEM((2 * SLAB_STATES // LANES, SSM_SEGMENTS * pitch, LANES), _F32),
            pltpu.VMEM((n_c, 2 * SLAB_STATES), _BF16),
        ],
        compiler_params=pltpu.CompilerParams(
            dimension_semantics=("arbitrary",), vmem_limit_bytes=VMEM_LIMIT),
        name="ssm",
    )(u_t, wst, kw, cp, wg, bg, a_chunk, a_seg)


def _outproj_body(n_cast, a_ref, s_ref, x_ref, ag_ref, sg_ref, w_ref, *refs):
    o_ref, sn_ref = refs[n_cast], refs[-1]
    for src, dst in zip(refs[:n_cast], refs[n_cast + 1:-1]):
        dst[...] = src[...].astype(_BF16)
    tm = x_ref.shape[0]
    an = _rms(a_ref[...].astype(_F32), ag_ref[...]).astype(_BF16)
    for t in range(SSM_CHUNK):
        sn_t = _rms(s_ref[t].astype(_F32), sg_ref[...])
        for s in range(SSM_WIDTH // LANES):
            sn_ref[s, pl.ds(t, tm // SSM_CHUNK, stride=SSM_CHUNK), :] = sn_t[:, s * LANES:(s + 1) * LANES]
    sn = jnp.concatenate([sn_ref[s] for s in range(SSM_WIDTH // LANES)], axis=1).astype(_BF16)
    mixed = jnp.concatenate([an, sn], axis=1)
    o_ref[...] = x_ref[...] + jnp.dot(mixed, w_ref[...], preferred_element_type=_F32)


def _outproj(attn, ssm_t, x, a_gain, s_gain, w_out, to_cast=(), *, tm=512):
    L, D = x.shape
    row = lambda i: (i, 0)
    fixed = lambda i: (0, 0)
    cast_specs, cast_shapes = _row_cast_specs(to_cast, L // tm)
    out = pl.pallas_call(
        functools.partial(_outproj_body, len(to_cast)),
        grid=(L // tm,),
        in_specs=[
            pl.BlockSpec((tm, ATTN_WIDTH), row),
            pl.BlockSpec((SSM_CHUNK, tm // SSM_CHUNK, SSM_WIDTH), lambda i: (0, i, 0)),
            pl.BlockSpec((tm, D), row),
            pl.BlockSpec((1, ATTN_WIDTH), fixed),
            pl.BlockSpec((1, SSM_WIDTH), fixed),
            pl.BlockSpec((ATTN_WIDTH + SSM_WIDTH, D), fixed),
        ] + cast_specs,
        out_specs=[pl.BlockSpec((tm, D), row)] + cast_specs,
        out_shape=[jax.ShapeDtypeStruct((L, D), _F32)] + cast_shapes,
        scratch_shapes=[pltpu.VMEM((SSM_WIDTH // LANES, tm, LANES), _F32)],
        compiler_params=pltpu.CompilerParams(
            dimension_semantics=("arbitrary",), vmem_limit_bytes=VMEM_LIMIT),
        name="outproj",
    )(attn, ssm_t, x, a_gain.reshape(1, -1), s_gain.reshape(1, -1), w_out, *to_cast)
    return out[0], out[1:]


def _layer(x, pos, invf, p):
    L = x.shape[0]
    seg = L // (SSM_SEGMENTS * SSM_CHUNK)
    act, (w_down1, w_in, w_out) = _ffn_up(
        x, p['ffn1_norm'], p['ffn1_w_gate'].astype(_BF16), p['ffn1_w_up'].astype(_BF16),
        [p['ffn1_w_down'], p['w_in'], p['w_out']])
    x = _ffn_down(act, w_down1, x)
    q, k2, v2, u_t = _proj(x, p['mix_norm'], w_in, pos, invf, p['q_norm'], p['k_norm'])
    attn, (w_gate2, w_up2) = _attn(q, k2, v2, p['attn_sinks'], [p['ffn2_w_gate'], p['ffn2_w_up']])
    weights = _ssm_weights(p['ssm_log_dt'], p['ssm_a_re'], p['ssm_a_im'], p['ssm_b_re'], p['ssm_b_im'],
                           p['ssm_c_re'], p['ssm_c_im'], p['ssm_d'], p['ssm_w_glu'], p['ssm_b_glu'], seg)
    ssm_t = _ssm(u_t, weights)
    x, (w_down2,) = _outproj(attn, ssm_t, x, p['attn_out_norm'], p['ssm_out_norm'], w_out,
                             [p['ffn2_w_down']])
    act, _ = _ffn_up(x, p['ffn2_norm'], w_gate2, w_up2)
    return _ffn_down(act, w_down2, x)


def kernel(x, positions, ffn1_norm, ffn1_w_gate, ffn1_w_up, ffn1_w_down, mix_norm, w_in, q_norm, k_norm,
           attn_sinks, ssm_log_dt, ssm_a_re, ssm_a_im, ssm_b_re, ssm_b_im, ssm_c_re, ssm_c_im, ssm_d,
           ssm_w_glu, ssm_b_glu, attn_out_norm, ssm_out_norm, w_out, ffn2_norm, ffn2_w_gate, ffn2_w_up,
           ffn2_w_down):
    params = dict(
        ffn1_norm=ffn1_norm, ffn1_w_gate=ffn1_w_gate, ffn1_w_up=ffn1_w_up, ffn1_w_down=ffn1_w_down,
        mix_norm=mix_norm, w_in=w_in, q_norm=q_norm, k_norm=k_norm, attn_sinks=attn_sinks,
        ssm_log_dt=ssm_log_dt, ssm_a_re=ssm_a_re, ssm_a_im=ssm_a_im, ssm_b_re=ssm_b_re, ssm_b_im=ssm_b_im,
        ssm_c_re=ssm_c_re, ssm_c_im=ssm_c_im, ssm_d=ssm_d, ssm_w_glu=ssm_w_glu, ssm_b_glu=ssm_b_glu,
        attn_out_norm=attn_out_norm, ssm_out_norm=ssm_out_norm, w_out=w_out,
        ffn2_norm=ffn2_norm, ffn2_w_gate=ffn2_w_gate, ffn2_w_up=ffn2_w_up, ffn2_w_down=ffn2_w_down)
    depth = ffn1_norm.shape[0]
    half = HEAD_DIM // 2
    inv_freq = ROPE_THETA ** (-jnp.arange(half, dtype=_F32) * 2.0 / HEAD_DIM)
    invf = jnp.tile(inv_freq, LANES // half).reshape(1, LANES)
    outs = []
    for b in range(x.shape[0]):
        xb = x[b]
        for i in range(depth):
            xb = _layer(xb, positions[b], invf, {name: val[i] for name, val in params.items()})
        outs.append(xb)
    return jnp.stack(outs, axis=0)
```

```python
import functools
import math

import jax
import jax.numpy as jnp
from jax import lax
from jax.experimental import pallas as pl
from jax.experimental.pallas import tpu as pltpu

HEAD_DIM = 64
N_Q_HEADS = 16
N_KV_HEADS = 4
ATTN_WIDTH = N_Q_HEADS * HEAD_DIM
KV_WIDTH = N_KV_HEADS * HEAD_DIM
BLOCK = 128
ROPE_THETA = 10000.0
SSM_GROUP = 16
SSM_GROUPS = 64
SSM_STATE = 64
SSM_WIDTH = SSM_GROUP * SSM_GROUPS
FFN_RESIDUAL = 0.5
EPS = 1e-6

LANES = 128
SUBLANES = 8
MXU_DIM = 256
SSM_CHUNK = 8
SSM_SEGMENTS = SUBLANES
SLAB_GROUPS = LANES // SSM_GROUP
SLAB_STATES = SLAB_GROUPS * SSM_STATE
SCAN_UNROLL = 4
VMEM_LIMIT = 56 * 1024 * 1024

_BF16 = jnp.bfloat16
_F32 = jnp.float32


def _rms(x, gain):
    ms = jnp.mean(x * x, axis=-1, keepdims=True)
    return x * lax.rsqrt(ms + EPS) * gain


def _ffn_up_body(cast_every_step, x_ref, gain_ref, wg_ref, wu_ref, *refs):
    n = len(cast_every_step)
    src_refs, a_ref, dst_refs, h_ref = refs[:n], refs[n], refs[n + 1:2 * n + 1], refs[2 * n + 1]
    first = pl.program_id(1) == 0

    @pl.when(first)
    def _():
        h_ref[...] = _rms(x_ref[...], gain_ref[...]).astype(_BF16)

    h = h_ref[...]
    g = jnp.dot(h, wg_ref[...], preferred_element_type=_F32)
    u = jnp.dot(h, wu_ref[...], preferred_element_type=_F32)
    a_ref[...] = (g * jax.nn.sigmoid(g) * u).astype(_BF16)

    for every_step, src, dst in zip(cast_every_step, src_refs, dst_refs):
        if every_step:
            dst[...] = src[...].astype(_BF16)
        else:
            @pl.when(first)
            def _(src=src, dst=dst):
                dst[...] = src[...].astype(_BF16)


def _ffn_down_body(a_ref, wd_ref, x_ref, o_ref):
    d = jnp.dot(a_ref[...], wd_ref[...], preferred_element_type=_F32)
    o_ref[...] = x_ref[...] + FFN_RESIDUAL * d


def _cast_spec(shape, ni, nj):
    R, C = shape
    pack = 2 * SUBLANES
    if R % ni == 0 and (R // ni) % pack == 0 and C % nj == 0 and (C // nj) % LANES == 0:
        return pl.BlockSpec((R // ni, C // nj), lambda i, j: (i, j)), True
    if R % (ni * nj) == 0 and (R // (ni * nj)) % pack == 0:
        return pl.BlockSpec((R // (ni * nj), C), lambda i, j: (i * nj + j, 0)), True
    assert R % ni == 0 and (R // ni) % pack == 0, shape
    return pl.BlockSpec((R // ni, C), lambda i, j: (i, 0)), False


def _ffn_up(x, gain, wg, wu, to_cast=(), *, tm=1024, tf=512):
    L, D = x.shape
    F = wg.shape[1]
    tm = min(tm, L)
    ni, nj = L // tm, F // tf
    cast_specs = [_cast_spec(w.shape, ni, nj) for w in to_cast]
    out = pl.pallas_call(
        functools.partial(_ffn_up_body, tuple(every for _, every in cast_specs)),
        grid=(ni, nj),
        in_specs=[
            pl.BlockSpec((tm, D), lambda i, j: (i, 0)),
            pl.BlockSpec((1, D), lambda i, j: (0, 0)),
            pl.BlockSpec((D, tf), lambda i, j: (0, j)),
            pl.BlockSpec((D, tf), lambda i, j: (0, j)),
        ] + [spec for spec, _ in cast_specs],
        out_specs=[pl.BlockSpec((tm, tf), lambda i, j: (i, j))] + [spec for spec, _ in cast_specs],
        out_shape=[jax.ShapeDtypeStruct((L, F), _BF16)]
        + [jax.ShapeDtypeStruct(w.shape, _BF16) for w in to_cast],
        scratch_shapes=[pltpu.VMEM((tm, D), _BF16)],
        compiler_params=pltpu.CompilerParams(
            dimension_semantics=("arbitrary", "arbitrary"), vmem_limit_bytes=VMEM_LIMIT),
        name="ffn_up",
    )(x, gain.reshape(1, D), wg, wu, *to_cast)
    return out[0], out[1:]


def _ffn_down(act, wd, x, *, tm=512):
    L, D = x.shape
    F = act.shape[1]
    tm = min(tm, L)
    return pl.pallas_call(
        _ffn_down_body,
        grid=(L // tm,),
        in_specs=[
            pl.BlockSpec((tm, F), lambda i: (i, 0)),
            pl.BlockSpec((F, D), lambda i: (0, 0), pipeline_mode=pl.Buffered(1)),
            pl.BlockSpec((tm, D), lambda i: (i, 0)),
        ],
        out_specs=pl.BlockSpec((tm, D), lambda i: (i, 0)),
        out_shape=jax.ShapeDtypeStruct((L, D), _F32),
        compiler_params=pltpu.CompilerParams(
            dimension_semantics=("arbitrary",), vmem_limit_bytes=VMEM_LIMIT),
        name="ffn_down",
    )(act, wd, x)


def _proj_body(x_ref, gain_ref, w_ref, pos_ref, invf_ref, qg_ref, kg_ref, pn_ref,
               q_ref, k_ref, v_ref, u_ref, us_ref):
    tm = x_ref.shape[0]
    h = _rms(x_ref[...], gain_ref[...]).astype(_BF16)
    group = 2 * MXU_DIM

    def project(g):
        return jnp.dot(h, w_ref[:, g * group:(g + 1) * group], preferred_element_type=_F32)

    half = HEAD_DIM // 2
    ang = pos_ref[...] * invf_ref[...]
    lane_q = lax.broadcasted_iota(jnp.int32, ang.shape, 1) // half

    def spread(table):
        parts = []
        for qtr in range(LANES // half):
            m = jnp.where(lane_q == qtr, table, 0.0)
            parts.append(m + pltpu.roll(m, half, 1) + pltpu.roll(m, 2 * half, 1) + pltpu.roll(m, 3 * half, 1))
        return jnp.concatenate(parts, axis=0)

    cos = spread(jnp.cos(ang))
    sin = spread(jnp.sin(ang))
    lane = lax.broadcasted_iota(jnp.int32, (tm, LANES), 1)
    first_half = (lane & (HEAD_DIM // 2)) == 0
    low_head = lane < HEAD_DIM
    sin_signed = jnp.where(first_half, -sin, sin)

    def norm_rotary(x4, gain):
        ms4 = jnp.dot((x4 * x4).astype(_BF16), pn_ref[...], preferred_element_type=_F32)
        out = []
        for part in range(2):
            lanes = slice(part * LANES, (part + 1) * LANES)
            y = x4[:, lanes] * lax.rsqrt(ms4[:, lanes] + EPS) * gain
            swapped = jnp.where(first_half, pltpu.roll(y, LANES - HEAD_DIM // 2, 1),
                                pltpu.roll(y, HEAD_DIM // 2, 1))
            out.append(y * cos + swapped * sin_signed)
        return out

    def dup_heads(xc):
        r = pltpu.roll(xc, HEAD_DIM, 1)
        return jnp.where(low_head, xc, r), jnp.where(low_head, r, xc)

    scale = 1.0 / math.sqrt(HEAD_DIM)
    assert KV_WIDTH == MXU_DIM and 2 * KV_WIDTH == group
    for g in range(ATTN_WIDTH // group):
        pg = project(g)
        for c in range(group // MXU_DIM):
            for part, qc in enumerate(norm_rotary(pg[:, c * MXU_DIM:(c + 1) * MXU_DIM], qg_ref[...])):
                at = g * group + c * MXU_DIM + part * LANES
                q_ref[:, at:at + LANES] = (qc * scale).astype(_BF16)
    pg = project(ATTN_WIDTH // group)
    for part, kc in enumerate(norm_rotary(pg[:, :KV_WIDTH], kg_ref[...])):
        ka, kb = dup_heads(kc)
        k_ref[:, 2 * part * LANES:(2 * part + 1) * LANES] = ka.astype(_BF16)
        k_ref[:, (2 * part + 1) * LANES:(2 * part + 2) * LANES] = kb.astype(_BF16)
    for part in range(KV_WIDTH // LANES):
        va, vb = dup_heads(pg[:, KV_WIDTH + part * LANES:KV_WIDTH + (part + 1) * LANES])
        v_ref[:, 2 * part * LANES:(2 * part + 1) * LANES] = va.astype(_BF16)
        v_ref[:, (2 * part + 1) * LANES:(2 * part + 2) * LANES] = vb.astype(_BF16)
    first_u = (ATTN_WIDTH + 2 * KV_WIDTH) // group
    per_group = group // LANES
    for g in range(SSM_WIDTH // group):
        pg = project(first_u + g)
        for s in range(per_group):
            us_ref[g * per_group + s] = pg[:, s * LANES:(s + 1) * LANES]
        for t in range(SSM_CHUNK):
            for s in range(g * per_group, (g + 1) * per_group):
                rows = us_ref[s, pl.ds(t, tm // SSM_CHUNK, stride=SSM_CHUNK), :]
                u_ref[t, :, s * LANES:(s + 1) * LANES] = rows.astype(_BF16)


def _proj(x, gain, w_in, pos, invf, q_gain, k_gain, *, tm=512):
    L, D = x.shape
    C = w_in.shape[1]
    quarters = LANES // (HEAD_DIM // 2)
    pos_f = pos.astype(_F32).reshape(L // tm, quarters, tm // quarters).transpose(0, 2, 1)
    pos_f = jnp.repeat(pos_f, HEAD_DIM // 2, axis=2)
    head_of_lane = jnp.arange(MXU_DIM) // HEAD_DIM
    pn = jnp.where(head_of_lane[:, None] == head_of_lane[None, :], 1.0 / HEAD_DIM, 0.0).astype(_BF16)
    qg = jnp.tile(q_gain, LANES // HEAD_DIM).reshape(1, LANES)
    kg = jnp.tile(k_gain, LANES // HEAD_DIM).reshape(1, LANES)
    row = lambda i: (i, 0)
    fixed = lambda i: (0, 0)
    return pl.pallas_call(
        _proj_body,
        grid=(L // tm,),
        in_specs=[
            pl.BlockSpec((tm, D), row),
            pl.BlockSpec((1, D), fixed),
            pl.BlockSpec((D, C), fixed),
            pl.BlockSpec((None, tm // quarters, LANES), lambda i: (i, 0, 0)),
            pl.BlockSpec((1, LANES), fixed),
            pl.BlockSpec((1, LANES), fixed),
            pl.BlockSpec((1, LANES), fixed),
            pl.BlockSpec((MXU_DIM, MXU_DIM), fixed),
        ],
        out_specs=[
            pl.BlockSpec((tm, ATTN_WIDTH), row),
            pl.BlockSpec((tm, 2 * KV_WIDTH), row),
            pl.BlockSpec((tm, 2 * KV_WIDTH), row),
            pl.BlockSpec((SSM_CHUNK, tm // SSM_CHUNK, SSM_WIDTH), lambda i: (0, i, 0)),
        ],
        out_shape=[
            jax.ShapeDtypeStruct((L, ATTN_WIDTH), _BF16),
            jax.ShapeDtypeStruct((L, 2 * KV_WIDTH), _BF16),
            jax.ShapeDtypeStruct((L, 2 * KV_WIDTH), _BF16),
            jax.ShapeDtypeStruct((SSM_CHUNK, L // SSM_CHUNK, SSM_WIDTH), _BF16),
        ],
        scratch_shapes=[pltpu.VMEM((SSM_WIDTH // LANES, tm, LANES), _F32)],
        compiler_params=pltpu.CompilerParams(
            dimension_semantics=("arbitrary",), vmem_limit_bytes=VMEM_LIMIT),
        name="proj",
    )(x, gain.reshape(1, D), w_in, pos_f, invf, qg, kg, pn)


def _row_cast_specs(arrays, n):
    pack = 2 * SUBLANES
    for w in arrays:
        assert w.shape[0] % (n * pack) == 0, w.shape
    specs = [pl.BlockSpec((w.shape[0] // n, w.shape[1]), lambda i: (i, 0)) for w in arrays]
    return specs, [jax.ShapeDtypeStruct(w.shape, _BF16) for w in arrays]


def _attn_body(n_cast, sink_ref, q_ref, kc_ref, vc_ref, kp_ref, vp_ref, *refs):
    o_ref = refs[n_cast]
    for src, dst in zip(refs[:n_cast], refs[n_cast + 1:]):
        dst[...] = src[...].astype(_BF16)
    tq = q_ref.shape[0]
    kj = lax.broadcasted_iota(jnp.int32, (BLOCK, BLOCK), 0)
    qi = lax.broadcasted_iota(jnp.int32, (BLOCK, BLOCK), 1)
    from_prev = kj > qi
    has_prev = pl.program_id(0) > 0
    low_head = lax.broadcasted_iota(jnp.int32, (2 * BLOCK, LANES), 1) < HEAD_DIM
    nt = (((1,), (1,)), ((), ()))
    tn = (((0,), (0,)), ((), ()))

    for b in range(tq // BLOCK):
        rows = slice(b * BLOCK, (b + 1) * BLOCK)
        for hk in range(N_KV_HEADS):
            cols = slice(hk * LANES, (hk + 1) * LANES)
            if b == 0:
                kd = jnp.concatenate([kp_ref[:, cols], kc_ref[0:BLOCK, cols]], axis=0)
                vd = jnp.concatenate([vp_ref[:, cols], vc_ref[0:BLOCK, cols]], axis=0)
            else:
                kd = kc_ref[(b - 1) * BLOCK:(b + 1) * BLOCK, cols]
                vd = vc_ref[(b - 1) * BLOCK:(b + 1) * BLOCK, cols]
            zero = jnp.zeros_like(kd)
            k_half = (jnp.where(low_head, kd, zero), jnp.where(low_head, zero, kd))
            v_half = (jnp.where(low_head, vd, zero), jnp.where(low_head, zero, vd))
            for pp in range(2):
                pair = hk * 2 + pp
                qp = q_ref[rows, pair * LANES:(pair + 1) * LANES]
                acc = None
                for half in range(2):
                    sink = sink_ref[pair * 2 + half]
                    s2 = lax.dot_general(k_half[half], qp, nt, preferred_element_type=_F32)
                    s_prev = s2[:BLOCK]
                    if b == 0:
                        s_prev = jnp.where(has_prev, s_prev, -jnp.inf)
                    s = jnp.where(from_prev, s_prev, s2[BLOCK:])
                    m = jnp.maximum(jnp.max(s, axis=0, keepdims=True), sink)
                    p = jnp.exp(s - m)
                    den = jnp.sum(p, axis=0, keepdims=True) + jnp.exp(sink - m)
                    pb = p.astype(_BF16)
                    pz = jnp.zeros_like(pb)
                    p2 = jnp.concatenate([jnp.where(from_prev, pb, pz), jnp.where(from_prev, pz, pb)], axis=0)
                    o = lax.dot_general(v_half[half], p2, tn, preferred_element_type=_F32) * (1.0 / den)
                    acc = o if acc is None else acc + o
                o_ref[rows, pair * LANES:(pair + 1) * LANES] = jnp.transpose(acc).astype(_BF16)


def _attn(q, k2, v2, sinks, to_cast=(), *, tq=512):
    L = q.shape[0]
    per = tq // BLOCK
    row = lambda i: (i, 0)
    prev = lambda i: (jnp.maximum(i * per - 1, 0), 0)
    cast_specs, cast_shapes = _row_cast_specs(to_cast, L // tq)
    out = pl.pallas_call(
        functools.partial(_attn_body, len(to_cast)),
        grid=(L // tq,),
        in_specs=[
            pl.BlockSpec(memory_space=pltpu.SMEM),
            pl.BlockSpec((tq, ATTN_WIDTH), row),
            pl.BlockSpec((tq, 2 * KV_WIDTH), row),
            pl.BlockSpec((tq, 2 * KV_WIDTH), row),
            pl.BlockSpec((BLOCK, 2 * KV_WIDTH), prev),
            pl.BlockSpec((BLOCK, 2 * KV_WIDTH), prev),
        ] + cast_specs,
        out_specs=[pl.BlockSpec((tq, ATTN_WIDTH), row)] + cast_specs,
        out_shape=[jax.ShapeDtypeStruct((L, ATTN_WIDTH), _BF16)] + cast_shapes,
        compiler_params=pltpu.CompilerParams(
            dimension_semantics=("arbitrary",), vmem_limit_bytes=VMEM_LIMIT),
        name="attn",
    )(sinks, q, k2, v2, k2, v2, *to_cast)
    return out[0], out[1:]


def _ssm_body(u_ref, wst_ref, kw_ref, cp_ref, wg_ref, bg_ref, a_ref, aseg_ref, o_ref,
              lhs_ref, toep_ref, s_ref, xb_ref):
    T = SSM_CHUNK
    n_c = u_ref.shape[1]
    seg = n_c // SSM_SEGMENTS
    pitch = s_ref.shape[1] // SSM_SEGMENTS
    n_state_slabs = SLAB_STATES // LANES
    pair_w = 2 * LANES
    out_w = 2 * MXU_DIM

    for t in range(T):
        lhs_ref[:, t * LANES:(t + 1) * LANES] = u_ref[t]

    toep_ref[...] = jnp.zeros(toep_ref.shape, _BF16)
    for t in range(T):
        for tp in range(t, T):
            toep_ref[t * LANES:(t + 1) * LANES, tp * LANES:(tp + 1) * LANES] = (
                kw_ref[:, (tp - t) * LANES:(tp - t + 1) * LANES])

    per_dot = out_w // LANES
    for nb in range(2 * n_state_slabs // per_dot):
        res = jnp.dot(lhs_ref[...], wst_ref[:, nb * out_w:(nb + 1) * out_w], preferred_element_type=_F32)
        for part in range(per_dot):
            for j in range(SSM_SEGMENTS):
                s_ref[per_dot * nb + part, j * pitch:j * pitch + seg, :] = (
                    res[j * seg:(j + 1) * seg, part * LANES:(part + 1) * LANES])

    shape = (SSM_SEGMENTS, LANES)
    a_re = [jnp.broadcast_to(a_ref[0:1, k * LANES:(k + 1) * LANES], shape) for k in range(n_state_slabs)]
    a_im = [jnp.broadcast_to(a_ref[1:2, k * LANES:(k + 1) * LANES], shape) for k in range(n_state_slabs)]
    g_re = [jnp.broadcast_to(aseg_ref[0:1, k * LANES:(k + 1) * LANES], shape) for k in range(n_state_slabs)]
    g_im = [jnp.broadcast_to(aseg_ref[1:2, k * LANES:(k + 1) * LANES], shape) for k in range(n_state_slabs)]

    def seg_rows(w):
        return pl.ds(w, SSM_SEGMENTS, stride=pitch)

    def advance(w, carry, store):
        out = []
        for k in range(n_state_slabs):
            z_re, z_im = carry[2 * k], carry[2 * k + 1]
            s_re = s_ref[k, seg_rows(w), :]
            s_im = s_ref[n_state_slabs + k, seg_rows(w), :]
            if store:
                s_ref[k, seg_rows(w), :] = z_re
                s_ref[n_state_slabs + k, seg_rows(w), :] = z_im
            out.append(a_re[k] * z_re - a_im[k] * z_im + s_re)
            out.append(a_re[k] * z_im + a_im[k] * z_re + s_im)
        return tuple(out)

    zero = jnp.zeros(shape, _F32)
    ends = lax.fori_loop(0, seg, lambda w, c: advance(w, c, False), (zero,) * (2 * n_state_slabs),
                         unroll=SCAN_UNROLL)

    segidx = lax.broadcasted_iota(jnp.int32, shape, 0)
    init = []
    for k in range(n_state_slabs):
        i_re, i_im = zero, zero
        f_re, f_im = ends[2 * k], ends[2 * k + 1]
        for j in range(SSM_SEGMENTS - 1):
            c_re = g_re[k] * i_re - g_im[k] * i_im + f_re
            c_im = g_re[k] * i_im + g_im[k] * i_re + f_im
            i_re = jnp.where(segidx == j + 1, pltpu.roll(c_re, 1, 0), i_re)
            i_im = jnp.where(segidx == j + 1, pltpu.roll(c_im, 1, 0), i_im)
        init += [i_re, i_im]

    lax.fori_loop(0, seg, lambda w, c: advance(w, c, True), tuple(init), unroll=SCAN_UNROLL)

    for col in range(2 * n_state_slabs):
        for j in range(SSM_SEGMENTS):
            xb_ref[j * seg:(j + 1) * seg, col * LANES:(col + 1) * LANES] = (
                s_ref[col, j * pitch:j * pitch + seg, :].astype(_BF16))

    steps = out_w // LANES
    for i in range(T // steps):
        kk = (i + 1) * out_w
        cols = slice(i * out_w, (i + 1) * out_w)
        y = (jnp.dot(lhs_ref[:, :kk], toep_ref[:kk, cols], preferred_element_type=_F32)
             + jnp.dot(xb_ref[...], cp_ref[:, cols], preferred_element_type=_F32))
        z = jax.nn.gelu(y, approximate=True).astype(_BF16)
        for pp in range(steps // 2):
            gt = jnp.dot(z[:, pp * pair_w:(pp + 1) * pair_w], wg_ref[...],
                         preferred_element_type=_F32) + bg_ref[...]
            out = gt[:, :pair_w] * jax.nn.sigmoid(gt[:, pair_w:])
            t0 = i * steps + 2 * pp
            o_ref[t0] = out[:, :LANES].astype(_BF16)
            o_ref[t0 + 1] = out[:, LANES:].astype(_BF16)


def _complex_power(re, im, n):
    out_re, out_im = None, None
    while n:
        if n & 1:
            if out_re is None:
                out_re, out_im = re, im
            else:
                out_re, out_im = out_re * re - out_im * im, out_re * im + out_im * re
        n >>= 1
        if n:
            re, im = re * re - im * im, 2.0 * re * im
    return out_re, out_im


def _ssm_prep_body(seg, rows_ref, bt_ref, ct_ref, d_ref, wt_ref, bglu_ref,
                   wst_ref, kw_ref, cp_ref, wg_ref, bg_ref, ach_ref, aseg_ref):
    T, H = SSM_CHUNK, SSM_GROUP
    a_re, a_im = rows_ref[0:1, :], rows_ref[1:2, :]
    dt = jnp.exp(rows_ref[2:3, :])
    mag = jnp.exp(a_re * dt)
    ab_re = mag * jnp.cos(a_im * dt)
    ab_im = mag * jnp.sin(a_im * dt)
    inv_den = 1.0 / (a_re * a_re + a_im * a_im)
    nr = ab_re - 1.0
    coef_re = (nr * a_re + ab_im * a_im) * inv_den
    coef_im = (ab_im * a_re - nr * a_im) * inv_den

    def block_diag(t16, shape):
        row_g = lax.broadcasted_iota(jnp.int32, shape, 0) // H
        col_g = lax.broadcasted_iota(jnp.int32, shape, 1) // (shape[1] // SLAB_GROUPS)
        return jnp.where(row_g == col_g, jnp.concatenate([t16] * SLAB_GROUPS, axis=0), 0.0)

    cs = (LANES, SLAB_STATES)
    b_re, b_im = block_diag(bt_ref[0], cs), block_diag(bt_ref[1], cs)
    bb_re = coef_re * b_re - coef_im * b_im
    bb_im = coef_re * b_im + coef_im * b_re
    ct_re, ct_im = block_diag(ct_ref[0], cs), block_diag(ct_ref[1], cs)

    pw = [(jnp.ones_like(ab_re), jnp.zeros_like(ab_im))]
    for _ in range(T):
        pr, pi = pw[-1]
        pw.append((pr * ab_re - pi * ab_im, pr * ab_im + pi * ab_re))

    for t in range(T):
        pr, pi = pw[T - 1 - t]
        wst_ref[t * LANES:(t + 1) * LANES, :SLAB_STATES] = (bb_re * pr - bb_im * pi).astype(_BF16)
        wst_ref[t * LANES:(t + 1) * LANES, SLAB_STATES:] = (bb_re * pi + bb_im * pr).astype(_BF16)

    lags = []
    for k in range(T + 1):
        if k < T:
            lags.append(jnp.concatenate([ct_re, ct_im], axis=1))
        if k > 0:
            cp_ref[:SLAB_STATES, (k - 1) * LANES:k * LANES] = jnp.transpose(ct_re).astype(_BF16)
            cp_ref[SLAB_STATES:, (k - 1) * LANES:k * LANES] = jnp.transpose(-ct_im).astype(_BF16)
        ct_re, ct_im = ct_re * ab_re - ct_im * ab_im, ct_re * ab_im + ct_im * ab_re

    def split(a):
        hi = a.astype(_BF16)
        return hi, (a - hi.astype(_F32)).astype(_BF16)

    nt = (((1,), (1,)), ((), ()))
    a_hi, a_lo = split(jnp.concatenate(lags, axis=0))
    b_hi, b_lo = split(jnp.concatenate([bb_re, -bb_im], axis=1))
    kern_t = (lax.dot_general(a_hi, b_hi, nt, preferred_element_type=_F32)
              + lax.dot_general(a_hi, b_lo, nt, preferred_element_type=_F32)
              + lax.dot_general(a_lo, b_hi, nt, preferred_element_type=_F32))
    kern = jnp.transpose(kern_t)
    eye = (lax.broadcasted_iota(jnp.int32, (LANES, LANES), 0)
           == lax.broadcasted_iota(jnp.int32, (LANES, LANES), 1))
    kw_ref[:, :LANES] = (kern[:, :LANES] + jnp.where(eye, d_ref[...], 0.0)).astype(_BF16)
    kw_ref[:, LANES:] = kern[:, LANES:].astype(_BF16)

    cc = (LANES, LANES)
    w_lin = jnp.transpose(block_diag(wt_ref[0], cc)).astype(_BF16)
    w_gate = jnp.transpose(block_diag(wt_ref[1], cc)).astype(_BF16)
    wg_ref[...] = jnp.zeros(wg_ref.shape, _BF16)
    for t in range(2):
        wg_ref[t * LANES:(t + 1) * LANES, t * LANES:(t + 1) * LANES] = w_lin
        wg_ref[t * LANES:(t + 1) * LANES, (2 + t) * LANES:(3 + t) * LANES] = w_gate
    bg_ref[...] = jnp.concatenate([bglu_ref[0:1, :], bglu_ref[0:1, :], bglu_ref[1:2, :], bglu_ref[1:2, :]],
                                  axis=1)

    ach_ref[0:1, :], ach_ref[1:2, :] = pw[T]
    aseg_ref[0:1, :], aseg_ref[1:2, :] = _complex_power(pw[T][0], pw[T][1], seg)


def _ssm_weights(log_dt, a_re, a_im, b_re, b_im, c_re, c_im, d_skip, w_glu, b_glu, seg):
    P, H, T = SSM_STATE, SSM_GROUP, SSM_CHUNK
    SG, NS = SLAB_GROUPS, SSM_GROUPS // SLAB_GROUPS
    rows = jnp.stack([a_re.reshape(NS, SG * P), a_im.reshape(NS, SG * P),
                      jnp.repeat(log_dt, P).reshape(NS, SG * P)], axis=1)

    def per_group_rows(w, lead):
        t = w.reshape((NS, SG) + w.shape[1:])
        t = jnp.moveaxis(t, 2 + lead, 1)
        return t.reshape(NS, t.shape[1], -1)

    bt = jnp.stack([per_group_rows(b_re, 1), per_group_rows(b_im, 1)], axis=1)
    ct = jnp.stack([per_group_rows(c_re, 0), per_group_rows(c_im, 0)], axis=1)
    wt = jnp.stack([per_group_rows(w_glu[..., :H], 1), per_group_rows(w_glu[..., H:], 1)], axis=1)
    bglu = jnp.stack([b_glu[:, :H].reshape(NS, SG * H), b_glu[:, H:].reshape(NS, SG * H)], axis=1)
    d = d_skip.reshape(NS, 1, SG * H)

    blk3 = lambda i: (i, 0, 0)
    blk4 = lambda i: (i, 0, 0, 0)
    return pl.pallas_call(
        functools.partial(_ssm_prep_body, seg),
        grid=(NS,),
        in_specs=[
            pl.BlockSpec((None, 3, SG * P), blk3),
            pl.BlockSpec((None, 2, H, SG * P), blk4),
            pl.BlockSpec((None, 2, H, SG * P), blk4),
            pl.BlockSpec((None, 1, SG * H), blk3),
            pl.BlockSpec((None, 2, H, SG * H), blk4),
            pl.BlockSpec((None, 2, SG * H), blk3),
        ],
        out_specs=[
            pl.BlockSpec((None, T * LANES, 2 * SLAB_STATES), blk3),
            pl.BlockSpec((None, LANES, T * LANES), blk3),
            pl.BlockSpec((None, 2 * SLAB_STATES, T * LANES), blk3),
            pl.BlockSpec((None, 2 * LANES, 4 * LANES), blk3),
            pl.BlockSpec((None, 1, 4 * LANES), blk3),
            pl.BlockSpec((None, 2, SLAB_STATES), blk3),
            pl.BlockSpec((None, 2, SLAB_STATES), blk3),
        ],
        out_shape=[
            jax.ShapeDtypeStruct((NS, T * LANES, 2 * SLAB_STATES), _BF16),
            jax.ShapeDtypeStruct((NS, LANES, T * LANES), _BF16),
            jax.ShapeDtypeStruct((NS, 2 * SLAB_STATES, T * LANES), _BF16),
            jax.ShapeDtypeStruct((NS, 2 * LANES, 4 * LANES), _BF16),
            jax.ShapeDtypeStruct((NS, 1, 4 * LANES), _F32),
            jax.ShapeDtypeStruct((NS, 2, SLAB_STATES), _F32),
            jax.ShapeDtypeStruct((NS, 2, SLAB_STATES), _F32),
        ],
        compiler_params=pltpu.CompilerParams(
            dimension_semantics=("arbitrary",), vmem_limit_bytes=VMEM_LIMIT),
        name="ssm_prep",
    )(rows, bt, ct, d, wt, bglu)


def _ssm(u_t, weights):
    T, n_c, W = u_t.shape
    seg = n_c // SSM_SEGMENTS
    pitch = seg + SUBLANES
    ns = W // LANES
    wst, kw, cp, wg, bg, a_chunk, a_seg = weights
    slab = lambda i: (0, 0, i)
    blk = lambda i: (i, 0, 0)
    return pl.pallas_call(
        _ssm_body,
        grid=(ns,),
        in_specs=[
            pl.BlockSpec((T, n_c, LANES), slab),
            pl.BlockSpec((None, T * LANES, 2 * SLAB_STATES), blk),
            pl.BlockSpec((None, LANES, T * LANES), blk),
            pl.BlockSpec((None, 2 * SLAB_STATES, T * LANES), blk),
            pl.BlockSpec((None, 2 * LANES, 4 * LANES), blk),
            pl.BlockSpec((None, 1, 4 * LANES), blk),
            pl.BlockSpec((None, 2, SLAB_STATES), blk),
            pl.BlockSpec((None, 2, SLAB_STATES), blk),
        ],
        out_specs=pl.BlockSpec((T, n_c, LANES), slab),
        out_shape=jax.ShapeDtypeStruct((T, n_c, W), _BF16),
        scratch_shapes=[
            pltpu.VMEM((n_c, T * LANES), _BF16),
            pltpu.VMEM((T * LANES, T * LANES), _BF16),
            pltpu.VMEM((2 * SLAB_STATES // LANES, SSM_SEGMENTS * pitch, LANES), _F32),
            pltpu.VMEM((n_c, 2 * SLAB_STATES), _BF16),
        ],
        compiler_params=pltpu.CompilerParams(
            dimension_semantics=("arbitrary",), vmem_limit_bytes=VMEM_LIMIT),
        name="ssm",
    )(u_t, wst, kw, cp, wg, bg, a_chunk, a_seg)


def _outproj_body(n_cast, a_ref, s_ref, x_ref, ag_ref, sg_ref, w_ref, *refs):
    o_ref, sn_ref = refs[n_cast], refs[-1]
    for src, dst in zip(refs[:n_cast], refs[n_cast + 1:-1]):
        dst[...] = src[...].astype(_BF16)
    tm = x_ref.shape[0]
    an = _rms(a_ref[...].astype(_F32), ag_ref[...]).astype(_BF16)
    for t in range(SSM_CHUNK):
        sn_t = _rms(s_ref[t].astype(_F32), sg_ref[...])
        for s in range(SSM_WIDTH // LANES):
            sn_ref[s, pl.ds(t, tm // SSM_CHUNK, stride=SSM_CHUNK), :] = sn_t[:, s * LANES:(s + 1) * LANES]
    sn = jnp.concatenate([sn_ref[s] for s in range(SSM_WIDTH // LANES)], axis=1).astype(_BF16)
    mixed = jnp.concatenate([an, sn], axis=1)
    o_ref[...] = x_ref[...] + jnp.dot(mixed, w_ref[...], preferred_element_type=_F32)


def _outproj(attn, ssm_t, x, a_gain, s_gain, w_out, to_cast=(), *, tm=512):
    L, D = x.shape
    row = lambda i: (i, 0)
    fixed = lambda i: (0, 0)
    cast_specs, cast_shapes = _row_cast_specs(to_cast, L // tm)
    out = pl.pallas_call(
        functools.partial(_outproj_body, len(to_cast)),
        grid=(L // tm,),
        in_specs=[
            pl.BlockSpec((tm, ATTN_WIDTH), row),
            pl.BlockSpec((SSM_CHUNK, tm // SSM_CHUNK, SSM_WIDTH), lambda i: (0, i, 0)),
            pl.BlockSpec((tm, D), row),
            pl.BlockSpec((1, ATTN_WIDTH), fixed),
            pl.BlockSpec((1, SSM_WIDTH), fixed),
            pl.BlockSpec((ATTN_WIDTH + SSM_WIDTH, D), fixed),
        ] + cast_specs,
        out_specs=[pl.BlockSpec((tm, D), row)] + cast_specs,
        out_shape=[jax.ShapeDtypeStruct((L, D), _F32)] + cast_shapes,
        scratch_shapes=[pltpu.VMEM((SSM_WIDTH // LANES, tm, LANES), _F32)],
        compiler_params=pltpu.CompilerParams(
            dimension_semantics=("arbitrary",), vmem_limit_bytes=VMEM_LIMIT),
        name="outproj",
    )(attn, ssm_t, x, a_gain.reshape(1, -1), s_gain.reshape(1, -1), w_out, *to_cast)
    return out[0], out[1:]


def _layer(x, pos, invf, p):
    L = x.shape[0]
    seg = L // (SSM_SEGMENTS * SSM_CHUNK)
    act, (w_down1, w_in, w_out) = _ffn_up(
        x, p['ffn1_norm'], p['ffn1_w_gate'].astype(_BF16), p['ffn1_w_up'].astype(_BF16),
        [p['ffn1_w_down'], p['w_in'], p['w_out']])
    x = _ffn_down(act, w_down1, x)
    q, k2, v2, u_t = _proj(x, p['mix_norm'], w_in, pos, invf, p['q_norm'], p['k_norm'])
    attn, (w_gate2, w_up2) = _attn(q, k2, v2, p['attn_sinks'], [p['ffn2_w_gate'], p['ffn2_w_up']])
    weights = _ssm_weights(p['ssm_log_dt'], p['ssm_a_re'], p['ssm_a_im'], p['ssm_b_re'], p['ssm_b_im'],
                           p['ssm_c_re'], p['ssm_c_im'], p['ssm_d'], p['ssm_w_glu'], p['ssm_b_glu'], seg)
    ssm_t = _ssm(u_t, weights)
    x, (w_down2,) = _outproj(attn, ssm_t, x, p['attn_out_norm'], p['ssm_out_norm'], w_out,
                             [p['ffn2_w_down']])
    act, _ = _ffn_up(x, p['ffn2_norm'], w_gate2, w_up2)
    return _ffn_down(act, w_down2, x)


def kernel(x, positions, ffn1_norm, ffn1_w_gate, ffn1_w_up, ffn1_w_down, mix_norm, w_in, q_norm, k_norm,
           attn_sinks, ssm_log_dt, ssm_a_re, ssm_a_im, ssm_b_re, ssm_b_im, ssm_c_re, ssm_c_im, ssm_d,
           ssm_w_glu, ssm_b_glu, attn_out_norm, ssm_out_norm, w_out, ffn2_norm, ffn2_w_gate, ffn2_w_up,
           ffn2_w_down):
    params = dict(
        ffn1_norm=ffn1_norm, ffn1_w_gate=ffn1_w_gate, ffn1_w_up=ffn1_w_up, ffn1_w_down=ffn1_w_down,
        mix_norm=mix_norm, w_in=w_in, q_norm=q_norm, k_norm=k_norm, attn_sinks=attn_sinks,
        ssm_log_dt=ssm_log_dt, ssm_a_re=ssm_a_re, ssm_a_im=ssm_a_im, ssm_b_re=ssm_b_re, ssm_b_im=ssm_b_im,
        ssm_c_re=ssm_c_re, ssm_c_im=ssm_c_im, ssm_d=ssm_d, ssm_w_glu=ssm_w_glu, ssm_b_glu=ssm_b_glu,
        attn_out_norm=attn_out_norm, ssm_out_norm=ssm_out_norm, w_out=w_out,
        ffn2_norm=ffn2_norm, ffn2_w_gate=ffn2_w_gate, ffn2_w_up=ffn2_w_up, ffn2_w_down=ffn2_w_down)
    depth = ffn1_norm.shape[0]
    half = HEAD_DIM // 2
    inv_freq = ROPE_THETA ** (-jnp.arange(half, dtype=_F32) * 2.0 / HEAD_DIM)
    invf = jnp.tile(inv_freq, LANES // half).reshape(1, LANES)
    outs = []
    for b in range(x.shape[0]):
        xb = x[b]
        for i in range(depth):
            xb = _layer(xb, positions[b], invf, {name: val[i] for name, val in params.items()})
        outs.append(xb)
    return jnp.stack(outs, axis=0)
```

```python
import functools
import math

import jax
import jax.numpy as jnp
from jax import lax
from jax.experimental import pallas as pl
from jax.experimental.pallas import tpu as pltpu

HEAD_DIM = 64
N_Q_HEADS = 16
N_KV_HEADS = 4
ATTN_WIDTH = N_Q_HEADS * HEAD_DIM
KV_WIDTH = N_KV_HEADS * HEAD_DIM
BLOCK = 128
ROPE_THETA = 10000.0
SSM_GROUP = 16
SSM_GROUPS = 64
SSM_STATE = 64
SSM_WIDTH = SSM_GROUP * SSM_GROUPS
FFN_RESIDUAL = 0.5
EPS = 1e-6

LANES = 128
SUBLANES = 8
MXU_DIM = 256
SSM_CHUNK = 8
SSM_SEGMENTS = SUBLANES
SLAB_GROUPS = LANES // SSM_GROUP
SLAB_STATES = SLAB_GROUPS * SSM_STATE
SCAN_UNROLL = 4
VMEM_LIMIT = 56 * 1024 * 1024

_BF16 = jnp.bfloat16
_F32 = jnp.float32


def _rms(x, gain):
    ms = jnp.mean(x * x, axis=-1, keepdims=True)
    return x * lax.rsqrt(ms + EPS) * gain


def _ffn_up_body(cast_sliced, x_ref, gain_ref, wg_ref, wu_ref, *refs):
    n = len(cast_sliced)
    src_refs, a_ref, dst_refs, h_ref = refs[:n], refs[n], refs[n + 1:2 * n + 1], refs[2 * n + 1]
    j = pl.program_id(1)
    first = j == 0

    @pl.when(first)
    def _():
        h_ref[...] = _rms(x_ref[...], gain_ref[...]).astype(_BF16)

    h = h_ref[...]
    g = jnp.dot(h, wg_ref[...], preferred_element_type=_F32)
    u = jnp.dot(h, wu_ref[...], preferred_element_type=_F32)
    a_ref[...] = (g * jax.nn.sigmoid(g) * u).astype(_BF16)

    for per, src, dst in zip(cast_sliced, src_refs, dst_refs):
        if per:
            rows = pl.ds(pl.multiple_of(j * per, per), per)
            dst[rows, :] = src[rows, :].astype(_BF16)
        else:
            @pl.when(first)
            def _(src=src, dst=dst):
                dst[...] = src[...].astype(_BF16)


def _ffn_down_body(a_ref, wd_ref, x_ref, o_ref):
    d = jnp.dot(a_ref[...], wd_ref[...], preferred_element_type=_F32)
    o_ref[...] = x_ref[...] + FFN_RESIDUAL * d


def _cast_spec(shape, ni, nj):
    R, C = shape
    pack = 2 * SUBLANES
    assert R % ni == 0 and (R // ni) % pack == 0, shape
    per = R // ni // nj if (R // ni) % (nj * pack) == 0 else 0
    return pl.BlockSpec((R // ni, C), lambda i, j: (i, 0)), per


def _ffn_up(x, gain, wg, wu, to_cast=(), *, tm=1024, tf=512):
    L, D = x.shape
    F = wg.shape[1]
    tm = min(tm, L)
    ni, nj = L // tm, F // tf
    cast_specs = [_cast_spec(w.shape, ni, nj) for w in to_cast]
    out = pl.pallas_call(
        functools.partial(_ffn_up_body, tuple(every for _, every in cast_specs)),
        grid=(ni, nj),
        in_specs=[
            pl.BlockSpec((tm, D), lambda i, j: (i, 0)),
            pl.BlockSpec((1, D), lambda i, j: (0, 0)),
            pl.BlockSpec((D, tf), lambda i, j: (0, j)),
            pl.BlockSpec((D, tf), lambda i, j: (0, j)),
        ] + [spec for spec, _ in cast_specs],
        out_specs=[pl.BlockSpec((tm, tf), lambda i, j: (i, j))] + [spec for spec, _ in cast_specs],
        out_shape=[jax.ShapeDtypeStruct((L, F), _BF16)]
        + [jax.ShapeDtypeStruct(w.shape, _BF16) for w in to_cast],
        scratch_shapes=[pltpu.VMEM((tm, D), _BF16)],
        compiler_params=pltpu.CompilerParams(
            dimension_semantics=("arbitrary", "arbitrary"), vmem_limit_bytes=VMEM_LIMIT),
        name="ffn_up",
    )(x, gain.reshape(1, D), wg, wu, *to_cast)
    return out[0], out[1:]


def _ffn_down(act, wd, x, *, tm=512):
    L, D = x.shape
    F = act.shape[1]
    tm = min(tm, L)
    return pl.pallas_call(
        _ffn_down_body,
        grid=(L // tm,),
        in_specs=[
            pl.BlockSpec((tm, F), lambda i: (i, 0)),
            pl.BlockSpec((F, D), lambda i: (0, 0), pipeline_mode=pl.Buffered(1)),
            pl.BlockSpec((tm, D), lambda i: (i, 0)),
        ],
        out_specs=pl.BlockSpec((tm, D), lambda i: (i, 0)),
        out_shape=jax.ShapeDtypeStruct((L, D), _F32),
        compiler_params=pltpu.CompilerParams(
            dimension_semantics=("arbitrary",), vmem_limit_bytes=VMEM_LIMIT),
        name="ffn_down",
    )(act, wd, x)


def _proj_body(x_ref, gain_ref, w_ref, pos_ref, invf_ref, qg_ref, kg_ref, pn_ref,
               q_ref, k_ref, v_ref, u_ref, us_ref):
    tm = x_ref.shape[0]
    h = _rms(x_ref[...], gain_ref[...]).astype(_BF16)
    group = 2 * MXU_DIM

    def project(g):
        return jnp.dot(h, w_ref[:, g * group:(g + 1) * group], preferred_element_type=_F32)

    half = HEAD_DIM // 2
    ang = pos_ref[...] * invf_ref[...]
    lane_q = lax.broadcasted_iota(jnp.int32, ang.shape, 1) // half

    def spread(table):
        parts = []
        for qtr in range(LANES // half):
            m = jnp.where(lane_q == qtr, table, 0.0)
            parts.append(m + pltpu.roll(m, half, 1) + pltpu.roll(m, 2 * half, 1) + pltpu.roll(m, 3 * half, 1))
        return jnp.concatenate(parts, axis=0)

    cos = spread(jnp.cos(ang))
    sin = spread(jnp.sin(ang))
    lane = lax.broadcasted_iota(jnp.int32, (tm, LANES), 1)
    first_half = (lane & (HEAD_DIM // 2)) == 0
    low_head = lane < HEAD_DIM
    sin_signed = jnp.where(first_half, -sin, sin)

    def norm_rotary(x4, gain):
        ms4 = jnp.dot((x4 * x4).astype(_BF16), pn_ref[...], preferred_element_type=_F32)
        out = []
        for part in range(2):
            lanes = slice(part * LANES, (part + 1) * LANES)
            y = x4[:, lanes] * lax.rsqrt(ms4[:, lanes] + EPS) * gain
            swapped = jnp.where(first_half, pltpu.roll(y, LANES - HEAD_DIM // 2, 1),
                                pltpu.roll(y, HEAD_DIM // 2, 1))
            out.append(y * cos + swapped * sin_signed)
        return out

    def dup_heads(xc):
        r = pltpu.roll(xc, HEAD_DIM, 1)
        return jnp.where(low_head, xc, r), jnp.where(low_head, r, xc)

    scale = 1.0 / math.sqrt(HEAD_DIM)
    assert KV_WIDTH == MXU_DIM and 2 * KV_WIDTH == group
    for g in range(ATTN_WIDTH // group):
        pg = project(g)
        for c in range(group // MXU_DIM):
            for part, qc in enumerate(norm_rotary(pg[:, c * MXU_DIM:(c + 1) * MXU_DIM], qg_ref[...])):
                at = g * group + c * MXU_DIM + part * LANES
                q_ref[:, at:at + LANES] = (qc * scale).astype(_BF16)
    pg = project(ATTN_WIDTH // group)
    for part, kc in enumerate(norm_rotary(pg[:, :KV_WIDTH], kg_ref[...])):
        ka, kb = dup_heads(kc)
        k_ref[:, 2 * part * LANES:(2 * part + 1) * LANES] = ka.astype(_BF16)
        k_ref[:, (2 * part + 1) * LANES:(2 * part + 2) * LANES] = kb.astype(_BF16)
    for part in range(KV_WIDTH // LANES):
        va, vb = dup_heads(pg[:, KV_WIDTH + part * LANES:KV_WIDTH + (part + 1) * LANES])
        v_ref[:, 2 * part * LANES:(2 * part + 1) * LANES] = va.astype(_BF16)
        v_ref[:, (2 * part + 1) * LANES:(2 * part + 2) * LANES] = vb.astype(_BF16)
    first_u = (ATTN_WIDTH + 2 * KV_WIDTH) // group
    per_group = group // LANES
    for g in range(SSM_WIDTH // group):
        pg = project(first_u + g)
        for s in range(per_group):
            us_ref[g * per_group + s] = pg[:, s * LANES:(s + 1) * LANES]
        for t in range(SSM_CHUNK):
            for s in range(g * per_group, (g + 1) * per_group):
                rows = us_ref[s, pl.ds(t, tm // SSM_CHUNK, stride=SSM_CHUNK), :]
                u_ref[t, :, s * LANES:(s + 1) * LANES] = rows.astype(_BF16)


def _proj(x, gain, w_in, pos, invf, q_gain, k_gain, *, tm=512):
    L, D = x.shape
    C = w_in.shape[1]
    quarters = LANES // (HEAD_DIM // 2)
    pos_f = pos.astype(_F32).reshape(L // tm, quarters, tm // quarters).transpose(0, 2, 1)
    pos_f = jnp.repeat(pos_f, HEAD_DIM // 2, axis=2)
    head_of_lane = jnp.arange(MXU_DIM) // HEAD_DIM
    pn = jnp.where(head_of_lane[:, None] == head_of_lane[None, :], 1.0 / HEAD_DIM, 0.0).astype(_BF16)
    qg = jnp.tile(q_gain, LANES // HEAD_DIM).reshape(1, LANES)
    kg = jnp.tile(k_gain, LANES // HEAD_DIM).reshape(1, LANES)
    row = lambda i: (i, 0)
    fixed = lambda i: (0, 0)
    return pl.pallas_call(
        _proj_body,
        grid=(L // tm,),
        in_specs=[
            pl.BlockSpec((tm, D), row),
            pl.BlockSpec((1, D), fixed),
            pl.BlockSpec((D, C), fixed),
            pl.BlockSpec((None, tm // quarters, LANES), lambda i: (i, 0, 0)),
            pl.BlockSpec((1, LANES), fixed),
            pl.BlockSpec((1, LANES), fixed),
            pl.BlockSpec((1, LANES), fixed),
            pl.BlockSpec((MXU_DIM, MXU_DIM), fixed),
        ],
        out_specs=[
            pl.BlockSpec((tm, ATTN_WIDTH), row),
            pl.BlockSpec((tm, 2 * KV_WIDTH), row),
            pl.BlockSpec((tm, 2 * KV_WIDTH), row),
            pl.BlockSpec((SSM_CHUNK, tm // SSM_CHUNK, SSM_WIDTH), lambda i: (0, i, 0)),
        ],
        out_shape=[
            jax.ShapeDtypeStruct((L, ATTN_WIDTH), _BF16),
            jax.ShapeDtypeStruct((L, 2 * KV_WIDTH), _BF16),
            jax.ShapeDtypeStruct((L, 2 * KV_WIDTH), _BF16),
            jax.ShapeDtypeStruct((SSM_CHUNK, L // SSM_CHUNK, SSM_WIDTH), _BF16),
        ],
        scratch_shapes=[pltpu.VMEM((SSM_WIDTH // LANES, tm, LANES), _F32)],
        compiler_params=pltpu.CompilerParams(
            dimension_semantics=("arbitrary",), vmem_limit_bytes=VMEM_LIMIT),
        name="proj",
    )(x, gain.reshape(1, D), w_in, pos_f, invf, qg, kg, pn)


def _row_cast_specs(arrays, n):
    pack = 2 * SUBLANES
    for w in arrays:
        assert w.shape[0] % (n * pack) == 0, w.shape
    specs = [pl.BlockSpec((w.shape[0] // n, w.shape[1]), lambda i: (i, 0)) for w in arrays]
    return specs, [jax.ShapeDtypeStruct(w.shape, _BF16) for w in arrays]


def _attn_body(n_cast, sink_ref, q_ref, kc_ref, vc_ref, kp_ref, vp_ref, *refs):
    o_ref = refs[n_cast]
    for src, dst in zip(refs[:n_cast], refs[n_cast + 1:]):
        dst[...] = src[...].astype(_BF16)
    tq = q_ref.shape[0]
    kj = lax.broadcasted_iota(jnp.int32, (BLOCK, BLOCK), 0)
    qi = lax.broadcasted_iota(jnp.int32, (BLOCK, BLOCK), 1)
    from_prev = kj > qi
    has_prev = pl.program_id(0) > 0
    low_head = lax.broadcasted_iota(jnp.int32, (2 * BLOCK, LANES), 1) < HEAD_DIM
    nt = (((1,), (1,)), ((), ()))
    tn = (((0,), (0,)), ((), ()))

    for b in range(tq // BLOCK):
        rows = slice(b * BLOCK, (b + 1) * BLOCK)
        for hk in range(N_KV_HEADS):
            cols = slice(hk * LANES, (hk + 1) * LANES)
            if b == 0:
                kd = jnp.concatenate([kp_ref[:, cols], kc_ref[0:BLOCK, cols]], axis=0)
                vd = jnp.concatenate([vp_ref[:, cols], vc_ref[0:BLOCK, cols]], axis=0)
            else:
                kd = kc_ref[(b - 1) * BLOCK:(b + 1) * BLOCK, cols]
                vd = vc_ref[(b - 1) * BLOCK:(b + 1) * BLOCK, cols]
            zero = jnp.zeros_like(kd)
            k_half = (jnp.where(low_head, kd, zero), jnp.where(low_head, zero, kd))
            v_half = (jnp.where(low_head, vd, zero), jnp.where(low_head, zero, vd))
            for pp in range(2):
                pair = hk * 2 + pp
                qp = q_ref[rows, pair * LANES:(pair + 1) * LANES]
                acc = None
                for half in range(2):
                    sink = sink_ref[pair * 2 + half]
                    s2 = lax.dot_general(k_half[half], qp, nt, preferred_element_type=_F32)
                    s_prev = s2[:BLOCK]
                    if b == 0:
                        s_prev = jnp.where(has_prev, s_prev, -jnp.inf)
                    s = jnp.where(from_prev, s_prev, s2[BLOCK:])
                    m = jnp.maximum(jnp.max(s, axis=0, keepdims=True), sink)
                    p = jnp.exp(s - m)
                    den = jnp.sum(p, axis=0, keepdims=True) + jnp.exp(sink - m)
                    pb = p.astype(_BF16)
                    pz = jnp.zeros_like(pb)
                    p2 = jnp.concatenate([jnp.where(from_prev, pb, pz), jnp.where(from_prev, pz, pb)], axis=0)
                    o = lax.dot_general(v_half[half], p2, tn, preferred_element_type=_F32) * (1.0 / den)
                    acc = o if acc is None else acc + o
                o_ref[rows, pair * LANES:(pair + 1) * LANES] = jnp.transpose(acc).astype(_BF16)


def _attn(q, k2, v2, sinks, to_cast=(), *, tq=512):
    L = q.shape[0]
    per = tq // BLOCK
    row = lambda i: (i, 0)
    prev = lambda i: (jnp.maximum(i * per - 1, 0), 0)
    cast_specs, cast_shapes = _row_cast_specs(to_cast, L // tq)
    out = pl.pallas_call(
        functools.partial(_attn_body, len(to_cast)),
        grid=(L // tq,),
        in_specs=[
            pl.BlockSpec(memory_space=pltpu.SMEM),
            pl.BlockSpec((tq, ATTN_WIDTH), row),
            pl.BlockSpec((tq, 2 * KV_WIDTH), row),
            pl.BlockSpec((tq, 2 * KV_WIDTH), row),
            pl.BlockSpec((BLOCK, 2 * KV_WIDTH), prev),
            pl.BlockSpec((BLOCK, 2 * KV_WIDTH), prev),
        ] + cast_specs,
        out_specs=[pl.BlockSpec((tq, ATTN_WIDTH), row)] + cast_specs,
        out_shape=[jax.ShapeDtypeStruct((L, ATTN_WIDTH), _BF16)] + cast_shapes,
        compiler_params=pltpu.CompilerParams(
            dimension_semantics=("arbitrary",), vmem_limit_bytes=VMEM_LIMIT),
        name="attn",
    )(sinks, q, k2, v2, k2, v2, *to_cast)
    return out[0], out[1:]


def _ssm_body(u_ref, wst_ref, kw_ref, cp_ref, wg_ref, bg_ref, a_ref, aseg_ref, o_ref,
              lhs_ref, toep_ref, s_ref, xb_ref):
    T = SSM_CHUNK
    n_c = u_ref.shape[1]
    seg = n_c // SSM_SEGMENTS
    pitch = s_ref.shape[1] // SSM_SEGMENTS
    n_state_slabs = SLAB_STATES // LANES
    pair_w = 2 * LANES
    out_w = 2 * MXU_DIM

    for t in range(T):
        lhs_ref[:, t * LANES:(t + 1) * LANES] = u_ref[t]

    toep_ref[...] = jnp.zeros(toep_ref.shape, _BF16)
    for t in range(T):
        for tp in range(t, T):
            toep_ref[t * LANES:(t + 1) * LANES, tp * LANES:(tp + 1) * LANES] = (
                kw_ref[:, (tp - t) * LANES:(tp - t + 1) * LANES])

    per_dot = out_w // LANES
    for nb in range(2 * n_state_slabs // per_dot):
        res = jnp.dot(lhs_ref[...], wst_ref[:, nb * out_w:(nb + 1) * out_w], preferred_element_type=_F32)
        for part in range(per_dot):
            for j in range(SSM_SEGMENTS):
                s_ref[per_dot * nb + part, j * pitch:j * pitch + seg, :] = (
                    res[j * seg:(j + 1) * seg, part * LANES:(part + 1) * LANES])

    shape = (SSM_SEGMENTS, LANES)
    a_re = [jnp.broadcast_to(a_ref[0:1, k * LANES:(k + 1) * LANES], shape) for k in range(n_state_slabs)]
    a_im = [jnp.broadcast_to(a_ref[1:2, k * LANES:(k + 1) * LANES], shape) for k in range(n_state_slabs)]
    g_re = [jnp.broadcast_to(aseg_ref[0:1, k * LANES:(k + 1) * LANES], shape) for k in range(n_state_slabs)]
    g_im = [jnp.broadcast_to(aseg_ref[1:2, k * LANES:(k + 1) * LANES], shape) for k in range(n_state_slabs)]

    def seg_rows(w):
        return pl.ds(w, SSM_SEGMENTS, stride=pitch)

    def advance(w, carry, store):
        out = []
        for k in range(n_state_slabs):
            z_re, z_im = carry[2 * k], carry[2 * k + 1]
            s_re = s_ref[k, seg_rows(w), :]
            s_im = s_ref[n_state_slabs + k, seg_rows(w), :]
            if store:
                s_ref[k, seg_rows(w), :] = z_re
                s_ref[n_state_slabs + k, seg_rows(w), :] = z_im
            out.append(a_re[k] * z_re - a_im[k] * z_im + s_re)
            out.append(a_re[k] * z_im + a_im[k] * z_re + s_im)
        return tuple(out)

    zero = jnp.zeros(shape, _F32)
    ends = lax.fori_loop(0, seg, lambda w, c: advance(w, c, False), (zero,) * (2 * n_state_slabs),
                         unroll=SCAN_UNROLL)

    segidx = lax.broadcasted_iota(jnp.int32, shape, 0)
    init = []
    for k in range(n_state_slabs):
        i_re, i_im = zero, zero
        f_re, f_im = ends[2 * k], ends[2 * k + 1]
        for j in range(SSM_SEGMENTS - 1):
            c_re = g_re[k] * i_re - g_im[k] * i_im + f_re
            c_im = g_re[k] * i_im + g_im[k] * i_re + f_im
            i_re = jnp.where(segidx == j + 1, pltpu.roll(c_re, 1, 0), i_re)
            i_im = jnp.where(segidx == j + 1, pltpu.roll(c_im, 1, 0), i_im)
        init += [i_re, i_im]

    lax.fori_loop(0, seg, lambda w, c: advance(w, c, True), tuple(init), unroll=SCAN_UNROLL)

    for col in range(2 * n_state_slabs):
        for j in range(SSM_SEGMENTS):
            xb_ref[j * seg:(j + 1) * seg, col * LANES:(col + 1) * LANES] = (
                s_ref[col, j * pitch:j * pitch + seg, :].astype(_BF16))

    steps = out_w // LANES
    for i in range(T // steps):
        kk = (i + 1) * out_w
        cols = slice(i * out_w, (i + 1) * out_w)
        y = (jnp.dot(lhs_ref[:, :kk], toep_ref[:kk, cols], preferred_element_type=_F32)
             + jnp.dot(xb_ref[...], cp_ref[:, cols], preferred_element_type=_F32))
        z = jax.nn.gelu(y, approximate=True).astype(_BF16)
        for pp in range(steps // 2):
            gt = jnp.dot(z[:, pp * pair_w:(pp + 1) * pair_w], wg_ref[...],
                         preferred_element_type=_F32) + bg_ref[...]
            out = gt[:, :pair_w] * jax.nn.sigmoid(gt[:, pair_w:])
            t0 = i * steps + 2 * pp
            o_ref[t0] = out[:, :LANES].astype(_BF16)
            o_ref[t0 + 1] = out[:, LANES:].astype(_BF16)


def _complex_power(re, im, n):
    out_re, out_im = None, None
    while n:
        if n & 1:
            if out_re is None:
                out_re, out_im = re, im
            else:
                out_re, out_im = out_re * re - out_im * im, out_re * im + out_im * re
        n >>= 1
        if n:
            re, im = re * re - im * im, 2.0 * re * im
    return out_re, out_im


def _ssm_prep_body(seg, rows_ref, bt_ref, ct_ref, d_ref, wt_ref, bglu_ref,
                   wst_ref, kw_ref, cp_ref, wg_ref, bg_ref, ach_ref, aseg_ref):
    T, H = SSM_CHUNK, SSM_GROUP
    a_re, a_im = rows_ref[0:1, :], rows_ref[1:2, :]
    dt = jnp.exp(rows_ref[2:3, :])
    mag = jnp.exp(a_re * dt)
    ab_re = mag * jnp.cos(a_im * dt)
    ab_im = mag * jnp.sin(a_im * dt)
    inv_den = 1.0 / (a_re * a_re + a_im * a_im)
    nr = ab_re - 1.0
    coef_re = (nr * a_re + ab_im * a_im) * inv_den
    coef_im = (ab_im * a_re - nr * a_im) * inv_den

    def block_diag(t16, shape):
        row_g = lax.broadcasted_iota(jnp.int32, shape, 0) // H
        col_g = lax.broadcasted_iota(jnp.int32, shape, 1) // (shape[1] // SLAB_GROUPS)
        return jnp.where(row_g == col_g, jnp.concatenate([t16] * SLAB_GROUPS, axis=0), 0.0)

    cs = (LANES, SLAB_STATES)
    b_re, b_im = block_diag(bt_ref[0], cs), block_diag(bt_ref[1], cs)
    bb_re = coef_re * b_re - coef_im * b_im
    bb_im = coef_re * b_im + coef_im * b_re
    ct_re, ct_im = block_diag(ct_ref[0], cs), block_diag(ct_ref[1], cs)

    pw = [(jnp.ones_like(ab_re), jnp.zeros_like(ab_im))]
    for _ in range(T):
        pr, pi = pw[-1]
        pw.append((pr * ab_re - pi * ab_im, pr * ab_im + pi * ab_re))

    for t in range(T):
        pr, pi = pw[T - 1 - t]
        wst_ref[t * LANES:(t + 1) * LANES, :SLAB_STATES] = (bb_re * pr - bb_im * pi).astype(_BF16)
        wst_ref[t * LANES:(t + 1) * LANES, SLAB_STATES:] = (bb_re * pi + bb_im * pr).astype(_BF16)

    lags = []
    for k in range(T + 1):
        if k < T:
            lags.append(jnp.concatenate([ct_re, ct_im], axis=1))
        if k > 0:
            cp_ref[:SLAB_STATES, (k - 1) * LANES:k * LANES] = jnp.transpose(ct_re).astype(_BF16)
            cp_ref[SLAB_STATES:, (k - 1) * LANES:k * LANES] = jnp.transpose(-ct_im).astype(_BF16)
        ct_re, ct_im = ct_re * ab_re - ct_im * ab_im, ct_re * ab_im + ct_im * ab_re

    def split(a):
        hi = a.astype(_BF16)
        return hi, (a - hi.astype(_F32)).astype(_BF16)

    nt = (((1,), (1,)), ((), ()))
    a_hi, a_lo = split(jnp.concatenate(lags, axis=0))
    b_hi, b_lo = split(jnp.concatenate([bb_re, -bb_im], axis=1))
    kern_t = (lax.dot_general(a_hi, b_hi, nt, preferred_element_type=_F32)
              + lax.dot_general(a_hi, b_lo, nt, preferred_element_type=_F32)
              + lax.dot_general(a_lo, b_hi, nt, preferred_element_type=_F32))
    kern = jnp.transpose(kern_t)
    eye = (lax.broadcasted_iota(jnp.int32, (LANES, LANES), 0)
           == lax.broadcasted_iota(jnp.int32, (LANES, LANES), 1))
    kw_ref[:, :LANES] = (kern[:, :LANES] + jnp.where(eye, d_ref[...], 0.0)).astype(_BF16)
    kw_ref[:, LANES:] = kern[:, LANES:].astype(_BF16)

    cc = (LANES, LANES)
    w_lin = jnp.transpose(block_diag(wt_ref[0], cc)).astype(_BF16)
    w_gate = jnp.transpose(block_diag(wt_ref[1], cc)).astype(_BF16)
    wg_ref[...] = jnp.zeros(wg_ref.shape, _BF16)
    for t in range(2):
        wg_ref[t * LANES:(t + 1) * LANES, t * LANES:(t + 1) * LANES] = w_lin
        wg_ref[t * LANES:(t + 1) * LANES, (2 + t) * LANES:(3 + t) * LANES] = w_gate
    bg_ref[...] = jnp.concatenate([bglu_ref[0:1, :], bglu_ref[0:1, :], bglu_ref[1:2, :], bglu_ref[1:2, :]],
                                  axis=1)

    ach_ref[0:1, :], ach_ref[1:2, :] = pw[T]
    aseg_ref[0:1, :], aseg_ref[1:2, :] = _complex_power(pw[T][0], pw[T][1], seg)


def _ssm_weights(log_dt, a_re, a_im, b_re, b_im, c_re, c_im, d_skip, w_glu, b_glu, seg):
    P, H, T = SSM_STATE, SSM_GROUP, SSM_CHUNK
    SG, NS = SLAB_GROUPS, SSM_GROUPS // SLAB_GROUPS
    rows = jnp.stack([a_re.reshape(NS, SG * P), a_im.reshape(NS, SG * P),
                      jnp.repeat(log_dt, P).reshape(NS, SG * P)], axis=1)

    def per_group_rows(w, lead):
        t = w.reshape((NS, SG) + w.shape[1:])
        t = jnp.moveaxis(t, 2 + lead, 1)
        return t.reshape(NS, t.shape[1], -1)

    bt = jnp.stack([per_group_rows(b_re, 1), per_group_rows(b_im, 1)], axis=1)
    ct = jnp.stack([per_group_rows(c_re, 0), per_group_rows(c_im, 0)], axis=1)
    wt = jnp.stack([per_group_rows(w_glu[..., :H], 1), per_group_rows(w_glu[..., H:], 1)], axis=1)
    bglu = jnp.stack([b_glu[:, :H].reshape(NS, SG * H), b_glu[:, H:].reshape(NS, SG * H)], axis=1)
    d = d_skip.reshape(NS, 1, SG * H)

    blk3 = lambda i: (i, 0, 0)
    blk4 = lambda i: (i, 0, 0, 0)
    return pl.pallas_call(
        functools.partial(_ssm_prep_body, seg),
        grid=(NS,),
        in_specs=[
            pl.BlockSpec((None, 3, SG * P), blk3),
            pl.BlockSpec((None, 2, H, SG * P), blk4),
            pl.BlockSpec((None, 2, H, SG * P), blk4),
            pl.BlockSpec((None, 1, SG * H), blk3),
            pl.BlockSpec((None, 2, H, SG * H), blk4),
            pl.BlockSpec((None, 2, SG * H), blk3),
        ],
        out_specs=[
            pl.BlockSpec((None, T * LANES, 2 * SLAB_STATES), blk3),
            pl.BlockSpec((None, LANES, T * LANES), blk3),
            pl.BlockSpec((None, 2 * SLAB_STATES, T * LANES), blk3),
            pl.BlockSpec((None, 2 * LANES, 4 * LANES), blk3),
            pl.BlockSpec((None, 1, 4 * LANES), blk3),
            pl.BlockSpec((None, 2, SLAB_STATES), blk3),
            pl.BlockSpec((None, 2, SLAB_STATES), blk3),
        ],
        out_shape=[
            jax.ShapeDtypeStruct((NS, T * LANES, 2 * SLAB_STATES), _BF16),
            jax.ShapeDtypeStruct((NS, LANES, T * LANES), _BF16),
            jax.ShapeDtypeStruct((NS, 2 * SLAB_STATES, T * LANES), _BF16),
            jax.ShapeDtypeStruct((NS, 2 * LANES, 4 * LANES), _BF16),
            jax.ShapeDtypeStruct((NS, 1, 4 * LANES), _F32),
            jax.ShapeDtypeStruct((NS, 2, SLAB_STATES), _F32),
            jax.ShapeDtypeStruct((NS, 2, SLAB_STATES), _F32),
        ],
        compiler_params=pltpu.CompilerParams(
            dimension_semantics=("arbitrary",), vmem_limit_bytes=VMEM_LIMIT),
        name="ssm_prep",
    )(rows, bt, ct, d, wt, bglu)


def _ssm(u_t, weights):
    T, n_c, W = u_t.shape
    seg = n_c // SSM_SEGMENTS
    pitch = seg + SUBLANES
    ns = W // LANES
    wst, kw, cp, wg, bg, a_chunk, a_seg = weights
    slab = lambda i: (0, 0, i)
    blk = lambda i: (i, 0, 0)
    return pl.pallas_call(
        _ssm_body,
        grid=(ns,),
        in_specs=[
            pl.BlockSpec((T, n_c, LANES), slab),
            pl.BlockSpec((None, T * LANES, 2 * SLAB_STATES), blk),
            pl.BlockSpec((None, LANES, T * LANES), blk),
            pl.BlockSpec((None, 2 * SLAB_STATES, T * LANES), blk),
            pl.BlockSpec((None, 2 * LANES, 4 * LANES), blk),
            pl.BlockSpec((None, 1, 4 * LANES), blk),
            pl.BlockSpec((None, 2, SLAB_STATES), blk),
            pl.BlockSpec((None, 2, SLAB_STATES), blk),
        ],
        out_specs=pl.BlockSpec((T, n_c, LANES), slab),
        out_shape=jax.ShapeDtypeStruct((T, n_c, W), _BF16),
        scratch_shapes=[
            pltpu.VMEM((n_c, T * LANES), _BF16),
            pltpu.VMEM((T * LANES, T * LANES), _BF16),
            pltpu.VMEM((2 * SLAB_STATES // LANES, SSM_SEGMENTS * pitch, LANES), _F32),
            pltpu.VMEM((n_c, 2 * SLAB_STATES), _BF16),
        ],
        compiler_params=pltpu.CompilerParams(
            dimension_semantics=("arbitrary",), vmem_limit_bytes=VMEM_LIMIT),
        name="ssm",
    )(u_t, wst, kw, cp, wg, bg, a_chunk, a_seg)


def _outproj_body(n_cast, a_ref, s_ref, x_ref, ag_ref, sg_ref, w_ref, *refs):
    o_ref, sn_ref = refs[n_cast], refs[-1]
    for src, dst in zip(refs[:n_cast], refs[n_cast + 1:-1]):
        dst[...] = src[...].astype(_BF16)
    tm = x_ref.shape[0]
    an = _rms(a_ref[...].astype(_F32), ag_ref[...]).astype(_BF16)
    for t in range(SSM_CHUNK):
        sn_t = _rms(s_ref[t].astype(_F32), sg_ref[...])
        for s in range(SSM_WIDTH // LANES):
            sn_ref[s, pl.ds(t, tm // SSM_CHUNK, stride=SSM_CHUNK), :] = sn_t[:, s * LANES:(s + 1) * LANES]
    sn = jnp.concatenate([sn_ref[s] for s in range(SSM_WIDTH // LANES)], axis=1).astype(_BF16)
    mixed = jnp.concatenate([an, sn], axis=1)
    o_ref[...] = x_ref[...] + jnp.dot(mixed, w_ref[...], preferred_element_type=_F32)


def _outproj(attn, ssm_t, x, a_gain, s_gain, w_out, to_cast=(), *, tm=512):
    L, D = x.shape
    row = lambda i: (i, 0)
    fixed = lambda i: (0, 0)
    cast_specs, cast_shapes = _row_cast_specs(to_cast, L // tm)
    out = pl.pallas_call(
        functools.partial(_outproj_body, len(to_cast)),
        grid=(L // tm,),
        in_specs=[
            pl.BlockSpec((tm, ATTN_WIDTH), row),
            pl.BlockSpec((SSM_CHUNK, tm // SSM_CHUNK, SSM_WIDTH), lambda i: (0, i, 0)),
            pl.BlockSpec((tm, D), row),
            pl.BlockSpec((1, ATTN_WIDTH), fixed),
            pl.BlockSpec((1, SSM_WIDTH), fixed),
            pl.BlockSpec((ATTN_WIDTH + SSM_WIDTH, D), fixed),
        ] + cast_specs,
        out_specs=[pl.BlockSpec((tm, D), row)] + cast_specs,
        out_shape=[jax.ShapeDtypeStruct((L, D), _F32)] + cast_shapes,
        scratch_shapes=[pltpu.VMEM((SSM_WIDTH // LANES, tm, LANES), _F32)],
        compiler_params=pltpu.CompilerParams(
            dimension_semantics=("arbitrary",), vmem_limit_bytes=VMEM_LIMIT),
        name="outproj",
    )(attn, ssm_t, x, a_gain.reshape(1, -1), s_gain.reshape(1, -1), w_out, *to_cast)
    return out[0], out[1:]


def _layer(x, pos, invf, p):
    L = x.shape[0]
    seg = L // (SSM_SEGMENTS * SSM_CHUNK)
    act, (w_down1, w_in, w_out) = _ffn_up(
        x, p['ffn1_norm'], p['ffn1_w_gate'].astype(_BF16), p['ffn1_w_up'].astype(_BF16),
        [p['ffn1_w_down'], p['w_in'], p['w_out']])
    x = _ffn_down(act, w_down1, x)
    q, k2, v2, u_t = _proj(x, p['mix_norm'], w_in, pos, invf, p['q_norm'], p['k_norm'])
    attn, (w_gate2, w_up2) = _attn(q, k2, v2, p['attn_sinks'], [p['ffn2_w_gate'], p['ffn2_w_up']])
    weights = _ssm_weights(p['ssm_log_dt'], p['ssm_a_re'], p['ssm_a_im'], p['ssm_b_re'], p['ssm_b_im'],
                           p['ssm_c_re'], p['ssm_c_im'], p['ssm_d'], p['ssm_w_glu'], p['ssm_b_glu'], seg)
    ssm_t = _ssm(u_t, weights)
    x, (w_down2,) = _outproj(attn, ssm_t, x, p['attn_out_norm'], p['ssm_out_norm'], w_out,
                             [p['ffn2_w_down']])
    act, _ = _ffn_up(x, p['ffn2_norm'], w_gate2, w_up2)
    return _ffn_down(act, w_down2, x)


def kernel(x, positions, ffn1_norm, ffn1_w_gate, ffn1_w_up, ffn1_w_down, mix_norm, w_in, q_norm, k_norm,
           attn_sinks, ssm_log_dt, ssm_a_re, ssm_a_im, ssm_b_re, ssm_b_im, ssm_c_re, ssm_c_im, ssm_d,
           ssm_w_glu, ssm_b_glu, attn_out_norm, ssm_out_norm, w_out, ffn2_norm, ffn2_w_gate, ffn2_w_up,
           ffn2_w_down):
    params = dict(
        ffn1_norm=ffn1_norm, ffn1_w_gate=ffn1_w_gate, ffn1_w_up=ffn1_w_up, ffn1_w_down=ffn1_w_down,
        mix_norm=mix_norm, w_in=w_in, q_norm=q_norm, k_norm=k_norm, attn_sinks=attn_sinks,
        ssm_log_dt=ssm_log_dt, ssm_a_re=ssm_a_re, ssm_a_im=ssm_a_im, ssm_b_re=ssm_b_re, ssm_b_im=ssm_b_im,
        ssm_c_re=ssm_c_re, ssm_c_im=ssm_c_im, ssm_d=ssm_d, ssm_w_glu=ssm_w_glu, ssm_b_glu=ssm_b_glu,
        attn_out_norm=attn_out_norm, ssm_out_norm=ssm_out_norm, w_out=w_out,
        ffn2_norm=ffn2_norm, ffn2_w_gate=ffn2_w_gate, ffn2_w_up=ffn2_w_up, ffn2_w_down=ffn2_w_down)
    depth = ffn1_norm.shape[0]
    half = HEAD_DIM // 2
    inv_freq = ROPE_THETA ** (-jnp.arange(half, dtype=_F32) * 2.0 / HEAD_DIM)
    invf = jnp.tile(inv_freq, LANES // half).reshape(1, LANES)
    outs = []
    for b in range(x.shape[0]):
        xb = x[b]
        for i in range(depth):
            xb = _layer(xb, positions[b], invf, {name: val[i] for name, val in params.items()})
        outs.append(xb)
    return jnp.stack(outs, axis=0)
```

```python
import functools
import math

import jax
import jax.numpy as jnp
from jax import lax
from jax.experimental import pallas as pl
from jax.experimental.pallas import tpu as pltpu

HEAD_DIM = 64
N_Q_HEADS = 16
N_KV_HEADS = 4
ATTN_WIDTH = N_Q_HEADS * HEAD_DIM
KV_WIDTH = N_KV_HEADS * HEAD_DIM
BLOCK = 128
ROPE_THETA = 10000.0
SSM_GROUP = 16
SSM_GROUPS = 64
SSM_STATE = 64
SSM_WIDTH = SSM_GROUP * SSM_GROUPS
FFN_RESIDUAL = 0.5
EPS = 1e-6

LANES = 128
SUBLANES = 8
MXU_DIM = 256
SSM_CHUNK = 8
SSM_SEGMENTS = SUBLANES
SLAB_GROUPS = LANES // SSM_GROUP
SLAB_STATES = SLAB_GROUPS * SSM_STATE
SCAN_UNROLL = 4
VMEM_LIMIT = 56 * 1024 * 1024

_BF16 = jnp.bfloat16
_F32 = jnp.float32


def _rms(x, gain):
    ms = jnp.mean(x * x, axis=-1, keepdims=True)
    return x * lax.rsqrt(ms + EPS) * gain


def _ffn_up_body(n_cast, x_ref, gain_ref, wg_ref, wu_ref, *refs):
    src_refs, a_ref, dst_refs, h_ref = refs[:n_cast], refs[n_cast], refs[n_cast + 1:-1], refs[-1]
    first = pl.program_id(1) == 0

    @pl.when(first)
    def _():
        h_ref[...] = _rms(x_ref[...], gain_ref[...]).astype(_BF16)

    h = h_ref[...]
    g = jnp.dot(h, wg_ref[...], preferred_element_type=_F32)
    u = jnp.dot(h, wu_ref[...], preferred_element_type=_F32)
    a_ref[...] = (g * jax.nn.sigmoid(g) * u).astype(_BF16)

    for src, dst in zip(src_refs, dst_refs):
        dst[...] = src[...].astype(_BF16)


def _ffn_down_body(n_cast, a_ref, wd_ref, x_ref, *refs):
    o_ref = refs[n_cast]
    d = jnp.dot(a_ref[...], wd_ref[...], preferred_element_type=_F32)
    o_ref[...] = x_ref[...] + FFN_RESIDUAL * d
    for src, dst in zip(refs[:n_cast], refs[n_cast + 1:]):
        dst[...] = src[...].astype(_BF16)


def _row_cast_specs(arrays, n):
    pack = 2 * SUBLANES
    for w in arrays:
        assert w.shape[0] % (n * pack) == 0, w.shape
    specs = [pl.BlockSpec((w.shape[0] // n, w.shape[1]), lambda i: (i, 0)) for w in arrays]
    return specs, [jax.ShapeDtypeStruct(w.shape, _BF16) for w in arrays]


def _step_cast_spec(shape, ni, nj):
    R, C = shape
    pack = 2 * SUBLANES
    if R % ni == 0 and (R // ni) % pack == 0 and C % nj == 0 and (C // nj) % LANES == 0:
        return pl.BlockSpec((R // ni, C // nj), lambda i, j: (i, j))
    assert R % (ni * nj) == 0 and (R // (ni * nj)) % pack == 0, shape
    return pl.BlockSpec((R // (ni * nj), C), lambda i, j: (i * nj + j, 0))


def _ffn_up(x, gain, wg, wu, to_cast=(), *, tm=1024, tf=512):
    L, D = x.shape
    F = wg.shape[1]
    tm = min(tm, L)
    ni, nj = L // tm, F // tf
    cast_specs = [_step_cast_spec(w.shape, ni, nj) for w in to_cast]
    out = pl.pallas_call(
        functools.partial(_ffn_up_body, len(to_cast)),
        grid=(ni, nj),
        in_specs=[
            pl.BlockSpec((tm, D), lambda i, j: (i, 0)),
            pl.BlockSpec((1, D), lambda i, j: (0, 0)),
            pl.BlockSpec((D, tf), lambda i, j: (0, j)),
            pl.BlockSpec((D, tf), lambda i, j: (0, j)),
        ] + cast_specs,
        out_specs=[pl.BlockSpec((tm, tf), lambda i, j: (i, j))] + cast_specs,
        out_shape=[jax.ShapeDtypeStruct((L, F), _BF16)]
        + [jax.ShapeDtypeStruct(w.shape, _BF16) for w in to_cast],
        scratch_shapes=[pltpu.VMEM((tm, D), _BF16)],
        compiler_params=pltpu.CompilerParams(
            dimension_semantics=("arbitrary", "arbitrary"), vmem_limit_bytes=VMEM_LIMIT),
        name="ffn_up",
    )(x, gain.reshape(1, D), wg, wu, *to_cast)
    return out[0], out[1:]


def _ffn_down(act, wd, x, to_cast=(), *, tm=512):
    L, D = x.shape
    F = act.shape[1]
    tm = min(tm, L)
    cast_specs, cast_shapes = _row_cast_specs(to_cast, L // tm)
    out = pl.pallas_call(
        functools.partial(_ffn_down_body, len(to_cast)),
        grid=(L // tm,),
        in_specs=[
            pl.BlockSpec((tm, F), lambda i: (i, 0)),
            pl.BlockSpec((F, D), lambda i: (0, 0), pipeline_mode=pl.Buffered(1)),
            pl.BlockSpec((tm, D), lambda i: (i, 0)),
        ] + cast_specs,
        out_specs=[pl.BlockSpec((tm, D), lambda i: (i, 0))] + cast_specs,
        out_shape=[jax.ShapeDtypeStruct((L, D), _F32)] + cast_shapes,
        compiler_params=pltpu.CompilerParams(
            dimension_semantics=("arbitrary",), vmem_limit_bytes=VMEM_LIMIT),
        name="ffn_down",
    )(act, wd, x, *to_cast)
    return out[0], out[1:]


def _proj_body(x_ref, gain_ref, w_ref, pos_ref, invf_ref, qg_ref, kg_ref, pn_ref,
               q_ref, k_ref, v_ref, u_ref, us_ref):
    tm = x_ref.shape[0]
    h = _rms(x_ref[...], gain_ref[...]).astype(_BF16)
    group = 2 * MXU_DIM

    def project(g):
        return jnp.dot(h, w_ref[:, g * group:(g + 1) * group], preferred_element_type=_F32)

    half = HEAD_DIM // 2
    ang = pos_ref[...] * invf_ref[...]
    lane_q = lax.broadcasted_iota(jnp.int32, ang.shape, 1) // half

    def spread(table):
        parts = []
        for qtr in range(LANES // half):
            m = jnp.where(lane_q == qtr, table, 0.0)
            parts.append(m + pltpu.roll(m, half, 1) + pltpu.roll(m, 2 * half, 1) + pltpu.roll(m, 3 * half, 1))
        return jnp.concatenate(parts, axis=0)

    cos = spread(jnp.cos(ang))
    sin = spread(jnp.sin(ang))
    lane = lax.broadcasted_iota(jnp.int32, (tm, LANES), 1)
    first_half = (lane & (HEAD_DIM // 2)) == 0
    low_head = lane < HEAD_DIM
    sin_signed = jnp.where(first_half, -sin, sin)

    def norm_rotary(x4, gain):
        ms4 = jnp.dot((x4 * x4).astype(_BF16), pn_ref[...], preferred_element_type=_F32)
        out = []
        for part in range(2):
            lanes = slice(part * LANES, (part + 1) * LANES)
            y = x4[:, lanes] * lax.rsqrt(ms4[:, lanes] + EPS) * gain
            swapped = jnp.where(first_half, pltpu.roll(y, LANES - HEAD_DIM // 2, 1),
                                pltpu.roll(y, HEAD_DIM // 2, 1))
            out.append(y * cos + swapped * sin_signed)
        return out

    def dup_heads(xc):
        r = pltpu.roll(xc, HEAD_DIM, 1)
        return jnp.where(low_head, xc, r), jnp.where(low_head, r, xc)

    scale = 1.0 / math.sqrt(HEAD_DIM)
    assert KV_WIDTH == MXU_DIM and 2 * KV_WIDTH == group
    for g in range(ATTN_WIDTH // group):
        pg = project(g)
        for c in range(group // MXU_DIM):
            for part, qc in enumerate(norm_rotary(pg[:, c * MXU_DIM:(c + 1) * MXU_DIM], qg_ref[...])):
                at = g * group + c * MXU_DIM + part * LANES
                q_ref[:, at:at + LANES] = (qc * scale).astype(_BF16)
    pg = project(ATTN_WIDTH // group)
    for part, kc in enumerate(norm_rotary(pg[:, :KV_WIDTH], kg_ref[...])):
        ka, kb = dup_heads(kc)
        k_ref[:, 2 * part * LANES:(2 * part + 1) * LANES] = ka.astype(_BF16)
        k_ref[:, (2 * part + 1) * LANES:(2 * part + 2) * LANES] = kb.astype(_BF16)
    for part in range(KV_WIDTH // LANES):
        va, vb = dup_heads(pg[:, KV_WIDTH + part * LANES:KV_WIDTH + (part + 1) * LANES])
        v_ref[:, 2 * part * LANES:(2 * part + 1) * LANES] = va.astype(_BF16)
        v_ref[:, (2 * part + 1) * LANES:(2 * part + 2) * LANES] = vb.astype(_BF16)
    first_u = (ATTN_WIDTH + 2 * KV_WIDTH) // group
    per_group = group // LANES
    for g in range(SSM_WIDTH // group):
        pg = project(first_u + g)
        for s in range(per_group):
            us_ref[g * per_group + s] = pg[:, s * LANES:(s + 1) * LANES]
        for t in range(SSM_CHUNK):
            for s in range(g * per_group, (g + 1) * per_group):
                rows = us_ref[s, pl.ds(t, tm // SSM_CHUNK, stride=SSM_CHUNK), :]
                u_ref[t, :, s * LANES:(s + 1) * LANES] = rows.astype(_BF16)


def _proj(x, gain, w_in, pos, invf, q_gain, k_gain, *, tm=512):
    L, D = x.shape
    C = w_in.shape[1]
    quarters = LANES // (HEAD_DIM // 2)
    pos_f = pos.astype(_F32).reshape(L // tm, quarters, tm // quarters).transpose(0, 2, 1)
    pos_f = jnp.repeat(pos_f, HEAD_DIM // 2, axis=2)
    head_of_lane = jnp.arange(MXU_DIM) // HEAD_DIM
    pn = jnp.where(head_of_lane[:, None] == head_of_lane[None, :], 1.0 / HEAD_DIM, 0.0).astype(_BF16)
    qg = jnp.tile(q_gain, LANES // HEAD_DIM).reshape(1, LANES)
    kg = jnp.tile(k_gain, LANES // HEAD_DIM).reshape(1, LANES)
    row = lambda i: (i, 0)
    fixed = lambda i: (0, 0)
    return pl.pallas_call(
        _proj_body,
        grid=(L // tm,),
        in_specs=[
            pl.BlockSpec((tm, D), row),
            pl.BlockSpec((1, D), fixed),
            pl.BlockSpec((D, C), fixed),
            pl.BlockSpec((None, tm // quarters, LANES), lambda i: (i, 0, 0)),
            pl.BlockSpec((1, LANES), fixed),
            pl.BlockSpec((1, LANES), fixed),
            pl.BlockSpec((1, LANES), fixed),
            pl.BlockSpec((MXU_DIM, MXU_DIM), fixed),
        ],
        out_specs=[
            pl.BlockSpec((tm, ATTN_WIDTH), row),
            pl.BlockSpec((tm, 2 * KV_WIDTH), row),
            pl.BlockSpec((tm, 2 * KV_WIDTH), row),
            pl.BlockSpec((SSM_CHUNK, tm // SSM_CHUNK, SSM_WIDTH), lambda i: (0, i, 0)),
        ],
        out_shape=[
            jax.ShapeDtypeStruct((L, ATTN_WIDTH), _BF16),
            jax.ShapeDtypeStruct((L, 2 * KV_WIDTH), _BF16),
            jax.ShapeDtypeStruct((L, 2 * KV_WIDTH), _BF16),
            jax.ShapeDtypeStruct((SSM_CHUNK, L // SSM_CHUNK, SSM_WIDTH), _BF16),
        ],
        scratch_shapes=[pltpu.VMEM((SSM_WIDTH // LANES, tm, LANES), _F32)],
        compiler_params=pltpu.CompilerParams(
            dimension_semantics=("arbitrary",), vmem_limit_bytes=VMEM_LIMIT),
        name="proj",
    )(x, gain.reshape(1, D), w_in, pos_f, invf, qg, kg, pn)


def _attn_body(n_cast, sink_ref, q_ref, kc_ref, vc_ref, kp_ref, vp_ref, *refs):
    o_ref = refs[n_cast]
    for src, dst in zip(refs[:n_cast], refs[n_cast + 1:]):
        dst[...] = src[...].astype(_BF16)
    tq = q_ref.shape[0]
    kj = lax.broadcasted_iota(jnp.int32, (BLOCK, BLOCK), 0)
    qi = lax.broadcasted_iota(jnp.int32, (BLOCK, BLOCK), 1)
    from_prev = kj > qi
    has_prev = pl.program_id(0) > 0
    low_head = lax.broadcasted_iota(jnp.int32, (2 * BLOCK, LANES), 1) < HEAD_DIM
    nt = (((1,), (1,)), ((), ()))
    tn = (((0,), (0,)), ((), ()))

    for b in range(tq // BLOCK):
        rows = slice(b * BLOCK, (b + 1) * BLOCK)
        for hk in range(N_KV_HEADS):
            cols = slice(hk * LANES, (hk + 1) * LANES)
            if b == 0:
                kd = jnp.concatenate([kp_ref[:, cols], kc_ref[0:BLOCK, cols]], axis=0)
                vd = jnp.concatenate([vp_ref[:, cols], vc_ref[0:BLOCK, cols]], axis=0)
            else:
                kd = kc_ref[(b - 1) * BLOCK:(b + 1) * BLOCK, cols]
                vd = vc_ref[(b - 1) * BLOCK:(b + 1) * BLOCK, cols]
            zero = jnp.zeros_like(kd)
            k_half = (jnp.where(low_head, kd, zero), jnp.where(low_head, zero, kd))
            v_half = (jnp.where(low_head, vd, zero), jnp.where(low_head, zero, vd))
            for pp in range(2):
                pair = hk * 2 + pp
                qp = q_ref[rows, pair * LANES:(pair + 1) * LANES]
                acc = None
                for half in range(2):
                    sink = sink_ref[pair * 2 + half]
                    s2 = lax.dot_general(k_half[half], qp, nt, preferred_element_type=_F32)
                    s_prev = s2[:BLOCK]
                    if b == 0:
                        s_prev = jnp.where(has_prev, s_prev, -jnp.inf)
                    s = jnp.where(from_prev, s_prev, s2[BLOCK:])
                    m = jnp.maximum(jnp.max(s, axis=0, keepdims=True), sink)
                    p = jnp.exp(s - m)
                    den = jnp.sum(p, axis=0, keepdims=True) + jnp.exp(sink - m)
                    pb = p.astype(_BF16)
                    pz = jnp.zeros_like(pb)
                    p2 = jnp.concatenate([jnp.where(from_prev, pb, pz), jnp.where(from_prev, pz, pb)], axis=0)
                    o = lax.dot_general(v_half[half], p2, tn, preferred_element_type=_F32) * (1.0 / den)
                    acc = o if acc is None else acc + o
                o_ref[rows, pair * LANES:(pair + 1) * LANES] = jnp.transpose(acc).astype(_BF16)


def _attn(q, k2, v2, sinks, to_cast=(), *, tq=512):
    L = q.shape[0]
    per = tq // BLOCK
    row = lambda i: (i, 0)
    prev = lambda i: (jnp.maximum(i * per - 1, 0), 0)
    cast_specs, cast_shapes = _row_cast_specs(to_cast, L // tq)
    out = pl.pallas_call(
        functools.partial(_attn_body, len(to_cast)),
        grid=(L // tq,),
        in_specs=[
            pl.BlockSpec(memory_space=pltpu.SMEM),
            pl.BlockSpec((tq, ATTN_WIDTH), row),
            pl.BlockSpec((tq, 2 * KV_WIDTH), row),
            pl.BlockSpec((tq, 2 * KV_WIDTH), row),
            pl.BlockSpec((BLOCK, 2 * KV_WIDTH), prev),
            pl.BlockSpec((BLOCK, 2 * KV_WIDTH), prev),
        ] + cast_specs,
        out_specs=[pl.BlockSpec((tq, ATTN_WIDTH), row)] + cast_specs,
        out_shape=[jax.ShapeDtypeStruct((L, ATTN_WIDTH), _BF16)] + cast_shapes,
        compiler_params=pltpu.CompilerParams(
            dimension_semantics=("arbitrary",), vmem_limit_bytes=VMEM_LIMIT),
        name="attn",
    )(sinks, q, k2, v2, k2, v2, *to_cast)
    return out[0], out[1:]


def _ssm_body(u_ref, wst_ref, kw_ref, cp_ref, wg_ref, bg_ref, a_ref, aseg_ref, o_ref,
              lhs_ref, toep_ref, s_ref, xb_ref):
    T = SSM_CHUNK
    n_c = u_ref.shape[1]
    seg = n_c // SSM_SEGMENTS
    pitch = s_ref.shape[1] // SSM_SEGMENTS
    n_state_slabs = SLAB_STATES // LANES
    pair_w = 2 * LANES
    out_w = 2 * MXU_DIM

    for t in range(T):
        lhs_ref[:, t * LANES:(t + 1) * LANES] = u_ref[t]

    toep_ref[...] = jnp.zeros(toep_ref.shape, _BF16)
    for t in range(T):
        for tp in range(t, T):
            toep_ref[t * LANES:(t + 1) * LANES, tp * LANES:(tp + 1) * LANES] = (
                kw_ref[:, (tp - t) * LANES:(tp - t + 1) * LANES])

    per_dot = out_w // LANES
    for nb in range(2 * n_state_slabs // per_dot):
        res = jnp.dot(lhs_ref[...], wst_ref[:, nb * out_w:(nb + 1) * out_w], preferred_element_type=_F32)
        for part in range(per_dot):
            for j in range(SSM_SEGMENTS):
                s_ref[per_dot * nb + part, j * pitch:j * pitch + seg, :] = (
                    res[j * seg:(j + 1) * seg, part * LANES:(part + 1) * LANES])

    shape = (SSM_SEGMENTS, LANES)
    a_re = [jnp.broadcast_to(a_ref[0:1, k * LANES:(k + 1) * LANES], shape) for k in range(n_state_slabs)]
    a_im = [jnp.broadcast_to(a_ref[1:2, k * LANES:(k + 1) * LANES], shape) for k in range(n_state_slabs)]
    g_re = [jnp.broadcast_to(aseg_ref[0:1, k * LANES:(k + 1) * LANES], shape) for k in range(n_state_slabs)]
    g_im = [jnp.broadcast_to(aseg_ref[1:2, k * LANES:(k + 1) * LANES], shape) for k in range(n_state_slabs)]

    def seg_rows(w):
        return pl.ds(w, SSM_SEGMENTS, stride=pitch)

    def advance(w, carry, store):
        out = []
        for k in range(n_state_slabs):
            z_re, z_im = carry[2 * k], carry[2 * k + 1]
            s_re = s_ref[k, seg_rows(w), :]
            s_im = s_ref[n_state_slabs + k, seg_rows(w), :]
            if store:
                s_ref[k, seg_rows(w), :] = z_re
                s_ref[n_state_slabs + k, seg_rows(w), :] = z_im
            out.append(a_re[k] * z_re - a_im[k] * z_im + s_re)
            out.append(a_re[k] * z_im + a_im[k] * z_re + s_im)
        return tuple(out)

    zero = jnp.zeros(shape, _F32)
    ends = lax.fori_loop(0, seg, lambda w, c: advance(w, c, False), (zero,) * (2 * n_state_slabs),
                         unroll=SCAN_UNROLL)

    segidx = lax.broadcasted_iota(jnp.int32, shape, 0)
    init = []
    for k in range(n_state_slabs):
        i_re, i_im = zero, zero
        f_re, f_im = ends[2 * k], ends[2 * k + 1]
        for j in range(SSM_SEGMENTS - 1):
            c_re = g_re[k] * i_re - g_im[k] * i_im + f_re
            c_im = g_re[k] * i_im + g_im[k] * i_re + f_im
            i_re = jnp.where(segidx == j + 1, pltpu.roll(c_re, 1, 0), i_re)
            i_im = jnp.where(segidx == j + 1, pltpu.roll(c_im, 1, 0), i_im)
        init += [i_re, i_im]

    lax.fori_loop(0, seg, lambda w, c: advance(w, c, True), tuple(init), unroll=SCAN_UNROLL)

    for col in range(2 * n_state_slabs):
        for j in range(SSM_SEGMENTS):
            xb_ref[j * seg:(j + 1) * seg, col * LANES:(col + 1) * LANES] = (
                s_ref[col, j * pitch:j * pitch + seg, :].astype(_BF16))

    steps = out_w // LANES
    for i in range(T // steps):
        kk = (i + 1) * out_w
        cols = slice(i * out_w, (i + 1) * out_w)
        y = (jnp.dot(lhs_ref[:, :kk], toep_ref[:kk, cols], preferred_element_type=_F32)
             + jnp.dot(xb_ref[...], cp_ref[:, cols], preferred_element_type=_F32))
        z = jax.nn.gelu(y, approximate=True).astype(_BF16)
        for pp in range(steps // 2):
            gt = jnp.dot(z[:, pp * pair_w:(pp + 1) * pair_w], wg_ref[...],
                         preferred_element_type=_F32) + bg_ref[...]
            out = gt[:, :pair_w] * jax.nn.sigmoid(gt[:, pair_w:])
            t0 = i * steps + 2 * pp
            o_ref[t0] = out[:, :LANES].astype(_BF16)
            o_ref[t0 + 1] = out[:, LANES:].astype(_BF16)


def _complex_power(re, im, n):
    out_re, out_im = None, None
    while n:
        if n & 1:
            if out_re is None:
                out_re, out_im = re, im
            else:
                out_re, out_im = out_re * re - out_im * im, out_re * im + out_im * re
        n >>= 1
        if n:
            re, im = re * re - im * im, 2.0 * re * im
    return out_re, out_im


def _ssm_prep_body(seg, rows_ref, bt_ref, ct_ref, d_ref, wt_ref, bglu_ref,
                   wst_ref, kw_ref, cp_ref, wg_ref, bg_ref, ach_ref, aseg_ref):
    T, H = SSM_CHUNK, SSM_GROUP
    a_re, a_im = rows_ref[0:1, :], rows_ref[1:2, :]
    dt = jnp.exp(rows_ref[2:3, :])
    mag = jnp.exp(a_re * dt)
    ab_re = mag * jnp.cos(a_im * dt)
    ab_im = mag * jnp.sin(a_im * dt)
    inv_den = 1.0 / (a_re * a_re + a_im * a_im)
    nr = ab_re - 1.0
    coef_re = (nr * a_re + ab_im * a_im) * inv_den
    coef_im = (ab_im * a_re - nr * a_im) * inv_den

    def block_diag(t16, shape):
        row_g = lax.broadcasted_iota(jnp.int32, shape, 0) // H
        col_g = lax.broadcasted_iota(jnp.int32, shape, 1) // (shape[1] // SLAB_GROUPS)
        return jnp.where(row_g == col_g, jnp.concatenate([t16] * SLAB_GROUPS, axis=0), 0.0)

    cs = (LANES, SLAB_STATES)
    b_re, b_im = block_diag(bt_ref[0], cs), block_diag(bt_ref[1], cs)
    bb_re = coef_re * b_re - coef_im * b_im
    bb_im = coef_re * b_im + coef_im * b_re
    ct_re, ct_im = block_diag(ct_ref[0], cs), block_diag(ct_ref[1], cs)

    pw = [(jnp.ones_like(ab_re), jnp.zeros_like(ab_im))]
    for _ in range(T):
        pr, pi = pw[-1]
        pw.append((pr * ab_re - pi * ab_im, pr * ab_im + pi * ab_re))

    for t in range(T):
        pr, pi = pw[T - 1 - t]
        wst_ref[t * LANES:(t + 1) * LANES, :SLAB_STATES] = (bb_re * pr - bb_im * pi).astype(_BF16)
        wst_ref[t * LANES:(t + 1) * LANES, SLAB_STATES:] = (bb_re * pi + bb_im * pr).astype(_BF16)

    lags = []
    for k in range(T + 1):
        if k < T:
            lags.append(jnp.concatenate([ct_re, ct_im], axis=1))
        if k > 0:
            cp_ref[:SLAB_STATES, (k - 1) * LANES:k * LANES] = jnp.transpose(ct_re).astype(_BF16)
            cp_ref[SLAB_STATES:, (k - 1) * LANES:k * LANES] = jnp.transpose(-ct_im).astype(_BF16)
        ct_re, ct_im = ct_re * ab_re - ct_im * ab_im, ct_re * ab_im + ct_im * ab_re

    def split(a):
        hi = a.astype(_BF16)
        return hi, (a - hi.astype(_F32)).astype(_BF16)

    nt = (((1,), (1,)), ((), ()))
    a_hi, a_lo = split(jnp.concatenate(lags, axis=0))
    b_hi, b_lo = split(jnp.concatenate([bb_re, -bb_im], axis=1))
    kern_t = (lax.dot_general(a_hi, b_hi, nt, preferred_element_type=_F32)
              + lax.dot_general(a_hi, b_lo, nt, preferred_element_type=_F32)
              + lax.dot_general(a_lo, b_hi, nt, preferred_element_type=_F32))
    kern = jnp.transpose(kern_t)
    eye = (lax.broadcasted_iota(jnp.int32, (LANES, LANES), 0)
           == lax.broadcasted_iota(jnp.int32, (LANES, LANES), 1))
    kw_ref[:, :LANES] = (kern[:, :LANES] + jnp.where(eye, d_ref[...], 0.0)).astype(_BF16)
    kw_ref[:, LANES:] = kern[:, LANES:].astype(_BF16)

    cc = (LANES, LANES)
    w_lin = jnp.transpose(block_diag(wt_ref[0], cc)).astype(_BF16)
    w_gate = jnp.transpose(block_diag(wt_ref[1], cc)).astype(_BF16)
    wg_ref[...] = jnp.zeros(wg_ref.shape, _BF16)
    for t in range(2):
        wg_ref[t * LANES:(t + 1) * LANES, t * LANES:(t + 1) * LANES] = w_lin
        wg_ref[t * LANES:(t + 1) * LANES, (2 + t) * LANES:(3 + t) * LANES] = w_gate
    bg_ref[...] = jnp.concatenate([bglu_ref[0:1, :], bglu_ref[0:1, :], bglu_ref[1:2, :], bglu_ref[1:2, :]],
                                  axis=1)

    ach_ref[0:1, :], ach_ref[1:2, :] = pw[T]
    aseg_ref[0:1, :], aseg_ref[1:2, :] = _complex_power(pw[T][0], pw[T][1], seg)


def _ssm_weights(log_dt, a_re, a_im, b_re, b_im, c_re, c_im, d_skip, w_glu, b_glu, seg):
    P, H, T = SSM_STATE, SSM_GROUP, SSM_CHUNK
    SG, NS = SLAB_GROUPS, SSM_GROUPS // SLAB_GROUPS
    rows = jnp.stack([a_re.reshape(NS, SG * P), a_im.reshape(NS, SG * P),
                      jnp.repeat(log_dt, P).reshape(NS, SG * P)], axis=1)

    def per_group_rows(w, lead):
        t = w.reshape((NS, SG) + w.shape[1:])
        t = jnp.moveaxis(t, 2 + lead, 1)
        return t.reshape(NS, t.shape[1], -1)

    bt = jnp.stack([per_group_rows(b_re, 1), per_group_rows(b_im, 1)], axis=1)
    ct = jnp.stack([per_group_rows(c_re, 0), per_group_rows(c_im, 0)], axis=1)
    wt = jnp.stack([per_group_rows(w_glu[..., :H], 1), per_group_rows(w_glu[..., H:], 1)], axis=1)
    bglu = jnp.stack([b_glu[:, :H].reshape(NS, SG * H), b_glu[:, H:].reshape(NS, SG * H)], axis=1)
    d = d_skip.reshape(NS, 1, SG * H)

    blk3 = lambda i: (i, 0, 0)
    blk4 = lambda i: (i, 0, 0, 0)
    return pl.pallas_call(
        functools.partial(_ssm_prep_body, seg),
        grid=(NS,),
        in_specs=[
            pl.BlockSpec((None, 3, SG * P), blk3),
            pl.BlockSpec((None, 2, H, SG * P), blk4),
            pl.BlockSpec((None, 2, H, SG * P), blk4),
            pl.BlockSpec((None, 1, SG * H), blk3),
            pl.BlockSpec((None, 2, H, SG * H), blk4),
            pl.BlockSpec((None, 2, SG * H), blk3),
        ],
        out_specs=[
            pl.BlockSpec((None, T * LANES, 2 * SLAB_STATES), blk3),
            pl.BlockSpec((None, LANES, T * LANES), blk3),
            pl.BlockSpec((None, 2 * SLAB_STATES, T * LANES), blk3),
            pl.BlockSpec((None, 2 * LANES, 4 * LANES), blk3),
            pl.BlockSpec((None, 1, 4 * LANES), blk3),
            pl.BlockSpec((None, 2, SLAB_STATES), blk3),
            pl.BlockSpec((None, 2, SLAB_STATES), blk3),
        ],
        out_shape=[
            jax.ShapeDtypeStruct((NS, T * LANES, 2 * SLAB_STATES), _BF16),
            jax.ShapeDtypeStruct((NS, LANES, T * LANES), _BF16),
            jax.ShapeDtypeStruct((NS, 2 * SLAB_STATES, T * LANES), _BF16),
            jax.ShapeDtypeStruct((NS, 2 * LANES, 4 * LANES), _BF16),
            jax.ShapeDtypeStruct((NS, 1, 4 * LANES), _F32),
            jax.ShapeDtypeStruct((NS, 2, SLAB_STATES), _F32),
            jax.ShapeDtypeStruct((NS, 2, SLAB_STATES), _F32),
        ],
        compiler_params=pltpu.CompilerParams(
            dimension_semantics=("arbitrary",), vmem_limit_bytes=VMEM_LIMIT),
        name="ssm_prep",
    )(rows, bt, ct, d, wt, bglu)


def _ssm(u_t, weights):
    T, n_c, W = u_t.shape
    seg = n_c // SSM_SEGMENTS
    pitch = seg + SUBLANES
    ns = W // LANES
    wst, kw, cp, wg, bg, a_chunk, a_seg = weights
    slab = lambda i: (0, 0, i)
    blk = lambda i: (i, 0, 0)
    return pl.pallas_call(
        _ssm_body,
        grid=(ns,),
        in_specs=[
            pl.BlockSpec((T, n_c, LANES), slab),
            pl.BlockSpec((None, T * LANES, 2 * SLAB_STATES), blk),
            pl.BlockSpec((None, LANES, T * LANES), blk),
            pl.BlockSpec((None, 2 * SLAB_STATES, T * LANES), blk),
            pl.BlockSpec((None, 2 * LANES, 4 * LANES), blk),
            pl.BlockSpec((None, 1, 4 * LANES), blk),
            pl.BlockSpec((None, 2, SLAB_STATES), blk),
            pl.BlockSpec((None, 2, SLAB_STATES), blk),
        ],
        out_specs=pl.BlockSpec((T, n_c, LANES), slab),
        out_shape=jax.ShapeDtypeStruct((T, n_c, W), _BF16),
        scratch_shapes=[
            pltpu.VMEM((n_c, T * LANES), _BF16),
            pltpu.VMEM((T * LANES, T * LANES), _BF16),
            pltpu.VMEM((2 * SLAB_STATES // LANES, SSM_SEGMENTS * pitch, LANES), _F32),
            pltpu.VMEM((n_c, 2 * SLAB_STATES), _BF16),
        ],
        compiler_params=pltpu.CompilerParams(
            dimension_semantics=("arbitrary",), vmem_limit_bytes=VMEM_LIMIT),
        name="ssm",
    )(u_t, wst, kw, cp, wg, bg, a_chunk, a_seg)


def _outproj_body(n_cast, a_ref, s_ref, x_ref, ag_ref, sg_ref, w_ref, *refs):
    o_ref, sn_ref = refs[n_cast], refs[-1]
    for src, dst in zip(refs[:n_cast], refs[n_cast + 1:-1]):
        dst[...] = src[...].astype(_BF16)
    tm = x_ref.shape[0]
    an = _rms(a_ref[...].astype(_F32), ag_ref[...]).astype(_BF16)
    for t in range(SSM_CHUNK):
        sn_t = _rms(s_ref[t].astype(_F32), sg_ref[...])
        for s in range(SSM_WIDTH // LANES):
            sn_ref[s, pl.ds(t, tm // SSM_CHUNK, stride=SSM_CHUNK), :] = sn_t[:, s * LANES:(s + 1) * LANES]
    sn = jnp.concatenate([sn_ref[s] for s in range(SSM_WIDTH // LANES)], axis=1).astype(_BF16)
    mixed = jnp.concatenate([an, sn], axis=1)
    o_ref[...] = x_ref[...] + jnp.dot(mixed, w_ref[...], preferred_element_type=_F32)


def _outproj(attn, ssm_t, x, a_gain, s_gain, w_out, to_cast=(), *, tm=512):
    L, D = x.shape
    row = lambda i: (i, 0)
    fixed = lambda i: (0, 0)
    cast_specs, cast_shapes = _row_cast_specs(to_cast, L // tm)
    out = pl.pallas_call(
        functools.partial(_outproj_body, len(to_cast)),
        grid=(L // tm,),
        in_specs=[
            pl.BlockSpec((tm, ATTN_WIDTH), row),
            pl.BlockSpec((SSM_CHUNK, tm // SSM_CHUNK, SSM_WIDTH), lambda i: (0, i, 0)),
            pl.BlockSpec((tm, D), row),
            pl.BlockSpec((1, ATTN_WIDTH), fixed),
            pl.BlockSpec((1, SSM_WIDTH), fixed),
            pl.BlockSpec((ATTN_WIDTH + SSM_WIDTH, D), fixed),
        ] + cast_specs,
        out_specs=[pl.BlockSpec((tm, D), row)] + cast_specs,
        out_shape=[jax.ShapeDtypeStruct((L, D), _F32)] + cast_shapes,
        scratch_shapes=[pltpu.VMEM((SSM_WIDTH // LANES, tm, LANES), _F32)],
        compiler_params=pltpu.CompilerParams(
            dimension_semantics=("arbitrary",), vmem_limit_bytes=VMEM_LIMIT),
        name="outproj",
    )(attn, ssm_t, x, a_gain.reshape(1, -1), s_gain.reshape(1, -1), w_out, *to_cast)
    return out[0], out[1:]


def _layer(x, pos, invf, p):
    L = x.shape[0]
    seg = L // (SSM_SEGMENTS * SSM_CHUNK)
    act, (w_down1,) = _ffn_up(
        x, p['ffn1_norm'], p['ffn1_w_gate'].astype(_BF16), p['ffn1_w_up'].astype(_BF16),
        [p['ffn1_w_down']])
    x, (w_in, w_out) = _ffn_down(act, w_down1, x, [p['w_in'], p['w_out']])
    q, k2, v2, u_t = _proj(x, p['mix_norm'], w_in, pos, invf, p['q_norm'], p['k_norm'])
    attn, (w_gate2, w_up2) = _attn(q, k2, v2, p['attn_sinks'], [p['ffn2_w_gate'], p['ffn2_w_up']])
    weights = _ssm_weights(p['ssm_log_dt'], p['ssm_a_re'], p['ssm_a_im'], p['ssm_b_re'], p['ssm_b_im'],
                           p['ssm_c_re'], p['ssm_c_im'], p['ssm_d'], p['ssm_w_glu'], p['ssm_b_glu'], seg)
    ssm_t = _ssm(u_t, weights)
    x, (w_down2,) = _outproj(attn, ssm_t, x, p['attn_out_norm'], p['ssm_out_norm'], w_out,
                             [p['ffn2_w_down']])
    act, _ = _ffn_up(x, p['ffn2_norm'], w_gate2, w_up2)
    return _ffn_down(act, w_down2, x)[0]


def kernel(x, positions, ffn1_norm, ffn1_w_gate, ffn1_w_up, ffn1_w_down, mix_norm, w_in, q_norm, k_norm,
           attn_sinks, ssm_log_dt, ssm_a_re, ssm_a_im, ssm_b_re, ssm_b_im, ssm_c_re, ssm_c_im, ssm_d,
           ssm_w_glu, ssm_b_glu, attn_out_norm, ssm_out_norm, w_out, ffn2_norm, ffn2_w_gate, ffn2_w_up,
           ffn2_w_down):
    params = dict(
        ffn1_norm=ffn1_norm, ffn1_w_gate=ffn1_w_gate, ffn1_w_up=ffn1_w_up, ffn1_w_down=ffn1_w_down,
        mix_norm=mix_norm, w_in=w_in, q_norm=q_norm, k_norm=k_norm, attn_sinks=attn_sinks,
        ssm_log_dt=ssm_log_dt, ssm_a_re=ssm_a_re, ssm_a_im=ssm_a_im, ssm_b_re=ssm_b_re, ssm_b_im=ssm_b_im,
        ssm_c_re=ssm_c_re, ssm_c_im=ssm_c_im, ssm_d=ssm_d, ssm_w_glu=ssm_w_glu, ssm_b_glu=ssm_b_glu,
        attn_out_norm=attn_out_norm, ssm_out_norm=ssm_out_norm, w_out=w_out,
        ffn2_norm=ffn2_norm, ffn2_w_gate=ffn2_w_gate, ffn2_w_up=ffn2_w_up, ffn2_w_down=ffn2_w_down)
    depth = ffn1_norm.shape[0]
    half = HEAD_DIM // 2
    inv_freq = ROPE_THETA ** (-jnp.arange(half, dtype=_F32) * 2.0 / HEAD_DIM)
    invf = jnp.tile(inv_freq, LANES // half).reshape(1, LANES)
    outs = []
    for b in range(x.shape[0]):
        xb = x[b]
        for i in range(depth):
            xb = _layer(xb, positions[b], invf, {name: val[i] for name, val in params.items()})
        outs.append(xb)
    return jnp.stack(outs, axis=0)
```

```python
import functools
import math

import jax
import jax.numpy as jnp
from jax import lax
from jax.experimental import pallas as pl
from jax.experimental.pallas import tpu as pltpu

HEAD_DIM = 64
N_Q_HEADS = 16
N_KV_HEADS = 4
ATTN_WIDTH = N_Q_HEADS * HEAD_DIM
KV_WIDTH = N_KV_HEADS * HEAD_DIM
BLOCK = 128
ROPE_THETA = 10000.0
SSM_GROUP = 16
SSM_GROUPS = 64
SSM_STATE = 64
SSM_WIDTH = SSM_GROUP * SSM_GROUPS
FFN_RESIDUAL = 0.5
EPS = 1e-6

LANES = 128
SUBLANES = 8
MXU_DIM = 256
SSM_CHUNK = 8
SSM_SEGMENTS = SUBLANES
SLAB_GROUPS = LANES // SSM_GROUP
SLAB_STATES = SLAB_GROUPS * SSM_STATE
SCAN_UNROLL = 4
VMEM_LIMIT = 56 * 1024 * 1024

_BF16 = jnp.bfloat16
_F32 = jnp.float32


def _rms(x, gain):
    ms = jnp.mean(x * x, axis=-1, keepdims=True)
    return x * lax.rsqrt(ms + EPS) * gain


def _ffn_up_body(n_cast, xa_ref, xb_ref, gain_ref, wg_ref, wu_ref, *refs):
    src_refs, a_ref, dst_refs, h_ref = refs[:n_cast], refs[n_cast], refs[n_cast + 1:-1], refs[-1]
    i, j = pl.program_id(0), pl.program_id(1)
    half = xa_ref.shape[0]

    def normalise():
        h_ref[:half] = _rms(xa_ref[...], gain_ref[...]).astype(_BF16)
        h_ref[half:] = _rms(xb_ref[...], gain_ref[...]).astype(_BF16)

    pl.when((i == 0) & (j == 0))(normalise)

    h = h_ref[...]
    g = jnp.dot(h, wg_ref[...], preferred_element_type=_F32)
    u = jnp.dot(h, wu_ref[...], preferred_element_type=_F32)
    a_ref[...] = (g * jax.nn.sigmoid(g) * u).astype(_BF16)

    for src, dst in zip(src_refs, dst_refs):
        dst[...] = src[...].astype(_BF16)

    pl.when(j == pl.num_programs(1) - 1)(normalise)


def _ffn_down_body(n_cast, a_ref, wd_ref, x_ref, *refs):
    o_ref = refs[n_cast]
    d = jnp.dot(a_ref[...], wd_ref[...], preferred_element_type=_F32)
    o_ref[...] = x_ref[...] + FFN_RESIDUAL * d
    for src, dst in zip(refs[:n_cast], refs[n_cast + 1:]):
        dst[...] = src[...].astype(_BF16)


def _row_cast_specs(arrays, n):
    pack = 2 * SUBLANES
    for w in arrays:
        assert w.shape[0] % (n * pack) == 0, w.shape
    specs = [pl.BlockSpec((w.shape[0] // n, w.shape[1]), lambda i: (i, 0)) for w in arrays]
    return specs, [jax.ShapeDtypeStruct(w.shape, _BF16) for w in arrays]


def _step_cast_spec(shape, ni, nj):
    R, C = shape
    pack = 2 * SUBLANES
    if R % ni == 0 and (R // ni) % pack == 0 and C % nj == 0 and (C // nj) % LANES == 0:
        return pl.BlockSpec((R // ni, C // nj), lambda i, j: (i, j))
    assert R % (ni * nj) == 0 and (R // (ni * nj)) % pack == 0, shape
    return pl.BlockSpec((R // (ni * nj), C), lambda i, j: (i * nj + j, 0))


def _ffn_up(x, gain, wg, wu, to_cast=(), *, tm=1024, tf=512):
    L, D = x.shape
    F = wg.shape[1]
    tm = min(tm, L)
    ni, nj = L // tm, F // tf
    cast_specs = [_step_cast_spec(w.shape, ni, nj) for w in to_cast]

    def x_half(part):
        def index(i, j):
            tile = jnp.minimum(i + (j >= nj - 2 + part).astype(jnp.int32), ni - 1)
            return 2 * tile + part, 0
        return pl.BlockSpec((tm // 2, D), index)

    out = pl.pallas_call(
        functools.partial(_ffn_up_body, len(to_cast)),
        grid=(ni, nj),
        in_specs=[
            x_half(0),
            x_half(1),
            pl.BlockSpec((1, D), lambda i, j: (0, 0)),
            pl.BlockSpec((D, tf), lambda i, j: (0, j)),
            pl.BlockSpec((D, tf), lambda i, j: (0, j)),
        ] + cast_specs,
        out_specs=[pl.BlockSpec((tm, tf), lambda i, j: (i, j))] + cast_specs,
        out_shape=[jax.ShapeDtypeStruct((L, F), _BF16)]
        + [jax.ShapeDtypeStruct(w.shape, _BF16) for w in to_cast],
        scratch_shapes=[pltpu.VMEM((tm, D), _BF16)],
        compiler_params=pltpu.CompilerParams(
            dimension_semantics=("arbitrary", "arbitrary"), vmem_limit_bytes=VMEM_LIMIT),
        name="ffn_up",
    )(x, x, gain.reshape(1, D), wg, wu, *to_cast)
    return out[0], out[1:]


def _ffn_down(act, wd, x, to_cast=(), *, tm=512):
    L, D = x.shape
    F = act.shape[1]
    tm = min(tm, L)
    cast_specs, cast_shapes = _row_cast_specs(to_cast, L // tm)
    out = pl.pallas_call(
        functools.partial(_ffn_down_body, len(to_cast)),
        grid=(L // tm,),
        in_specs=[
            pl.BlockSpec((tm, F), lambda i: (i, 0)),
            pl.BlockSpec((F, D), lambda i: (0, 0), pipeline_mode=pl.Buffered(1)),
            pl.BlockSpec((tm, D), lambda i: (i, 0)),
        ] + cast_specs,
        out_specs=[pl.BlockSpec((tm, D), lambda i: (i, 0))] + cast_specs,
        out_shape=[jax.ShapeDtypeStruct((L, D), _F32)] + cast_shapes,
        compiler_params=pltpu.CompilerParams(
            dimension_semantics=("arbitrary",), vmem_limit_bytes=VMEM_LIMIT),
        name="ffn_down",
    )(act, wd, x, *to_cast)
    return out[0], out[1:]


def _proj_body(x_ref, gain_ref, w_ref, pos_ref, invf_ref, qg_ref, kg_ref, pn_ref,
               q_ref, k_ref, v_ref, u_ref, us_ref):
    tm = x_ref.shape[0]
    h = _rms(x_ref[...], gain_ref[...]).astype(_BF16)
    group = 2 * MXU_DIM

    def project(g):
        return jnp.dot(h, w_ref[:, g * group:(g + 1) * group], preferred_element_type=_F32)

    half = HEAD_DIM // 2
    ang = pos_ref[...] * invf_ref[...]
    lane_q = lax.broadcasted_iota(jnp.int32, ang.shape, 1) // half

    def spread(table):
        parts = []
        for qtr in range(LANES // half):
            m = jnp.where(lane_q == qtr, table, 0.0)
            parts.append(m + pltpu.roll(m, half, 1) + pltpu.roll(m, 2 * half, 1) + pltpu.roll(m, 3 * half, 1))
        return jnp.concatenate(parts, axis=0)

    cos = spread(jnp.cos(ang))
    sin = spread(jnp.sin(ang))
    lane = lax.broadcasted_iota(jnp.int32, (tm, LANES), 1)
    first_half = (lane & (HEAD_DIM // 2)) == 0
    low_head = lane < HEAD_DIM
    sin_signed = jnp.where(first_half, -sin, sin)

    def norm_rotary(x4, gain):
        ms4 = jnp.dot((x4 * x4).astype(_BF16), pn_ref[...], preferred_element_type=_F32)
        out = []
        for part in range(2):
            lanes = slice(part * LANES, (part + 1) * LANES)
            y = x4[:, lanes] * lax.rsqrt(ms4[:, lanes] + EPS) * gain
            swapped = jnp.where(first_half, pltpu.roll(y, LANES - HEAD_DIM // 2, 1),
                                pltpu.roll(y, HEAD_DIM // 2, 1))
            out.append(y * cos + swapped * sin_signed)
        return out

    def dup_heads(xc):
        r = pltpu.roll(xc, HEAD_DIM, 1)
        return jnp.where(low_head, xc, r), jnp.where(low_head, r, xc)

    scale = 1.0 / math.sqrt(HEAD_DIM)
    assert KV_WIDTH == MXU_DIM and 2 * KV_WIDTH == group
    for g in range(ATTN_WIDTH // group):
        pg = project(g)
        for c in range(group // MXU_DIM):
            for part, qc in enumerate(norm_rotary(pg[:, c * MXU_DIM:(c + 1) * MXU_DIM], qg_ref[...])):
                at = g * group + c * MXU_DIM + part * LANES
                q_ref[:, at:at + LANES] = (qc * scale).astype(_BF16)
    pg = project(ATTN_WIDTH // group)
    for part, kc in enumerate(norm_rotary(pg[:, :KV_WIDTH], kg_ref[...])):
        ka, kb = dup_heads(kc)
        k_ref[:, 2 * part * LANES:(2 * part + 1) * LANES] = ka.astype(_BF16)
        k_ref[:, (2 * part + 1) * LANES:(2 * part + 2) * LANES] = kb.astype(_BF16)
    for part in range(KV_WIDTH // LANES):
        va, vb = dup_heads(pg[:, KV_WIDTH + part * LANES:KV_WIDTH + (part + 1) * LANES])
        v_ref[:, 2 * part * LANES:(2 * part + 1) * LANES] = va.astype(_BF16)
        v_ref[:, (2 * part + 1) * LANES:(2 * part + 2) * LANES] = vb.astype(_BF16)
    first_u = (ATTN_WIDTH + 2 * KV_WIDTH) // group
    per_group = group // LANES
    for g in range(SSM_WIDTH // group):
        pg = project(first_u + g)
        for s in range(per_group):
            us_ref[g * per_group + s] = pg[:, s * LANES:(s + 1) * LANES]
        for t in range(SSM_CHUNK):
            for s in range(g * per_group, (g + 1) * per_group):
                rows = us_ref[s, pl.ds(t, tm // SSM_CHUNK, stride=SSM_CHUNK), :]
                u_ref[t, :, s * LANES:(s + 1) * LANES] = rows.astype(_BF16)


def _proj(x, gain, w_in, pos, invf, q_gain, k_gain, *, tm=512):
    L, D = x.shape
    C = w_in.shape[1]
    quarters = LANES // (HEAD_DIM // 2)
    pos_f = pos.astype(_F32).reshape(L // tm, quarters, tm // quarters).transpose(0, 2, 1)
    pos_f = jnp.repeat(pos_f, HEAD_DIM // 2, axis=2)
    head_of_lane = jnp.arange(MXU_DIM) // HEAD_DIM
    pn = jnp.where(head_of_lane[:, None] == head_of_lane[None, :], 1.0 / HEAD_DIM, 0.0).astype(_BF16)
    qg = jnp.tile(q_gain, LANES // HEAD_DIM).reshape(1, LANES)
    kg = jnp.tile(k_gain, LANES // HEAD_DIM).reshape(1, LANES)
    row = lambda i: (i, 0)
    fixed = lambda i: (0, 0)
    return pl.pallas_call(
        _proj_body,
        grid=(L // tm,),
        in_specs=[
            pl.BlockSpec((tm, D), row),
            pl.BlockSpec((1, D), fixed),
            pl.BlockSpec((D, C), fixed),
            pl.BlockSpec((None, tm // quarters, LANES), lambda i: (i, 0, 0)),
            pl.BlockSpec((1, LANES), fixed),
            pl.BlockSpec((1, LANES), fixed),
            pl.BlockSpec((1, LANES), fixed),
            pl.BlockSpec((MXU_DIM, MXU_DIM), fixed),
        ],
        out_specs=[
            pl.BlockSpec((tm, ATTN_WIDTH), row),
            pl.BlockSpec((tm, 2 * KV_WIDTH), row),
            pl.BlockSpec((tm, 2 * KV_WIDTH), row),
            pl.BlockSpec((SSM_CHUNK, tm // SSM_CHUNK, SSM_WIDTH), lambda i: (0, i, 0)),
        ],
        out_shape=[
            jax.ShapeDtypeStruct((L, ATTN_WIDTH), _BF16),
            jax.ShapeDtypeStruct((L, 2 * KV_WIDTH), _BF16),
            jax.ShapeDtypeStruct((L, 2 * KV_WIDTH), _BF16),
            jax.ShapeDtypeStruct((SSM_CHUNK, L // SSM_CHUNK, SSM_WIDTH), _BF16),
        ],
        scratch_shapes=[pltpu.VMEM((SSM_WIDTH // LANES, tm, LANES), _F32)],
        compiler_params=pltpu.CompilerParams(
            dimension_semantics=("arbitrary",), vmem_limit_bytes=VMEM_LIMIT),
        name="proj",
    )(x, gain.reshape(1, D), w_in, pos_f, invf, qg, kg, pn)


def _attn_body(n_cast, sink_ref, q_ref, kc_ref, vc_ref, kp_ref, vp_ref, *refs):
    o_ref = refs[n_cast]
    for src, dst in zip(refs[:n_cast], refs[n_cast + 1:]):
        dst[...] = src[...].astype(_BF16)
    tq = q_ref.shape[0]
    kj = lax.broadcasted_iota(jnp.int32, (BLOCK, BLOCK), 0)
    qi = lax.broadcasted_iota(jnp.int32, (BLOCK, BLOCK), 1)
    from_prev = kj > qi
    has_prev = pl.program_id(0) > 0
    low_head = lax.broadcasted_iota(jnp.int32, (2 * BLOCK, LANES), 1) < HEAD_DIM
    nt = (((1,), (1,)), ((), ()))
    tn = (((0,), (0,)), ((), ()))

    for b in range(tq // BLOCK):
        rows = slice(b * BLOCK, (b + 1) * BLOCK)
        for hk in range(N_KV_HEADS):
            cols = slice(hk * LANES, (hk + 1) * LANES)
            if b == 0:
                kd = jnp.concatenate([kp_ref[:, cols], kc_ref[0:BLOCK, cols]], axis=0)
                vd = jnp.concatenate([vp_ref[:, cols], vc_ref[0:BLOCK, cols]], axis=0)
            else:
                kd = kc_ref[(b - 1) * BLOCK:(b + 1) * BLOCK, cols]
                vd = vc_ref[(b - 1) * BLOCK:(b + 1) * BLOCK, cols]
            zero = jnp.zeros_like(kd)
            k_half = (jnp.where(low_head, kd, zero), jnp.where(low_head, zero, kd))
            v_half = (jnp.where(low_head, vd, zero), jnp.where(low_head, zero, vd))
            for pp in range(2):
                pair = hk * 2 + pp
                qp = q_ref[rows, pair * LANES:(pair + 1) * LANES]
                acc = None
                for half in range(2):
                    sink = sink_ref[pair * 2 + half]
                    s2 = lax.dot_general(k_half[half], qp, nt, preferred_element_type=_F32)
                    s_prev = s2[:BLOCK]
                    if b == 0:
                        s_prev = jnp.where(has_prev, s_prev, -jnp.inf)
                    s = jnp.where(from_prev, s_prev, s2[BLOCK:])
                    m = jnp.maximum(jnp.max(s, axis=0, keepdims=True), sink)
                    p = jnp.exp(s - m)
                    den = jnp.sum(p, axis=0, keepdims=True) + jnp.exp(sink - m)
                    pb = p.astype(_BF16)
                    pz = jnp.zeros_like(pb)
                    p2 = jnp.concatenate([jnp.where(from_prev, pb, pz), jnp.where(from_prev, pz, pb)], axis=0)
                    o = lax.dot_general(v_half[half], p2, tn, preferred_element_type=_F32) * (1.0 / den)
                    acc = o if acc is None else acc + o
                o_ref[rows, pair * LANES:(pair + 1) * LANES] = jnp.transpose(acc).astype(_BF16)


def _attn(q, k2, v2, sinks, to_cast=(), *, tq=512):
    L = q.shape[0]
    per = tq // BLOCK
    row = lambda i: (i, 0)
    prev = lambda i: (jnp.maximum(i * per - 1, 0), 0)
    cast_specs, cast_shapes = _row_cast_specs(to_cast, L // tq)
    out = pl.pallas_call(
        functools.partial(_attn_body, len(to_cast)),
        grid=(L // tq,),
        in_specs=[
            pl.BlockSpec(memory_space=pltpu.SMEM),
            pl.BlockSpec((tq, ATTN_WIDTH), row),
            pl.BlockSpec((tq, 2 * KV_WIDTH), row),
            pl.BlockSpec((tq, 2 * KV_WIDTH), row),
            pl.BlockSpec((BLOCK, 2 * KV_WIDTH), prev),
            pl.BlockSpec((BLOCK, 2 * KV_WIDTH), prev),
        ] + cast_specs,
        out_specs=[pl.BlockSpec((tq, ATTN_WIDTH), row)] + cast_specs,
        out_shape=[jax.ShapeDtypeStruct((L, ATTN_WIDTH), _BF16)] + cast_shapes,
        compiler_params=pltpu.CompilerParams(
            dimension_semantics=("arbitrary",), vmem_limit_bytes=VMEM_LIMIT),
        name="attn",
    )(sinks, q, k2, v2, k2, v2, *to_cast)
    return out[0], out[1:]


def _ssm_body(u_ref, wst_ref, kw_ref, cp_ref, wg_ref, bg_ref, a_ref, aseg_ref, o_ref,
              lhs_ref, toep_ref, s_ref, xb_ref):
    T = SSM_CHUNK
    n_c = u_ref.shape[1]
    seg = n_c // SSM_SEGMENTS
    pitch = s_ref.shape[1] // SSM_SEGMENTS
    n_state_slabs = SLAB_STATES // LANES
    pair_w = 2 * LANES
    out_w = 2 * MXU_DIM

    for t in range(T):
        lhs_ref[:, t * LANES:(t + 1) * LANES] = u_ref[t]

    toep_ref[...] = jnp.zeros(toep_ref.shape, _BF16)
    for t in range(T):
        for tp in range(t, T):
            toep_ref[t * LANES:(t + 1) * LANES, tp * LANES:(tp + 1) * LANES] = (
                kw_ref[:, (tp - t) * LANES:(tp - t + 1) * LANES])

    per_dot = out_w // LANES
    for nb in range(2 * n_state_slabs // per_dot):
        res = jnp.dot(lhs_ref[...], wst_ref[:, nb * out_w:(nb + 1) * out_w], preferred_element_type=_F32)
        for part in range(per_dot):
            for j in range(SSM_SEGMENTS):
                s_ref[per_dot * nb + part, j * pitch:j * pitch + seg, :] = (
                    res[j * seg:(j + 1) * seg, part * LANES:(part + 1) * LANES])

    shape = (SSM_SEGMENTS, LANES)
    a_re = [jnp.broadcast_to(a_ref[0:1, k * LANES:(k + 1) * LANES], shape) for k in range(n_state_slabs)]
    a_im = [jnp.broadcast_to(a_ref[1:2, k * LANES:(k + 1) * LANES], shape) for k in range(n_state_slabs)]
    g_re = [jnp.broadcast_to(aseg_ref[0:1, k * LANES:(k + 1) * LANES], shape) for k in range(n_state_slabs)]
    g_im = [jnp.broadcast_to(aseg_ref[1:2, k * LANES:(k + 1) * LANES], shape) for k in range(n_state_slabs)]

    def seg_rows(w):
        return pl.ds(w, SSM_SEGMENTS, stride=pitch)

    def advance(w, carry, store):
        out = []
        for k in range(n_state_slabs):
            z_re, z_im = carry[2 * k], carry[2 * k + 1]
            s_re = s_ref[k, seg_rows(w), :]
            s_im = s_ref[n_state_slabs + k, seg_rows(w), :]
            if store:
                s_ref[k, seg_rows(w), :] = z_re
                s_ref[n_state_slabs + k, seg_rows(w), :] = z_im
            out.append(a_re[k] * z_re - a_im[k] * z_im + s_re)
            out.append(a_re[k] * z_im + a_im[k] * z_re + s_im)
        return tuple(out)

    zero = jnp.zeros(shape, _F32)
    ends = lax.fori_loop(0, seg, lambda w, c: advance(w, c, False), (zero,) * (2 * n_state_slabs),
                         unroll=SCAN_UNROLL)

    segidx = lax.broadcasted_iota(jnp.int32, shape, 0)
    init = []
    for k in range(n_state_slabs):
        i_re, i_im = zero, zero
        f_re, f_im = ends[2 * k], ends[2 * k + 1]
        for j in range(SSM_SEGMENTS - 1):
            c_re = g_re[k] * i_re - g_im[k] * i_im + f_re
            c_im = g_re[k] * i_im + g_im[k] * i_re + f_im
            i_re = jnp.where(segidx == j + 1, pltpu.roll(c_re, 1, 0), i_re)
            i_im = jnp.where(segidx == j + 1, pltpu.roll(c_im, 1, 0), i_im)
        init += [i_re, i_im]

    lax.fori_loop(0, seg, lambda w, c: advance(w, c, True), tuple(init), unroll=SCAN_UNROLL)

    for col in range(2 * n_state_slabs):
        for j in range(SSM_SEGMENTS):
            xb_ref[j * seg:(j + 1) * seg, col * LANES:(col + 1) * LANES] = (
                s_ref[col, j * pitch:j * pitch + seg, :].astype(_BF16))

    steps = out_w // LANES
    for i in range(T // steps):
        kk = (i + 1) * out_w
        cols = slice(i * out_w, (i + 1) * out_w)
        y = (jnp.dot(lhs_ref[:, :kk], toep_ref[:kk, cols], preferred_element_type=_F32)
             + jnp.dot(xb_ref[...], cp_ref[:, cols], preferred_element_type=_F32))
        z = jax.nn.gelu(y, approximate=True).astype(_BF16)
        for pp in range(steps // 2):
            gt = jnp.dot(z[:, pp * pair_w:(pp + 1) * pair_w], wg_ref[...],
                         preferred_element_type=_F32) + bg_ref[...]
            out = gt[:, :pair_w] * jax.nn.sigmoid(gt[:, pair_w:])
            t0 = i * steps + 2 * pp
            o_ref[t0] = out[:, :LANES].astype(_BF16)
            o_ref[t0 + 1] = out[:, LANES:].astype(_BF16)


def _complex_power(re, im, n):
    out_re, out_im = None, None
    while n:
        if n & 1:
            if out_re is None:
                out_re, out_im = re, im
            else:
                out_re, out_im = out_re * re - out_im * im, out_re * im + out_im * re
        n >>= 1
        if n:
            re, im = re * re - im * im, 2.0 * re * im
    return out_re, out_im


def _ssm_prep_body(seg, rows_ref, bt_ref, ct_ref, d_ref, wt_ref, bglu_ref,
                   wst_ref, kw_ref, cp_ref, wg_ref, bg_ref, ach_ref, aseg_ref):
    T, H = SSM_CHUNK, SSM_GROUP
    a_re, a_im = rows_ref[0:1, :], rows_ref[1:2, :]
    dt = jnp.exp(rows_ref[2:3, :])
    mag = jnp.exp(a_re * dt)
    ab_re = mag * jnp.cos(a_im * dt)
    ab_im = mag * jnp.sin(a_im * dt)
    inv_den = 1.0 / (a_re * a_re + a_im * a_im)
    nr = ab_re - 1.0
    coef_re = (nr * a_re + ab_im * a_im) * inv_den
    coef_im = (ab_im * a_re - nr * a_im) * inv_den

    def block_diag(t16, shape):
        row_g = lax.broadcasted_iota(jnp.int32, shape, 0) // H
        col_g = lax.broadcasted_iota(jnp.int32, shape, 1) // (shape[1] // SLAB_GROUPS)
        return jnp.where(row_g == col_g, jnp.concatenate([t16] * SLAB_GROUPS, axis=0), 0.0)

    cs = (LANES, SLAB_STATES)
    b_re, b_im = block_diag(bt_ref[0], cs), block_diag(bt_ref[1], cs)
    bb_re = coef_re * b_re - coef_im * b_im
    bb_im = coef_re * b_im + coef_im * b_re
    ct_re, ct_im = block_diag(ct_ref[0], cs), block_diag(ct_ref[1], cs)

    pw = [(jnp.ones_like(ab_re), jnp.zeros_like(ab_im))]
    for _ in range(T):
        pr, pi = pw[-1]
        pw.append((pr * ab_re - pi * ab_im, pr * ab_im + pi * ab_re))

    for t in range(T):
        pr, pi = pw[T - 1 - t]
        wst_ref[t * LANES:(t + 1) * LANES, :SLAB_STATES] = (bb_re * pr - bb_im * pi).astype(_BF16)
        wst_ref[t * LANES:(t + 1) * LANES, SLAB_STATES:] = (bb_re * pi + bb_im * pr).astype(_BF16)

    lags = []
    for k in range(T + 1):
        if k < T:
            lags.append(jnp.concatenate([ct_re, ct_im], axis=1))
        if k > 0:
            cp_ref[:SLAB_STATES, (k - 1) * LANES:k * LANES] = jnp.transpose(ct_re).astype(_BF16)
            cp_ref[SLAB_STATES:, (k - 1) * LANES:k * LANES] = jnp.transpose(-ct_im).astype(_BF16)
        ct_re, ct_im = ct_re * ab_re - ct_im * ab_im, ct_re * ab_im + ct_im * ab_re

    def split(a):
        hi = a.astype(_BF16)
        return hi, (a - hi.astype(_F32)).astype(_BF16)

    nt = (((1,), (1,)), ((), ()))
    a_hi, a_lo = split(jnp.concatenate(lags, axis=0))
    b_hi, b_lo = split(jnp.concatenate([bb_re, -bb_im], axis=1))
    kern_t = (lax.dot_general(a_hi, b_hi, nt, preferred_element_type=_F32)
              + lax.dot_general(a_hi, b_lo, nt, preferred_element_type=_F32)
              + lax.dot_general(a_lo, b_hi, nt, preferred_element_type=_F32))
    kern = jnp.transpose(kern_t)
    eye = (lax.broadcasted_iota(jnp.int32, (LANES, LANES), 0)
           == lax.broadcasted_iota(jnp.int32, (LANES, LANES), 1))
    kw_ref[:, :LANES] = (kern[:, :LANES] + jnp.where(eye, d_ref[...], 0.0)).astype(_BF16)
    kw_ref[:, LANES:] = kern[:, LANES:].astype(_BF16)

    cc = (LANES, LANES)
    w_lin = jnp.transpose(block_diag(wt_ref[0], cc)).astype(_BF16)
    w_gate = jnp.transpose(block_diag(wt_ref[1], cc)).astype(_BF16)
    wg_ref[...] = jnp.zeros(wg_ref.shape, _BF16)
    for t in range(2):
        wg_ref[t * LANES:(t + 1) * LANES, t * LANES:(t + 1) * LANES] = w_lin
        wg_ref[t * LANES:(t + 1) * LANES, (2 + t) * LANES:(3 + t) * LANES] = w_gate
    bg_ref[...] = jnp.concatenate([bglu_ref[0:1, :], bglu_ref[0:1, :], bglu_ref[1:2, :], bglu_ref[1:2, :]],
                                  axis=1)

    ach_ref[0:1, :], ach_ref[1:2, :] = pw[T]
    aseg_ref[0:1, :], aseg_ref[1:2, :] = _complex_power(pw[T][0], pw[T][1], seg)


def _ssm_weights(log_dt, a_re, a_im, b_re, b_im, c_re, c_im, d_skip, w_glu, b_glu, seg):
    P, H, T = SSM_STATE, SSM_GROUP, SSM_CHUNK
    SG, NS = SLAB_GROUPS, SSM_GROUPS // SLAB_GROUPS
    rows = jnp.stack([a_re.reshape(NS, SG * P), a_im.reshape(NS, SG * P),
                      jnp.repeat(log_dt, P).reshape(NS, SG * P)], axis=1)

    def per_group_rows(w, lead):
        t = w.reshape((NS, SG) + w.shape[1:])
        t = jnp.moveaxis(t, 2 + lead, 1)
        return t.reshape(NS, t.shape[1], -1)

    bt = jnp.stack([per_group_rows(b_re, 1), per_group_rows(b_im, 1)], axis=1)
    ct = jnp.stack([per_group_rows(c_re, 0), per_group_rows(c_im, 0)], axis=1)
    wt = jnp.stack([per_group_rows(w_glu[..., :H], 1), per_group_rows(w_glu[..., H:], 1)], axis=1)
    bglu = jnp.stack([b_glu[:, :H].reshape(NS, SG * H), b_glu[:, H:].reshape(NS, SG * H)], axis=1)
    d = d_skip.reshape(NS, 1, SG * H)

    blk3 = lambda i: (i, 0, 0)
    blk4 = lambda i: (i, 0, 0, 0)
    return pl.pallas_call(
        functools.partial(_ssm_prep_body, seg),
        grid=(NS,),
        in_specs=[
            pl.BlockSpec((None, 3, SG * P), blk3),
            pl.BlockSpec((None, 2, H, SG * P), blk4),
            pl.BlockSpec((None, 2, H, SG * P), blk4),
            pl.BlockSpec((None, 1, SG * H), blk3),
            pl.BlockSpec((None, 2, H, SG * H), blk4),
            pl.BlockSpec((None, 2, SG * H), blk3),
        ],
        out_specs=[
            pl.BlockSpec((None, T * LANES, 2 * SLAB_STATES), blk3),
            pl.BlockSpec((None, LANES, T * LANES), blk3),
            pl.BlockSpec((None, 2 * SLAB_STATES, T * LANES), blk3),
            pl.BlockSpec((None, 2 * LANES, 4 * LANES), blk3),
            pl.BlockSpec((None, 1, 4 * LANES), blk3),
            pl.BlockSpec((None, 2, SLAB_STATES), blk3),
            pl.BlockSpec((None, 2, SLAB_STATES), blk3),
        ],
        out_shape=[
            jax.ShapeDtypeStruct((NS, T * LANES, 2 * SLAB_STATES), _BF16),
            jax.ShapeDtypeStruct((NS, LANES, T * LANES), _BF16),
            jax.ShapeDtypeStruct((NS, 2 * SLAB_STATES, T * LANES), _BF16),
            jax.ShapeDtypeStruct((NS, 2 * LANES, 4 * LANES), _BF16),
            jax.ShapeDtypeStruct((NS, 1, 4 * LANES), _F32),
            jax.ShapeDtypeStruct((NS, 2, SLAB_STATES), _F32),
            jax.ShapeDtypeStruct((NS, 2, SLAB_STATES), _F32),
        ],
        compiler_params=pltpu.CompilerParams(
            dimension_semantics=("arbitrary",), vmem_limit_bytes=VMEM_LIMIT),
        name="ssm_prep",
    )(rows, bt, ct, d, wt, bglu)


def _ssm(u_t, weights):
    T, n_c, W = u_t.shape
    seg = n_c // SSM_SEGMENTS
    pitch = seg + SUBLANES
    ns = W // LANES
    wst, kw, cp, wg, bg, a_chunk, a_seg = weights
    slab = lambda i: (0, 0, i)
    blk = lambda i: (i, 0, 0)
    return pl.pallas_call(
        _ssm_body,
        grid=(ns,),
        in_specs=[
            pl.BlockSpec((T, n_c, LANES), slab),
            pl.BlockSpec((None, T * LANES, 2 * SLAB_STATES), blk),
            pl.BlockSpec((None, LANES, T * LANES), blk),
            pl.BlockSpec((None, 2 * SLAB_STATES, T * LANES), blk),
            pl.BlockSpec((None, 2 * LANES, 4 * LANES), blk),
            pl.BlockSpec((None, 1, 4 * LANES), blk),
            pl.BlockSpec((None, 2, SLAB_STATES), blk),
            pl.BlockSpec((None, 2, SLAB_STATES), blk),
        ],
        out_specs=pl.BlockSpec((T, n_c, LANES), slab),
        out_shape=jax.ShapeDtypeStruct((T, n_c, W), _BF16),
        scratch_shapes=[
            pltpu.VMEM((n_c, T * LANES), _BF16),
            pltpu.VMEM((T * LANES, T * LANES), _BF16),
            pltpu.VMEM((2 * SLAB_STATES // LANES, SSM_SEGMENTS * pitch, LANES), _F32),
            pltpu.VMEM((n_c, 2 * SLAB_STATES), _BF16),
        ],
        compiler_params=pltpu.CompilerParams(
            dimension_semantics=("arbitrary",), vmem_limit_bytes=VMEM_LIMIT),
        name="ssm",
    )(u_t, wst, kw, cp, wg, bg, a_chunk, a_seg)


def _outproj_body(n_cast, a_ref, s_ref, x_ref, ag_ref, sg_ref, w_ref, *refs):
    o_ref, sn_ref = refs[n_cast], refs[-1]
    for src, dst in zip(refs[:n_cast], refs[n_cast + 1:-1]):
        dst[...] = src[...].astype(_BF16)
    tm = x_ref.shape[0]
    an = _rms(a_ref[...].astype(_F32), ag_ref[...]).astype(_BF16)
    for t in range(SSM_CHUNK):
        sn_t = _rms(s_ref[t].astype(_F32), sg_ref[...])
        for s in range(SSM_WIDTH // LANES):
            sn_ref[s, pl.ds(t, tm // SSM_CHUNK, stride=SSM_CHUNK), :] = sn_t[:, s * LANES:(s + 1) * LANES]
    sn = jnp.concatenate([sn_ref[s] for s in range(SSM_WIDTH // LANES)], axis=1).astype(_BF16)
    mixed = jnp.concatenate([an, sn], axis=1)
    o_ref[...] = x_ref[...] + jnp.dot(mixed, w_ref[...], preferred_element_type=_F32)


def _outproj(attn, ssm_t, x, a_gain, s_gain, w_out, to_cast=(), *, tm=512):
    L, D = x.shape
    row = lambda i: (i, 0)
    fixed = lambda i: (0, 0)
    cast_specs, cast_shapes = _row_cast_specs(to_cast, L // tm)
    out = pl.pallas_call(
        functools.partial(_outproj_body, len(to_cast)),
        grid=(L // tm,),
        in_specs=[
            pl.BlockSpec((tm, ATTN_WIDTH), row),
            pl.BlockSpec((SSM_CHUNK, tm // SSM_CHUNK, SSM_WIDTH), lambda i: (0, i, 0)),
            pl.BlockSpec((tm, D), row),
            pl.BlockSpec((1, ATTN_WIDTH), fixed),
            pl.BlockSpec((1, SSM_WIDTH), fixed),
            pl.BlockSpec((ATTN_WIDTH + SSM_WIDTH, D), fixed),
        ] + cast_specs,
        out_specs=[pl.BlockSpec((tm, D), row)] + cast_specs,
        out_shape=[jax.ShapeDtypeStruct((L, D), _F32)] + cast_shapes,
        scratch_shapes=[pltpu.VMEM((SSM_WIDTH // LANES, tm, LANES), _F32)],
        compiler_params=pltpu.CompilerParams(
            dimension_semantics=("arbitrary",), vmem_limit_bytes=VMEM_LIMIT),
        name="outproj",
    )(attn, ssm_t, x, a_gain.reshape(1, -1), s_gain.reshape(1, -1), w_out, *to_cast)
    return out[0], out[1:]


def _layer(x, pos, invf, p):
    L = x.shape[0]
    seg = L // (SSM_SEGMENTS * SSM_CHUNK)
    act, (w_down1,) = _ffn_up(
        x, p['ffn1_norm'], p['ffn1_w_gate'].astype(_BF16), p['ffn1_w_up'].astype(_BF16),
        [p['ffn1_w_down']])
    x, (w_in, w_out) = _ffn_down(act, w_down1, x, [p['w_in'], p['w_out']])
    q, k2, v2, u_t = _proj(x, p['mix_norm'], w_in, pos, invf, p['q_norm'], p['k_norm'])
    attn, (w_gate2, w_up2) = _attn(q, k2, v2, p['attn_sinks'], [p['ffn2_w_gate'], p['ffn2_w_up']])
    weights = _ssm_weights(p['ssm_log_dt'], p['ssm_a_re'], p['ssm_a_im'], p['ssm_b_re'], p['ssm_b_im'],
                           p['ssm_c_re'], p['ssm_c_im'], p['ssm_d'], p['ssm_w_glu'], p['ssm_b_glu'], seg)
    ssm_t = _ssm(u_t, weights)
    x, (w_down2,) = _outproj(attn, ssm_t, x, p['attn_out_norm'], p['ssm_out_norm'], w_out,
                             [p['ffn2_w_down']])
    act, _ = _ffn_up(x, p['ffn2_norm'], w_gate2, w_up2)
    return _ffn_down(act, w_down2, x)[0]


def kernel(x, positions, ffn1_norm, ffn1_w_gate, ffn1_w_up, ffn1_w_down, mix_norm, w_in, q_norm, k_norm,
           attn_sinks, ssm_log_dt, ssm_a_re, ssm_a_im, ssm_b_re, ssm_b_im, ssm_c_re, ssm_c_im, ssm_d,
           ssm_w_glu, ssm_b_glu, attn_out_norm, ssm_out_norm, w_out, ffn2_norm, ffn2_w_gate, ffn2_w_up,
           ffn2_w_down):
    params = dict(
        ffn1_norm=ffn1_norm, ffn1_w_gate=ffn1_w_gate, ffn1_w_up=ffn1_w_up, ffn1_w_down=ffn1_w_down,
        mix_norm=mix_norm, w_in=w_in, q_norm=q_norm, k_norm=k_norm, attn_sinks=attn_sinks,
        ssm_log_dt=ssm_log_dt, ssm_a_re=ssm_a_re, ssm_a_im=ssm_a_im, ssm_b_re=ssm_b_re, ssm_b_im=ssm_b_im,
        ssm_c_re=ssm_c_re, ssm_c_im=ssm_c_im, ssm_d=ssm_d, ssm_w_glu=ssm_w_glu, ssm_b_glu=ssm_b_glu,
        attn_out_norm=attn_out_norm, ssm_out_norm=ssm_out_norm, w_out=w_out,
        ffn2_norm=ffn2_norm, ffn2_w_gate=ffn2_w_gate, ffn2_w_up=ffn2_w_up, ffn2_w_down=ffn2_w_down)
    depth = ffn1_norm.shape[0]
    half = HEAD_DIM // 2
    inv_freq = ROPE_THETA ** (-jnp.arange(half, dtype=_F32) * 2.0 / HEAD_DIM)
    invf = jnp.tile(inv_freq, LANES // half).reshape(1, LANES)
    outs = []
    for b in range(x.shape[0]):
        xb = x[b]
        for i in range(depth):
            xb = _layer(xb, positions[b], invf, {name: val[i] for name, val in params.items()})
        outs.append(xb)
    return jnp.stack(outs, axis=0)
```

```python
import functools
import math

import jax
import jax.numpy as jnp
from jax import lax
from jax.experimental import pallas as pl
from jax.experimental.pallas import tpu as pltpu

HEAD_DIM = 64
N_Q_HEADS = 16
N_KV_HEADS = 4
ATTN_WIDTH = N_Q_HEADS * HEAD_DIM
KV_WIDTH = N_KV_HEADS * HEAD_DIM
BLOCK = 128
ROPE_THETA = 10000.0
SSM_GROUP = 16
SSM_GROUPS = 64
SSM_STATE = 64
SSM_WIDTH = SSM_GROUP * SSM_GROUPS
FFN_RESIDUAL = 0.5
EPS = 1e-6

LANES = 128
SUBLANES = 8
MXU_DIM = 256
SSM_CHUNK = 8
SSM_SEGMENTS = SUBLANES
SLAB_GROUPS = LANES // SSM_GROUP
SLAB_STATES = SLAB_GROUPS * SSM_STATE
SCAN_UNROLL = 4
FFN_TF = 2 * MXU_DIM
VMEM_LIMIT = 56 * 1024 * 1024

_BF16 = jnp.bfloat16
_F32 = jnp.float32


def _rms(x, gain):
    ms = jnp.mean(x * x, axis=-1, keepdims=True)
    return x * lax.rsqrt(ms + EPS) * gain


def _ffn_up_body(n_cast, x_ref, gain_ref, wg_ref, wu_ref, *refs):
    src_refs, a_ref, dst_refs, h_ref = refs[:n_cast], refs[n_cast], refs[n_cast + 1:-1], refs[-1]
    first = pl.program_id(1) == 0

    @pl.when(first)
    def _():
        h_ref[...] = _rms(x_ref[...], gain_ref[...]).astype(_BF16)

    h = h_ref[...]
    g = jnp.dot(h, wg_ref[...], preferred_element_type=_F32)
    u = jnp.dot(h, wu_ref[...], preferred_element_type=_F32)
    a_ref[...] = (g * jax.nn.sigmoid(g) * u).astype(_BF16)

    for src, dst in zip(src_refs, dst_refs):
        dst[...] = src[...].astype(_BF16)


def _ffn_down_body(n_cast, a_ref, wd_ref, x_ref, *refs):
    o_ref = refs[n_cast]
    d = jnp.dot(a_ref[...], wd_ref[...], preferred_element_type=_F32)
    o_ref[...] = x_ref[...] + FFN_RESIDUAL * d
    for src, dst in zip(refs[:n_cast], refs[n_cast + 1:]):
        dst[...] = src[...].astype(_BF16)


def _row_cast_specs(arrays, n, col_block=None):
    pack = 2 * SUBLANES
    for w in arrays:
        assert w.shape[0] % (n * pack) == 0, w.shape
    in_specs = [pl.BlockSpec((w.shape[0] // n, w.shape[1]), lambda i: (i, 0)) for w in arrays]
    if col_block is None:
        return in_specs, in_specs, [jax.ShapeDtypeStruct(w.shape, _BF16) for w in arrays]
    out_specs = [pl.BlockSpec((w.shape[1] // col_block, w.shape[0] // n, col_block), lambda i: (0, i, 0))
                 for w in arrays]
    shapes = [jax.ShapeDtypeStruct((w.shape[1] // col_block, w.shape[0], col_block), _BF16) for w in arrays]
    return in_specs, out_specs, shapes


def _cast_block(src, dst):
    if len(dst.shape) == 2:
        dst[...] = src[...].astype(_BF16)
    else:
        for c in range(dst.shape[0]):
            dst[c] = src[:, c * dst.shape[2]:(c + 1) * dst.shape[2]].astype(_BF16)


def _step_cast_spec(shape, ni, nj):
    R, C = shape
    pack = 2 * SUBLANES
    if R % ni == 0 and (R // ni) % pack == 0 and C % nj == 0 and (C // nj) % LANES == 0:
        return pl.BlockSpec((R // ni, C // nj), lambda i, j: (i, j))
    assert R % (ni * nj) == 0 and (R // (ni * nj)) % pack == 0, shape
    return pl.BlockSpec((R // (ni * nj), C), lambda i, j: (i * nj + j, 0))


def _block_major_cast_body(*refs):
    n = len(refs) // 2
    for src, dst in zip(refs[:n], refs[n:]):
        dst[...] = src[...].astype(_BF16)


def _block_major_cast(weights):
    D, F = weights[0].shape
    return pl.pallas_call(
        _block_major_cast_body,
        grid=(F // FFN_TF,),
        in_specs=[pl.BlockSpec((D, FFN_TF), lambda j: (0, j)) for _ in weights],
        out_specs=[pl.BlockSpec((None, D, FFN_TF), lambda j: (j, 0, 0)) for _ in weights],
        out_shape=[jax.ShapeDtypeStruct((F // FFN_TF, D, FFN_TF), _BF16) for _ in weights],
        compiler_params=pltpu.CompilerParams(
            dimension_semantics=("arbitrary",), vmem_limit_bytes=VMEM_LIMIT),
        name="weight_cast",
    )(*weights)


def _ffn_up(x, gain, wg, wu, to_cast=(), *, tm=1024):
    L, D = x.shape
    nj, _, tf = wg.shape
    F = nj * tf
    tm = min(tm, L)
    ni = L // tm
    cast_specs = [_step_cast_spec(w.shape, ni, nj) for w in to_cast]
    out = pl.pallas_call(
        functools.partial(_ffn_up_body, len(to_cast)),
        grid=(ni, nj),
        in_specs=[
            pl.BlockSpec((tm, D), lambda i, j: (i, 0)),
            pl.BlockSpec((1, D), lambda i, j: (0, 0)),
            pl.BlockSpec((None, D, tf), lambda i, j: (j, 0, 0)),
            pl.BlockSpec((None, D, tf), lambda i, j: (j, 0, 0)),
        ] + cast_specs,
        out_specs=[pl.BlockSpec((tm, tf), lambda i, j: (i, j))] + cast_specs,
        out_shape=[jax.ShapeDtypeStruct((L, F), _BF16)]
        + [jax.ShapeDtypeStruct(w.shape, _BF16) for w in to_cast],
        scratch_shapes=[pltpu.VMEM((tm, D), _BF16)],
        compiler_params=pltpu.CompilerParams(
            dimension_semantics=("arbitrary", "arbitrary"), vmem_limit_bytes=VMEM_LIMIT),
        name="ffn_up",
    )(x, gain.reshape(1, D), wg, wu, *to_cast)
    return out[0], out[1:]


def _ffn_down(act, wd, x, to_cast=(), *, tm=512):
    L, D = x.shape
    F = act.shape[1]
    tm = min(tm, L)
    cast_specs, _, cast_shapes = _row_cast_specs(to_cast, L // tm)
    out = pl.pallas_call(
        functools.partial(_ffn_down_body, len(to_cast)),
        grid=(L // tm,),
        in_specs=[
            pl.BlockSpec((tm, F), lambda i: (i, 0)),
            pl.BlockSpec((F, D), lambda i: (0, 0), pipeline_mode=pl.Buffered(1)),
            pl.BlockSpec((tm, D), lambda i: (i, 0)),
        ] + cast_specs,
        out_specs=[pl.BlockSpec((tm, D), lambda i: (i, 0))] + cast_specs,
        out_shape=[jax.ShapeDtypeStruct((L, D), _F32)] + cast_shapes,
        compiler_params=pltpu.CompilerParams(
            dimension_semantics=("arbitrary",), vmem_limit_bytes=VMEM_LIMIT),
        name="ffn_down",
    )(act, wd, x, *to_cast)
    return out[0], out[1:]


def _proj_body(x_ref, gain_ref, w_ref, pos_ref, invf_ref, qg_ref, kg_ref, pn_ref,
               q_ref, k_ref, v_ref, u_ref, us_ref):
    tm = x_ref.shape[0]
    h = _rms(x_ref[...], gain_ref[...]).astype(_BF16)
    group = 2 * MXU_DIM

    def project(g):
        return jnp.dot(h, w_ref[:, g * group:(g + 1) * group], preferred_element_type=_F32)

    half = HEAD_DIM // 2
    ang = pos_ref[...] * invf_ref[...]
    lane_q = lax.broadcasted_iota(jnp.int32, ang.shape, 1) // half

    def spread(table):
        parts = []
        for qtr in range(LANES // half):
            m = jnp.where(lane_q == qtr, table, 0.0)
            parts.append(m + pltpu.roll(m, half, 1) + pltpu.roll(m, 2 * half, 1) + pltpu.roll(m, 3 * half, 1))
        return jnp.concatenate(parts, axis=0)

    cos = spread(jnp.cos(ang))
    sin = spread(jnp.sin(ang))
    lane = lax.broadcasted_iota(jnp.int32, (tm, LANES), 1)
    first_half = (lane & (HEAD_DIM // 2)) == 0
    low_head = lane < HEAD_DIM
    sin_signed = jnp.where(first_half, -sin, sin)

    def norm_rotary(x4, gain):
        ms4 = jnp.dot((x4 * x4).astype(_BF16), pn_ref[...], preferred_element_type=_F32)
        out = []
        for part in range(2):
            lanes = slice(part * LANES, (part + 1) * LANES)
            y = x4[:, lanes] * lax.rsqrt(ms4[:, lanes] + EPS) * gain
            swapped = jnp.where(first_half, pltpu.roll(y, LANES - HEAD_DIM // 2, 1),
                                pltpu.roll(y, HEAD_DIM // 2, 1))
            out.append(y * cos + swapped * sin_signed)
        return out

    def dup_heads(xc):
        r = pltpu.roll(xc, HEAD_DIM, 1)
        return jnp.where(low_head, xc, r), jnp.where(low_head, r, xc)

    scale = 1.0 / math.sqrt(HEAD_DIM)
    assert KV_WIDTH == MXU_DIM and 2 * KV_WIDTH == group
    for g in range(ATTN_WIDTH // group):
        pg = project(g)
        for c in range(group // MXU_DIM):
            for part, qc in enumerate(norm_rotary(pg[:, c * MXU_DIM:(c + 1) * MXU_DIM], qg_ref[...])):
                at = g * group + c * MXU_DIM + part * LANES
                q_ref[:, at:at + LANES] = (qc * scale).astype(_BF16)
    pg = project(ATTN_WIDTH // group)
    for part, kc in enumerate(norm_rotary(pg[:, :KV_WIDTH], kg_ref[...])):
        ka, kb = dup_heads(kc)
        k_ref[:, 2 * part * LANES:(2 * part + 1) * LANES] = ka.astype(_BF16)
        k_ref[:, (2 * part + 1) * LANES:(2 * part + 2) * LANES] = kb.astype(_BF16)
    for part in range(KV_WIDTH // LANES):
        va, vb = dup_heads(pg[:, KV_WIDTH + part * LANES:KV_WIDTH + (part + 1) * LANES])
        v_ref[:, 2 * part * LANES:(2 * part + 1) * LANES] = va.astype(_BF16)
        v_ref[:, (2 * part + 1) * LANES:(2 * part + 2) * LANES] = vb.astype(_BF16)
    first_u = (ATTN_WIDTH + 2 * KV_WIDTH) // group
    per_group = group // LANES
    for g in range(SSM_WIDTH // group):
        pg = project(first_u + g)
        for s in range(per_group):
            us_ref[g * per_group + s] = pg[:, s * LANES:(s + 1) * LANES]
        for t in range(SSM_CHUNK):
            for s in range(g * per_group, (g + 1) * per_group):
                rows = us_ref[s, pl.ds(t, tm // SSM_CHUNK, stride=SSM_CHUNK), :]
                u_ref[t, :, s * LANES:(s + 1) * LANES] = rows.astype(_BF16)


def _proj(x, gain, w_in, pos, invf, q_gain, k_gain, *, tm=512):
    L, D = x.shape
    C = w_in.shape[1]
    quarters = LANES // (HEAD_DIM // 2)
    pos_f = pos.astype(_F32).reshape(L // tm, quarters, tm // quarters).transpose(0, 2, 1)
    pos_f = jnp.repeat(pos_f, HEAD_DIM // 2, axis=2)
    head_of_lane = jnp.arange(MXU_DIM) // HEAD_DIM
    pn = jnp.where(head_of_lane[:, None] == head_of_lane[None, :], 1.0 / HEAD_DIM, 0.0).astype(_BF16)
    qg = jnp.tile(q_gain, LANES // HEAD_DIM).reshape(1, LANES)
    kg = jnp.tile(k_gain, LANES // HEAD_DIM).reshape(1, LANES)
    row = lambda i: (i, 0)
    fixed = lambda i: (0, 0)
    return pl.pallas_call(
        _proj_body,
        grid=(L // tm,),
        in_specs=[
            pl.BlockSpec((tm, D), row),
            pl.BlockSpec((1, D), fixed),
            pl.BlockSpec((D, C), fixed),
            pl.BlockSpec((None, tm // quarters, LANES), lambda i: (i, 0, 0)),
            pl.BlockSpec((1, LANES), fixed),
            pl.BlockSpec((1, LANES), fixed),
            pl.BlockSpec((1, LANES), fixed),
            pl.BlockSpec((MXU_DIM, MXU_DIM), fixed),
        ],
        out_specs=[
            pl.BlockSpec((tm, ATTN_WIDTH), row),
            pl.BlockSpec((tm, 2 * KV_WIDTH), row),
            pl.BlockSpec((tm, 2 * KV_WIDTH), row),
            pl.BlockSpec((SSM_CHUNK, tm // SSM_CHUNK, SSM_WIDTH), lambda i: (0, i, 0)),
        ],
        out_shape=[
            jax.ShapeDtypeStruct((L, ATTN_WIDTH), _BF16),
            jax.ShapeDtypeStruct((L, 2 * KV_WIDTH), _BF16),
            jax.ShapeDtypeStruct((L, 2 * KV_WIDTH), _BF16),
            jax.ShapeDtypeStruct((SSM_CHUNK, L // SSM_CHUNK, SSM_WIDTH), _BF16),
        ],
        scratch_shapes=[pltpu.VMEM((SSM_WIDTH // LANES, tm, LANES), _F32)],
        compiler_params=pltpu.CompilerParams(
            dimension_semantics=("arbitrary",), vmem_limit_bytes=VMEM_LIMIT),
        name="proj",
    )(x, gain.reshape(1, D), w_in, pos_f, invf, qg, kg, pn)


def _attn_body(n_cast, sink_ref, q_ref, kc_ref, vc_ref, kp_ref, vp_ref, *refs):
    o_ref = refs[n_cast]
    for src, dst in zip(refs[:n_cast], refs[n_cast + 1:]):
        _cast_block(src, dst)
    tq = q_ref.shape[0]
    kj = lax.broadcasted_iota(jnp.int32, (BLOCK, BLOCK), 0)
    qi = lax.broadcasted_iota(jnp.int32, (BLOCK, BLOCK), 1)
    from_prev = kj > qi
    has_prev = pl.program_id(0) > 0
    low_head = lax.broadcasted_iota(jnp.int32, (2 * BLOCK, LANES), 1) < HEAD_DIM
    nt = (((1,), (1,)), ((), ()))
    tn = (((0,), (0,)), ((), ()))

    for b in range(tq // BLOCK):
        rows = slice(b * BLOCK, (b + 1) * BLOCK)
        for hk in range(N_KV_HEADS):
            cols = slice(hk * LANES, (hk + 1) * LANES)
            if b == 0:
                kd = jnp.concatenate([kp_ref[:, cols], kc_ref[0:BLOCK, cols]], axis=0)
                vd = jnp.concatenate([vp_ref[:, cols], vc_ref[0:BLOCK, cols]], axis=0)
            else:
                kd = kc_ref[(b - 1) * BLOCK:(b + 1) * BLOCK, cols]
                vd = vc_ref[(b - 1) * BLOCK:(b + 1) * BLOCK, cols]
            zero = jnp.zeros_like(kd)
            k_half = (jnp.where(low_head, kd, zero), jnp.where(low_head, zero, kd))
            v_half = (jnp.where(low_head, vd, zero), jnp.where(low_head, zero, vd))
            for pp in range(2):
                pair = hk * 2 + pp
                qp = q_ref[rows, pair * LANES:(pair + 1) * LANES]
                acc = None
                for half in range(2):
                    sink = sink_ref[pair * 2 + half]
                    s2 = lax.dot_general(k_half[half], qp, nt, preferred_element_type=_F32)
                    s_prev = s2[:BLOCK]
                    if b == 0:
                        s_prev = jnp.where(has_prev, s_prev, -jnp.inf)
                    s = jnp.where(from_prev, s_prev, s2[BLOCK:])
                    m = jnp.maximum(jnp.max(s, axis=0, keepdims=True), sink)
                    p = jnp.exp(s - m)
                    den = jnp.sum(p, axis=0, keepdims=True) + jnp.exp(sink - m)
                    pb = p.astype(_BF16)
                    pz = jnp.zeros_like(pb)
                    p2 = jnp.concatenate([jnp.where(from_prev, pb, pz), jnp.where(from_prev, pz, pb)], axis=0)
                    o = lax.dot_general(v_half[half], p2, tn, preferred_element_type=_F32) * (1.0 / den)
                    acc = o if acc is None else acc + o
                o_ref[rows, pair * LANES:(pair + 1) * LANES] = jnp.transpose(acc).astype(_BF16)


def _attn(q, k2, v2, sinks, to_cast=(), cast_col_block=None, *, tq=512):
    L = q.shape[0]
    per = tq // BLOCK
    row = lambda i: (i, 0)
    prev = lambda i: (jnp.maximum(i * per - 1, 0), 0)
    cast_specs, cast_out_specs, cast_shapes = _row_cast_specs(to_cast, L // tq, cast_col_block)
    out = pl.pallas_call(
        functools.partial(_attn_body, len(to_cast)),
        grid=(L // tq,),
        in_specs=[
            pl.BlockSpec(memory_space=pltpu.SMEM),
            pl.BlockSpec((tq, ATTN_WIDTH), row),
            pl.BlockSpec((tq, 2 * KV_WIDTH), row),
            pl.BlockSpec((tq, 2 * KV_WIDTH), row),
            pl.BlockSpec((BLOCK, 2 * KV_WIDTH), prev),
            pl.BlockSpec((BLOCK, 2 * KV_WIDTH), prev),
        ] + cast_specs,
        out_specs=[pl.BlockSpec((tq, ATTN_WIDTH), row)] + cast_out_specs,
        out_shape=[jax.ShapeDtypeStruct((L, ATTN_WIDTH), _BF16)] + cast_shapes,
        compiler_params=pltpu.CompilerParams(
            dimension_semantics=("arbitrary",), vmem_limit_bytes=VMEM_LIMIT),
        name="attn",
    )(sinks, q, k2, v2, k2, v2, *to_cast)
    return out[0], out[1:]


def _ssm_body(u_ref, wst_ref, kw_ref, cp_ref, wg_ref, bg_ref, a_ref, aseg_ref, o_ref,
              lhs_ref, toep_ref, s_ref, xb_ref):
    T = SSM_CHUNK
    n_c = u_ref.shape[1]
    seg = n_c // SSM_SEGMENTS
    pitch = s_ref.shape[1] // SSM_SEGMENTS
    n_state_slabs = SLAB_STATES // LANES
    pair_w = 2 * LANES
    out_w = 2 * MXU_DIM

    for t in range(T):
        lhs_ref[:, t * LANES:(t + 1) * LANES] = u_ref[t]

    toep_ref[...] = jnp.zeros(toep_ref.shape, _BF16)
    for t in range(T):
        for tp in range(t, T):
            toep_ref[t * LANES:(t + 1) * LANES, tp * LANES:(tp + 1) * LANES] = (
                kw_ref[:, (tp - t) * LANES:(tp - t + 1) * LANES])

    per_dot = out_w // LANES
    for nb in range(2 * n_state_slabs // per_dot):
        res = jnp.dot(lhs_ref[...], wst_ref[:, nb * out_w:(nb + 1) * out_w], preferred_element_type=_F32)
        for part in range(per_dot):
            for j in range(SSM_SEGMENTS):
                s_ref[per_dot * nb + part, j * pitch:j * pitch + seg, :] = (
                    res[j * seg:(j + 1) * seg, part * LANES:(part + 1) * LANES])

    shape = (SSM_SEGMENTS, LANES)
    a_re = [jnp.broadcast_to(a_ref[0:1, k * LANES:(k + 1) * LANES], shape) for k in range(n_state_slabs)]
    a_im = [jnp.broadcast_to(a_ref[1:2, k * LANES:(k + 1) * LANES], shape) for k in range(n_state_slabs)]
    g_re = [jnp.broadcast_to(aseg_ref[0:1, k * LANES:(k + 1) * LANES], shape) for k in range(n_state_slabs)]
    g_im = [jnp.broadcast_to(aseg_ref[1:2, k * LANES:(k + 1) * LANES], shape) for k in range(n_state_slabs)]

    def seg_rows(w):
        return pl.ds(w, SSM_SEGMENTS, stride=pitch)

    def advance(w, carry, store):
        out = []
        for k in range(n_state_slabs):
            z_re, z_im = carry[2 * k], carry[2 * k + 1]
            s_re = s_ref[k, seg_rows(w), :]
            s_im = s_ref[n_state_slabs + k, seg_rows(w), :]
            if store:
                s_ref[k, seg_rows(w), :] = z_re
                s_ref[n_state_slabs + k, seg_rows(w), :] = z_im
            out.append(a_re[k] * z_re - a_im[k] * z_im + s_re)
            out.append(a_re[k] * z_im + a_im[k] * z_re + s_im)
        return tuple(out)

    zero = jnp.zeros(shape, _F32)
    ends = lax.fori_loop(0, seg, lambda w, c: advance(w, c, False), (zero,) * (2 * n_state_slabs),
                         unroll=SCAN_UNROLL)

    segidx = lax.broadcasted_iota(jnp.int32, shape, 0)
    init = []
    for k in range(n_state_slabs):
        i_re, i_im = zero, zero
        f_re, f_im = ends[2 * k], ends[2 * k + 1]
        for j in range(SSM_SEGMENTS - 1):
            c_re = g_re[k] * i_re - g_im[k] * i_im + f_re
            c_im = g_re[k] * i_im + g_im[k] * i_re + f_im
            i_re = jnp.where(segidx == j + 1, pltpu.roll(c_re, 1, 0), i_re)
            i_im = jnp.where(segidx == j + 1, pltpu.roll(c_im, 1, 0), i_im)
        init += [i_re, i_im]

    lax.fori_loop(0, seg, lambda w, c: advance(w, c, True), tuple(init), unroll=SCAN_UNROLL)

    for col in range(2 * n_state_slabs):
        for j in range(SSM_SEGMENTS):
            xb_ref[j * seg:(j + 1) * seg, col * LANES:(col + 1) * LANES] = (
                s_ref[col, j * pitch:j * pitch + seg, :].astype(_BF16))

    steps = out_w // LANES
    for i in range(T // steps):
        kk = (i + 1) * out_w
        cols = slice(i * out_w, (i + 1) * out_w)
        y = (jnp.dot(lhs_ref[:, :kk], toep_ref[:kk, cols], preferred_element_type=_F32)
             + jnp.dot(xb_ref[...], cp_ref[:, cols], preferred_element_type=_F32))
        z = jax.nn.gelu(y, approximate=True).astype(_BF16)
        for pp in range(steps // 2):
            gt = jnp.dot(z[:, pp * pair_w:(pp + 1) * pair_w], wg_ref[...],
                         preferred_element_type=_F32) + bg_ref[...]
            out = gt[:, :pair_w] * jax.nn.sigmoid(gt[:, pair_w:])
            t0 = i * steps + 2 * pp
            o_ref[t0] = out[:, :LANES].astype(_BF16)
            o_ref[t0 + 1] = out[:, LANES:].astype(_BF16)


def _complex_power(re, im, n):
    out_re, out_im = None, None
    while n:
        if n & 1:
            if out_re is None:
                out_re, out_im = re, im
            else:
                out_re, out_im = out_re * re - out_im * im, out_re * im + out_im * re
        n >>= 1
        if n:
            re, im = re * re - im * im, 2.0 * re * im
    return out_re, out_im


def _ssm_prep_body(seg, rows_ref, bt_ref, ct_ref, d_ref, wt_ref, bglu_ref,
                   wst_ref, kw_ref, cp_ref, wg_ref, bg_ref, ach_ref, aseg_ref):
    T, H = SSM_CHUNK, SSM_GROUP
    a_re, a_im = rows_ref[0:1, :], rows_ref[1:2, :]
    dt = jnp.exp(rows_ref[2:3, :])
    mag = jnp.exp(a_re * dt)
    ab_re = mag * jnp.cos(a_im * dt)
    ab_im = mag * jnp.sin(a_im * dt)
    inv_den = 1.0 / (a_re * a_re + a_im * a_im)
    nr = ab_re - 1.0
    coef_re = (nr * a_re + ab_im * a_im) * inv_den
    coef_im = (ab_im * a_re - nr * a_im) * inv_den

    def block_diag(t16, shape):
        row_g = lax.broadcasted_iota(jnp.int32, shape, 0) // H
        col_g = lax.broadcasted_iota(jnp.int32, shape, 1) // (shape[1] // SLAB_GROUPS)
        return jnp.where(row_g == col_g, jnp.concatenate([t16] * SLAB_GROUPS, axis=0), 0.0)

    cs = (LANES, SLAB_STATES)
    b_re, b_im = block_diag(bt_ref[0], cs), block_diag(bt_ref[1], cs)
    bb_re = coef_re * b_re - coef_im * b_im
    bb_im = coef_re * b_im + coef_im * b_re
    ct_re, ct_im = block_diag(ct_ref[0], cs), block_diag(ct_ref[1], cs)

    pw = [(jnp.ones_like(ab_re), jnp.zeros_like(ab_im))]
    for _ in range(T):
        pr, pi = pw[-1]
        pw.append((pr * ab_re - pi * ab_im, pr * ab_im + pi * ab_re))

    for t in range(T):
        pr, pi = pw[T - 1 - t]
        wst_ref[t * LANES:(t + 1) * LANES, :SLAB_STATES] = (bb_re * pr - bb_im * pi).astype(_BF16)
        wst_ref[t * LANES:(t + 1) * LANES, SLAB_STATES:] = (bb_re * pi + bb_im * pr).astype(_BF16)

    lags = []
    for k in range(T + 1):
        if k < T:
            lags.append(jnp.concatenate([ct_re, ct_im], axis=1))
        if k > 0:
            cp_ref[:SLAB_STATES, (k - 1) * LANES:k * LANES] = jnp.transpose(ct_re).astype(_BF16)
            cp_ref[SLAB_STATES:, (k - 1) * LANES:k * LANES] = jnp.transpose(-ct_im).astype(_BF16)
        ct_re, ct_im = ct_re * ab_re - ct_im * ab_im, ct_re * ab_im + ct_im * ab_re

    def split(a):
        hi = a.astype(_BF16)
        return hi, (a - hi.astype(_F32)).astype(_BF16)

    nt = (((1,), (1,)), ((), ()))
    a_hi, a_lo = split(jnp.concatenate(lags, axis=0))
    b_hi, b_lo = split(jnp.concatenate([bb_re, -bb_im], axis=1))
    kern_t = (lax.dot_general(a_hi, b_hi, nt, preferred_element_type=_F32)
              + lax.dot_general(a_hi, b_lo, nt, preferred_element_type=_F32)
              + lax.dot_general(a_lo, b_hi, nt, preferred_element_type=_F32))
    kern = jnp.transpose(kern_t)
    eye = (lax.broadcasted_iota(jnp.int32, (LANES, LANES), 0)
           == lax.broadcasted_iota(jnp.int32, (LANES, LANES), 1))
    kw_ref[:, :LANES] = (kern[:, :LANES] + jnp.where(eye, d_ref[...], 0.0)).astype(_BF16)
    kw_ref[:, LANES:] = kern[:, LANES:].astype(_BF16)

    cc = (LANES, LANES)
    w_lin = jnp.transpose(block_diag(wt_ref[0], cc)).astype(_BF16)
    w_gate = jnp.transpose(block_diag(wt_ref[1], cc)).astype(_BF16)
    wg_ref[...] = jnp.zeros(wg_ref.shape, _BF16)
    for t in range(2):
        wg_ref[t * LANES:(t + 1) * LANES, t * LANES:(t + 1) * LANES] = w_lin
        wg_ref[t * LANES:(t + 1) * LANES, (2 + t) * LANES:(3 + t) * LANES] = w_gate
    bg_ref[...] = jnp.concatenate([bglu_ref[0:1, :], bglu_ref[0:1, :], bglu_ref[1:2, :], bglu_ref[1:2, :]],
                                  axis=1)

    ach_ref[0:1, :], ach_ref[1:2, :] = pw[T]
    aseg_ref[0:1, :], aseg_ref[1:2, :] = _complex_power(pw[T][0], pw[T][1], seg)


def _ssm_weights(log_dt, a_re, a_im, b_re, b_im, c_re, c_im, d_skip, w_glu, b_glu, seg):
    P, H, T = SSM_STATE, SSM_GROUP, SSM_CHUNK
    SG, NS = SLAB_GROUPS, SSM_GROUPS // SLAB_GROUPS
    rows = jnp.stack([a_re.reshape(NS, SG * P), a_im.reshape(NS, SG * P),
                      jnp.repeat(log_dt, P).reshape(NS, SG * P)], axis=1)

    def per_group_rows(w, lead):
        t = w.reshape((NS, SG) + w.shape[1:])
        t = jnp.moveaxis(t, 2 + lead, 1)
        return t.reshape(NS, t.shape[1], -1)

    bt = jnp.stack([per_group_rows(b_re, 1), per_group_rows(b_im, 1)], axis=1)
    ct = jnp.stack([per_group_rows(c_re, 0), per_group_rows(c_im, 0)], axis=1)
    wt = jnp.stack([per_group_rows(w_glu[..., :H], 1), per_group_rows(w_glu[..., H:], 1)], axis=1)
    bglu = jnp.stack([b_glu[:, :H].reshape(NS, SG * H), b_glu[:, H:].reshape(NS, SG * H)], axis=1)
    d = d_skip.reshape(NS, 1, SG * H)

    blk3 = lambda i: (i, 0, 0)
    blk4 = lambda i: (i, 0, 0, 0)
    return pl.pallas_call(
        functools.partial(_ssm_prep_body, seg),
        grid=(NS,),
        in_specs=[
            pl.BlockSpec((None, 3, SG * P), blk3),
            pl.BlockSpec((None, 2, H, SG * P), blk4),
            pl.BlockSpec((None, 2, H, SG * P), blk4),
            pl.BlockSpec((None, 1, SG * H), blk3),
            pl.BlockSpec((None, 2, H, SG * H), blk4),
            pl.BlockSpec((None, 2, SG * H), blk3),
        ],
        out_specs=[
            pl.BlockSpec((None, T * LANES, 2 * SLAB_STATES), blk3),
            pl.BlockSpec((None, LANES, T * LANES), blk3),
            pl.BlockSpec((None, 2 * SLAB_STATES, T * LANES), blk3),
            pl.BlockSpec((None, 2 * LANES, 4 * LANES), blk3),
            pl.BlockSpec((None, 1, 4 * LANES), blk3),
            pl.BlockSpec((None, 2, SLAB_STATES), blk3),
            pl.BlockSpec((None, 2, SLAB_STATES), blk3),
        ],
        out_shape=[
            jax.ShapeDtypeStruct((NS, T * LANES, 2 * SLAB_STATES), _BF16),
            jax.ShapeDtypeStruct((NS, LANES, T * LANES), _BF16),
            jax.ShapeDtypeStruct((NS, 2 * SLAB_STATES, T * LANES), _BF16),
            jax.ShapeDtypeStruct((NS, 2 * LANES, 4 * LANES), _BF16),
            jax.ShapeDtypeStruct((NS, 1, 4 * LANES), _F32),
            jax.ShapeDtypeStruct((NS, 2, SLAB_STATES), _F32),
            jax.ShapeDtypeStruct((NS, 2, SLAB_STATES), _F32),
        ],
        compiler_params=pltpu.CompilerParams(
            dimension_semantics=("arbitrary",), vmem_limit_bytes=VMEM_LIMIT),
        name="ssm_prep",
    )(rows, bt, ct, d, wt, bglu)


def _ssm(u_t, weights):
    T, n_c, W = u_t.shape
    seg = n_c // SSM_SEGMENTS
    pitch = seg + SUBLANES
    ns = W // LANES
    wst, kw, cp, wg, bg, a_chunk, a_seg = weights
    slab = lambda i: (0, 0, i)
    blk = lambda i: (i, 0, 0)
    return pl.pallas_call(
        _ssm_body,
        grid=(ns,),
        in_specs=[
            pl.BlockSpec((T, n_c, LANES), slab),
            pl.BlockSpec((None, T * LANES, 2 * SLAB_STATES), blk),
            pl.BlockSpec((None, LANES, T * LANES), blk),
            pl.BlockSpec((None, 2 * SLAB_STATES, T * LANES), blk),
            pl.BlockSpec((None, 2 * LANES, 4 * LANES), blk),
            pl.BlockSpec((None, 1, 4 * LANES), blk),
            pl.BlockSpec((None, 2, SLAB_STATES), blk),
            pl.BlockSpec((None, 2, SLAB_STATES), blk),
        ],
        out_specs=pl.BlockSpec((T, n_c, LANES), slab),
        out_shape=jax.ShapeDtypeStruct((T, n_c, W), _BF16),
        scratch_shapes=[
            pltpu.VMEM((n_c, T * LANES), _BF16),
            pltpu.VMEM((T * LANES, T * LANES), _BF16),
            pltpu.VMEM((2 * SLAB_STATES // LANES, SSM_SEGMENTS * pitch, LANES), _F32),
            pltpu.VMEM((n_c, 2 * SLAB_STATES), _BF16),
        ],
        compiler_params=pltpu.CompilerParams(
            dimension_semantics=("arbitrary",), vmem_limit_bytes=VMEM_LIMIT),
        name="ssm",
    )(u_t, wst, kw, cp, wg, bg, a_chunk, a_seg)


def _outproj_body(n_cast, a_ref, s_ref, x_ref, ag_ref, sg_ref, w_ref, *refs):
    o_ref, sn_ref = refs[n_cast], refs[-1]
    for src, dst in zip(refs[:n_cast], refs[n_cast + 1:-1]):
        dst[...] = src[...].astype(_BF16)
    tm = x_ref.shape[0]
    an = _rms(a_ref[...].astype(_F32), ag_ref[...]).astype(_BF16)
    for t in range(SSM_CHUNK):
        sn_t = _rms(s_ref[t].astype(_F32), sg_ref[...])
        for s in range(SSM_WIDTH // LANES):
            sn_ref[s, pl.ds(t, tm // SSM_CHUNK, stride=SSM_CHUNK), :] = sn_t[:, s * LANES:(s + 1) * LANES]
    sn = jnp.concatenate([sn_ref[s] for s in range(SSM_WIDTH // LANES)], axis=1).astype(_BF16)
    mixed = jnp.concatenate([an, sn], axis=1)
    o_ref[...] = x_ref[...] + jnp.dot(mixed, w_ref[...], preferred_element_type=_F32)


def _outproj(attn, ssm_t, x, a_gain, s_gain, w_out, to_cast=(), *, tm=512):
    L, D = x.shape
    row = lambda i: (i, 0)
    fixed = lambda i: (0, 0)
    cast_specs, _, cast_shapes = _row_cast_specs(to_cast, L // tm)
    out = pl.pallas_call(
        functools.partial(_outproj_body, len(to_cast)),
        grid=(L // tm,),
        in_specs=[
            pl.BlockSpec((tm, ATTN_WIDTH), row),
            pl.BlockSpec((SSM_CHUNK, tm // SSM_CHUNK, SSM_WIDTH), lambda i: (0, i, 0)),
            pl.BlockSpec((tm, D), row),
            pl.BlockSpec((1, ATTN_WIDTH), fixed),
            pl.BlockSpec((1, SSM_WIDTH), fixed),
            pl.BlockSpec((ATTN_WIDTH + SSM_WIDTH, D), fixed),
        ] + cast_specs,
        out_specs=[pl.BlockSpec((tm, D), row)] + cast_specs,
        out_shape=[jax.ShapeDtypeStruct((L, D), _F32)] + cast_shapes,
        scratch_shapes=[pltpu.VMEM((SSM_WIDTH // LANES, tm, LANES), _F32)],
        compiler_params=pltpu.CompilerParams(
            dimension_semantics=("arbitrary",), vmem_limit_bytes=VMEM_LIMIT),
        name="outproj",
    )(attn, ssm_t, x, a_gain.reshape(1, -1), s_gain.reshape(1, -1), w_out, *to_cast)
    return out[0], out[1:]


def _layer(x, pos, invf, p):
    L = x.shape[0]
    seg = L // (SSM_SEGMENTS * SSM_CHUNK)
    w_gate1, w_up1 = _block_major_cast([p['ffn1_w_gate'], p['ffn1_w_up']])
    act, (w_down1,) = _ffn_up(x, p['ffn1_norm'], w_gate1, w_up1, [p['ffn1_w_down']])
    x, (w_in, w_out) = _ffn_down(act, w_down1, x, [p['w_in'], p['w_out']])
    q, k2, v2, u_t = _proj(x, p['mix_norm'], w_in, pos, invf, p['q_norm'], p['k_norm'])
    attn, (w_gate2, w_up2) = _attn(q, k2, v2, p['attn_sinks'], [p['ffn2_w_gate'], p['ffn2_w_up']], FFN_TF)
    weights = _ssm_weights(p['ssm_log_dt'], p['ssm_a_re'], p['ssm_a_im'], p['ssm_b_re'], p['ssm_b_im'],
                           p['ssm_c_re'], p['ssm_c_im'], p['ssm_d'], p['ssm_w_glu'], p['ssm_b_glu'], seg)
    ssm_t = _ssm(u_t, weights)
    x, (w_down2,) = _outproj(attn, ssm_t, x, p['attn_out_norm'], p['ssm_out_norm'], w_out,
                             [p['ffn2_w_down']])
    act, _ = _ffn_up(x, p['ffn2_norm'], w_gate2, w_up2)
    return _ffn_down(act, w_down2, x)[0]


def kernel(x, positions, ffn1_norm, ffn1_w_gate, ffn1_w_up, ffn1_w_down, mix_norm, w_in, q_norm, k_norm,
           attn_sinks, ssm_log_dt, ssm_a_re, ssm_a_im, ssm_b_re, ssm_b_im, ssm_c_re, ssm_c_im, ssm_d,
           ssm_w_glu, ssm_b_glu, attn_out_norm, ssm_out_norm, w_out, ffn2_norm, ffn2_w_gate, ffn2_w_up,
           ffn2_w_down):
    params = dict(
        ffn1_norm=ffn1_norm, ffn1_w_gate=ffn1_w_gate, ffn1_w_up=ffn1_w_up, ffn1_w_down=ffn1_w_down,
        mix_norm=mix_norm, w_in=w_in, q_norm=q_norm, k_norm=k_norm, attn_sinks=attn_sinks,
        ssm_log_dt=ssm_log_dt, ssm_a_re=ssm_a_re, ssm_a_im=ssm_a_im, ssm_b_re=ssm_b_re, ssm_b_im=ssm_b_im,
        ssm_c_re=ssm_c_re, ssm_c_im=ssm_c_im, ssm_d=ssm_d, ssm_w_glu=ssm_w_glu, ssm_b_glu=ssm_b_glu,
        attn_out_norm=attn_out_norm, ssm_out_norm=ssm_out_norm, w_out=w_out,
        ffn2_norm=ffn2_norm, ffn2_w_gate=ffn2_w_gate, ffn2_w_up=ffn2_w_up, ffn2_w_down=ffn2_w_down)
    depth = ffn1_norm.shape[0]
    half = HEAD_DIM // 2
    inv_freq = ROPE_THETA ** (-jnp.arange(half, dtype=_F32) * 2.0 / HEAD_DIM)
    invf = jnp.tile(inv_freq, LANES // half).reshape(1, LANES)
    outs = []
    for b in range(x.shape[0]):
        xb = x[b]
        for i in range(depth):
            xb = _layer(xb, positions[b], invf, {name: val[i] for name, val in params.items()})
        outs.append(xb)
    return jnp.stack(outs, axis=0)
```

```python
import functools
import math

import jax
import jax.numpy as jnp
from jax import lax
from jax.experimental import pallas as pl
from jax.experimental.pallas import tpu as pltpu

HEAD_DIM = 64
N_Q_HEADS = 16
N_KV_HEADS = 4
ATTN_WIDTH = N_Q_HEADS * HEAD_DIM
KV_WIDTH = N_KV_HEADS * HEAD_DIM
BLOCK = 128
ROPE_THETA = 10000.0
SSM_GROUP = 16
SSM_GROUPS = 64
SSM_STATE = 64
SSM_WIDTH = SSM_GROUP * SSM_GROUPS
FFN_RESIDUAL = 0.5
EPS = 1e-6

LANES = 128
SUBLANES = 8
MXU_DIM = 256
SSM_CHUNK = 8
SSM_SEGMENTS = SUBLANES
SLAB_GROUPS = LANES // SSM_GROUP
SLAB_STATES = SLAB_GROUPS * SSM_STATE
SCAN_UNROLL = 4
FFN_TF = 2 * MXU_DIM
VMEM_LIMIT = 56 * 1024 * 1024

_BF16 = jnp.bfloat16
_F32 = jnp.float32


def _rms(x, gain):
    ms = jnp.mean(x * x, axis=-1, keepdims=True)
    return x * lax.rsqrt(ms + EPS) * gain


def _ffn_up_body(n_cast, x_ref, gain_ref, wg_ref, wu_ref, *refs):
    src_refs, a_ref, dst_refs, h_ref = refs[:n_cast], refs[n_cast], refs[n_cast + 1:-1], refs[-1]
    first = pl.program_id(1) == 0

    @pl.when(first)
    def _():
        h_ref[...] = _rms(x_ref[...], gain_ref[...]).astype(_BF16)

    h = h_ref[...]
    g = jnp.dot(h, wg_ref[...], preferred_element_type=_F32)
    u = jnp.dot(h, wu_ref[...], preferred_element_type=_F32)
    a_ref[...] = (g * jax.nn.sigmoid(g) * u).astype(_BF16)

    for src, dst in zip(src_refs, dst_refs):
        dst[...] = src[...].astype(_BF16)


def _ffn_down_body(n_cast, a_ref, wd_ref, x_ref, *refs):
    o_ref = refs[n_cast]
    d = jnp.dot(a_ref[...], wd_ref[...], preferred_element_type=_F32)
    o_ref[...] = x_ref[...] + FFN_RESIDUAL * d
    for src, dst in zip(refs[:n_cast], refs[n_cast + 1:]):
        dst[...] = src[...].astype(_BF16)


def _row_cast_specs(arrays, n, col_block=None):
    pack = 2 * SUBLANES
    for w in arrays:
        assert w.shape[0] % (n * pack) == 0, w.shape
    in_specs = [pl.BlockSpec((w.shape[0] // n, w.shape[1]), lambda i: (i, 0)) for w in arrays]
    if col_block is None:
        return in_specs, in_specs, [jax.ShapeDtypeStruct(w.shape, _BF16) for w in arrays]
    out_specs = [pl.BlockSpec((w.shape[1] // col_block, w.shape[0] // n, col_block), lambda i: (0, i, 0))
                 for w in arrays]
    shapes = [jax.ShapeDtypeStruct((w.shape[1] // col_block, w.shape[0], col_block), _BF16) for w in arrays]
    return in_specs, out_specs, shapes


def _cast_block(src, dst):
    if len(dst.shape) == 2:
        dst[...] = src[...].astype(_BF16)
    else:
        for c in range(dst.shape[0]):
            dst[c] = src[:, c * dst.shape[2]:(c + 1) * dst.shape[2]].astype(_BF16)


def _step_cast_spec(shape, ni, nj):
    R, C = shape
    pack = 2 * SUBLANES
    if R % ni == 0 and (R // ni) % pack == 0 and C % nj == 0 and (C // nj) % LANES == 0:
        return pl.BlockSpec((R // ni, C // nj), lambda i, j: (i, j))
    assert R % (ni * nj) == 0 and (R // (ni * nj)) % pack == 0, shape
    return pl.BlockSpec((R // (ni * nj), C), lambda i, j: (i * nj + j, 0))


def _block_major_cast_body(*refs):
    n = len(refs) // 2
    for src, dst in zip(refs[:n], refs[n:]):
        dst[...] = src[...].astype(_BF16)


def _block_major_cast(weights):
    D, F = weights[0].shape
    return pl.pallas_call(
        _block_major_cast_body,
        grid=(F // FFN_TF,),
        in_specs=[pl.BlockSpec((D, FFN_TF), lambda j: (0, j)) for _ in weights],
        out_specs=[pl.BlockSpec((None, D, FFN_TF), lambda j: (j, 0, 0)) for _ in weights],
        out_shape=[jax.ShapeDtypeStruct((F // FFN_TF, D, FFN_TF), _BF16) for _ in weights],
        compiler_params=pltpu.CompilerParams(
            dimension_semantics=("arbitrary",), vmem_limit_bytes=VMEM_LIMIT),
        name="weight_cast",
    )(*weights)


def _ffn_up(x, gain, wg, wu, to_cast=(), *, tm=1024):
    L, D = x.shape
    nj, _, tf = wg.shape
    F = nj * tf
    tm = min(tm, L)
    ni = L // tm
    cast_specs = [_step_cast_spec(w.shape, ni, nj) for w in to_cast]
    out = pl.pallas_call(
        functools.partial(_ffn_up_body, len(to_cast)),
        grid=(ni, nj),
        in_specs=[
            pl.BlockSpec((tm, D), lambda i, j: (i, 0)),
            pl.BlockSpec((1, D), lambda i, j: (0, 0)),
            pl.BlockSpec((None, D, tf), lambda i, j: (j, 0, 0)),
            pl.BlockSpec((None, D, tf), lambda i, j: (j, 0, 0)),
        ] + cast_specs,
        out_specs=[pl.BlockSpec((tm, tf), lambda i, j: (i, j))] + cast_specs,
        out_shape=[jax.ShapeDtypeStruct((L, F), _BF16)]
        + [jax.ShapeDtypeStruct(w.shape, _BF16) for w in to_cast],
        scratch_shapes=[pltpu.VMEM((tm, D), _BF16)],
        compiler_params=pltpu.CompilerParams(
            dimension_semantics=("arbitrary", "arbitrary"), vmem_limit_bytes=VMEM_LIMIT),
        name="ffn_up",
    )(x, gain.reshape(1, D), wg, wu, *to_cast)
    return out[0], out[1:]


def _ffn_down(act, wd, x, to_cast=(), *, tm=512):
    L, D = x.shape
    F = act.shape[1]
    tm = min(tm, L)
    cast_specs, _, cast_shapes = _row_cast_specs(to_cast, L // tm)
    out = pl.pallas_call(
        functools.partial(_ffn_down_body, len(to_cast)),
        grid=(L // tm,),
        in_specs=[
            pl.BlockSpec((tm, F), lambda i: (i, 0)),
            pl.BlockSpec((F, D), lambda i: (0, 0), pipeline_mode=pl.Buffered(1)),
            pl.BlockSpec((tm, D), lambda i: (i, 0)),
        ] + cast_specs,
        out_specs=[pl.BlockSpec((tm, D), lambda i: (i, 0))] + cast_specs,
        out_shape=[jax.ShapeDtypeStruct((L, D), _F32)] + cast_shapes,
        compiler_params=pltpu.CompilerParams(
            dimension_semantics=("arbitrary",), vmem_limit_bytes=VMEM_LIMIT),
        name="ffn_down",
    )(act, wd, x, *to_cast)
    return out[0], out[1:]


def _proj_body(x_ref, gain_ref, w_ref, pos_ref, invf_ref, qg_ref, kg_ref, pn_ref,
               q_ref, k_ref, v_ref, u_ref, us_ref):
    tm = x_ref.shape[0]
    h = _rms(x_ref[...], gain_ref[...]).astype(_BF16)
    group = 2 * MXU_DIM

    def project(g):
        return jnp.dot(h, w_ref[:, g * group:(g + 1) * group], preferred_element_type=_F32)

    half = HEAD_DIM // 2
    ang = pos_ref[...] * invf_ref[...]
    lane_q = lax.broadcasted_iota(jnp.int32, ang.shape, 1) // half

    def spread(table):
        parts = []
        for qtr in range(LANES // half):
            m = jnp.where(lane_q == qtr, table, 0.0)
            parts.append(m + pltpu.roll(m, half, 1) + pltpu.roll(m, 2 * half, 1) + pltpu.roll(m, 3 * half, 1))
        return jnp.concatenate(parts, axis=0)

    cos = spread(jnp.cos(ang))
    sin = spread(jnp.sin(ang))
    lane = lax.broadcasted_iota(jnp.int32, (tm, LANES), 1)
    first_half = (lane & (HEAD_DIM // 2)) == 0
    low_head = lane < HEAD_DIM
    sin_signed = jnp.where(first_half, -sin, sin)

    def norm_rotary(x4, gain):
        ms4 = jnp.dot((x4 * x4).astype(_BF16), pn_ref[...], preferred_element_type=_F32)
        out = []
        for part in range(2):
            lanes = slice(part * LANES, (part + 1) * LANES)
            y = x4[:, lanes] * lax.rsqrt(ms4[:, lanes] + EPS) * gain
            swapped = jnp.where(first_half, pltpu.roll(y, LANES - HEAD_DIM // 2, 1),
                                pltpu.roll(y, HEAD_DIM // 2, 1))
            out.append(y * cos + swapped * sin_signed)
        return out

    def dup_heads(xc):
        r = pltpu.roll(xc, HEAD_DIM, 1)
        return jnp.where(low_head, xc, r), jnp.where(low_head, r, xc)

    scale = 1.0 / math.sqrt(HEAD_DIM)
    assert KV_WIDTH == MXU_DIM and 2 * KV_WIDTH == group
    for g in range(ATTN_WIDTH // group):
        pg = project(g)
        for c in range(group // MXU_DIM):
            for part, qc in enumerate(norm_rotary(pg[:, c * MXU_DIM:(c + 1) * MXU_DIM], qg_ref[...])):
                at = g * group + c * MXU_DIM + part * LANES
                q_ref[:, at:at + LANES] = (qc * scale).astype(_BF16)
    pg = project(ATTN_WIDTH // group)
    for part, kc in enumerate(norm_rotary(pg[:, :KV_WIDTH], kg_ref[...])):
        ka, kb = dup_heads(kc)
        k_ref[:, 2 * part * LANES:(2 * part + 1) * LANES] = ka.astype(_BF16)
        k_ref[:, (2 * part + 1) * LANES:(2 * part + 2) * LANES] = kb.astype(_BF16)
    for part in range(KV_WIDTH // LANES):
        va, vb = dup_heads(pg[:, KV_WIDTH + part * LANES:KV_WIDTH + (part + 1) * LANES])
        v_ref[:, 2 * part * LANES:(2 * part + 1) * LANES] = va.astype(_BF16)
        v_ref[:, (2 * part + 1) * LANES:(2 * part + 2) * LANES] = vb.astype(_BF16)
    first_u = (ATTN_WIDTH + 2 * KV_WIDTH) // group
    per_group = group // LANES
    for g in range(SSM_WIDTH // group):
        pg = project(first_u + g)
        for s in range(per_group):
            us_ref[g * per_group + s] = pg[:, s * LANES:(s + 1) * LANES]
        for t in range(SSM_CHUNK):
            for s in range(g * per_group, (g + 1) * per_group):
                rows = us_ref[s, pl.ds(t, tm // SSM_CHUNK, stride=SSM_CHUNK), :]
                u_ref[s, t] = rows.astype(_BF16)


def _proj(x, gain, w_in, pos, invf, q_gain, k_gain, *, tm=512):
    L, D = x.shape
    C = w_in.shape[1]
    quarters = LANES // (HEAD_DIM // 2)
    pos_f = pos.astype(_F32).reshape(L // tm, quarters, tm // quarters).transpose(0, 2, 1)
    pos_f = jnp.repeat(pos_f, HEAD_DIM // 2, axis=2)
    head_of_lane = jnp.arange(MXU_DIM) // HEAD_DIM
    pn = jnp.where(head_of_lane[:, None] == head_of_lane[None, :], 1.0 / HEAD_DIM, 0.0).astype(_BF16)
    qg = jnp.tile(q_gain, LANES // HEAD_DIM).reshape(1, LANES)
    kg = jnp.tile(k_gain, LANES // HEAD_DIM).reshape(1, LANES)
    row = lambda i: (i, 0)
    fixed = lambda i: (0, 0)
    return pl.pallas_call(
        _proj_body,
        grid=(L // tm,),
        in_specs=[
            pl.BlockSpec((tm, D), row),
            pl.BlockSpec((1, D), fixed),
            pl.BlockSpec((D, C), fixed),
            pl.BlockSpec((None, tm // quarters, LANES), lambda i: (i, 0, 0)),
            pl.BlockSpec((1, LANES), fixed),
            pl.BlockSpec((1, LANES), fixed),
            pl.BlockSpec((1, LANES), fixed),
            pl.BlockSpec((MXU_DIM, MXU_DIM), fixed),
        ],
        out_specs=[
            pl.BlockSpec((tm, ATTN_WIDTH), row),
            pl.BlockSpec((tm, 2 * KV_WIDTH), row),
            pl.BlockSpec((tm, 2 * KV_WIDTH), row),
            pl.BlockSpec((SSM_WIDTH // LANES, SSM_CHUNK, tm // SSM_CHUNK, LANES), lambda i: (0, 0, i, 0)),
        ],
        out_shape=[
            jax.ShapeDtypeStruct((L, ATTN_WIDTH), _BF16),
            jax.ShapeDtypeStruct((L, 2 * KV_WIDTH), _BF16),
            jax.ShapeDtypeStruct((L, 2 * KV_WIDTH), _BF16),
            jax.ShapeDtypeStruct((SSM_WIDTH // LANES, SSM_CHUNK, L // SSM_CHUNK, LANES), _BF16),
        ],
        scratch_shapes=[pltpu.VMEM((SSM_WIDTH // LANES, tm, LANES), _F32)],
        compiler_params=pltpu.CompilerParams(
            dimension_semantics=("arbitrary",), vmem_limit_bytes=VMEM_LIMIT),
        name="proj",
    )(x, gain.reshape(1, D), w_in, pos_f, invf, qg, kg, pn)


def _attn_body(n_cast, sink_ref, q_ref, kc_ref, vc_ref, kp_ref, vp_ref, *refs):
    o_ref = refs[n_cast]
    for src, dst in zip(refs[:n_cast], refs[n_cast + 1:]):
        _cast_block(src, dst)
    tq = q_ref.shape[0]
    kj = lax.broadcasted_iota(jnp.int32, (BLOCK, BLOCK), 0)
    qi = lax.broadcasted_iota(jnp.int32, (BLOCK, BLOCK), 1)
    from_prev = kj > qi
    has_prev = pl.program_id(0) > 0
    low_head = lax.broadcasted_iota(jnp.int32, (2 * BLOCK, LANES), 1) < HEAD_DIM
    nt = (((1,), (1,)), ((), ()))
    tn = (((0,), (0,)), ((), ()))

    for b in range(tq // BLOCK):
        rows = slice(b * BLOCK, (b + 1) * BLOCK)
        for hk in range(N_KV_HEADS):
            cols = slice(hk * LANES, (hk + 1) * LANES)
            if b == 0:
                kd = jnp.concatenate([kp_ref[:, cols], kc_ref[0:BLOCK, cols]], axis=0)
                vd = jnp.concatenate([vp_ref[:, cols], vc_ref[0:BLOCK, cols]], axis=0)
            else:
                kd = kc_ref[(b - 1) * BLOCK:(b + 1) * BLOCK, cols]
                vd = vc_ref[(b - 1) * BLOCK:(b + 1) * BLOCK, cols]
            zero = jnp.zeros_like(kd)
            k_half = (jnp.where(low_head, kd, zero), jnp.where(low_head, zero, kd))
            v_half = (jnp.where(low_head, vd, zero), jnp.where(low_head, zero, vd))
            for pp in range(2):
                pair = hk * 2 + pp
                qp = q_ref[rows, pair * LANES:(pair + 1) * LANES]
                acc = None
                for half in range(2):
                    sink = sink_ref[pair * 2 + half]
                    s2 = lax.dot_general(k_half[half], qp, nt, preferred_element_type=_F32)
                    s_prev = s2[:BLOCK]
                    if b == 0:
                        s_prev = jnp.where(has_prev, s_prev, -jnp.inf)
                    s = jnp.where(from_prev, s_prev, s2[BLOCK:])
                    m = jnp.maximum(jnp.max(s, axis=0, keepdims=True), sink)
                    p = jnp.exp(s - m)
                    den = jnp.sum(p, axis=0, keepdims=True) + jnp.exp(sink - m)
                    pb = p.astype(_BF16)
                    pz = jnp.zeros_like(pb)
                    p2 = jnp.concatenate([jnp.where(from_prev, pb, pz), jnp.where(from_prev, pz, pb)], axis=0)
                    o = lax.dot_general(v_half[half], p2, tn, preferred_element_type=_F32) * (1.0 / den)
                    acc = o if acc is None else acc + o
                o_ref[rows, pair * LANES:(pair + 1) * LANES] = jnp.transpose(acc).astype(_BF16)


def _attn(q, k2, v2, sinks, to_cast=(), cast_col_block=None, *, tq=512):
    L = q.shape[0]
    per = tq // BLOCK
    row = lambda i: (i, 0)
    prev = lambda i: (jnp.maximum(i * per - 1, 0), 0)
    cast_specs, cast_out_specs, cast_shapes = _row_cast_specs(to_cast, L // tq, cast_col_block)
    out = pl.pallas_call(
        functools.partial(_attn_body, len(to_cast)),
        grid=(L // tq,),
        in_specs=[
            pl.BlockSpec(memory_space=pltpu.SMEM),
            pl.BlockSpec((tq, ATTN_WIDTH), row),
            pl.BlockSpec((tq, 2 * KV_WIDTH), row),
            pl.BlockSpec((tq, 2 * KV_WIDTH), row),
            pl.BlockSpec((BLOCK, 2 * KV_WIDTH), prev),
            pl.BlockSpec((BLOCK, 2 * KV_WIDTH), prev),
        ] + cast_specs,
        out_specs=[pl.BlockSpec((tq, ATTN_WIDTH), row)] + cast_out_specs,
        out_shape=[jax.ShapeDtypeStruct((L, ATTN_WIDTH), _BF16)] + cast_shapes,
        compiler_params=pltpu.CompilerParams(
            dimension_semantics=("arbitrary",), vmem_limit_bytes=VMEM_LIMIT),
        name="attn",
    )(sinks, q, k2, v2, k2, v2, *to_cast)
    return out[0], out[1:]


def _ssm_body(u_ref, wst_ref, kw_ref, cp_ref, wg_ref, bg_ref, a_ref, aseg_ref, o_ref,
              lhs_ref, toep_ref, s_ref, xb_ref):
    T = SSM_CHUNK
    n_c = u_ref.shape[1]
    seg = n_c // SSM_SEGMENTS
    pitch = s_ref.shape[1] // SSM_SEGMENTS
    n_state_slabs = SLAB_STATES // LANES
    pair_w = 2 * LANES
    out_w = 2 * MXU_DIM

    for t in range(T):
        lhs_ref[:, t * LANES:(t + 1) * LANES] = u_ref[t]

    toep_ref[...] = jnp.zeros(toep_ref.shape, _BF16)
    for t in range(T):
        for tp in range(t, T):
            toep_ref[t * LANES:(t + 1) * LANES, tp * LANES:(tp + 1) * LANES] = (
                kw_ref[:, (tp - t) * LANES:(tp - t + 1) * LANES])

    per_dot = out_w // LANES
    for nb in range(2 * n_state_slabs // per_dot):
        res = jnp.dot(lhs_ref[...], wst_ref[:, nb * out_w:(nb + 1) * out_w], preferred_element_type=_F32)
        for part in range(per_dot):
            for j in range(SSM_SEGMENTS):
                s_ref[per_dot * nb + part, j * pitch:j * pitch + seg, :] = (
                    res[j * seg:(j + 1) * seg, part * LANES:(part + 1) * LANES])

    shape = (SSM_SEGMENTS, LANES)
    a_re = [jnp.broadcast_to(a_ref[0:1, k * LANES:(k + 1) * LANES], shape) for k in range(n_state_slabs)]
    a_im = [jnp.broadcast_to(a_ref[1:2, k * LANES:(k + 1) * LANES], shape) for k in range(n_state_slabs)]
    g_re = [jnp.broadcast_to(aseg_ref[0:1, k * LANES:(k + 1) * LANES], shape) for k in range(n_state_slabs)]
    g_im = [jnp.broadcast_to(aseg_ref[1:2, k * LANES:(k + 1) * LANES], shape) for k in range(n_state_slabs)]

    def seg_rows(w):
        return pl.ds(w, SSM_SEGMENTS, stride=pitch)

    def advance(w, carry, store):
        out = []
        for k in range(n_state_slabs):
            z_re, z_im = carry[2 * k], carry[2 * k + 1]
            s_re = s_ref[k, seg_rows(w), :]
            s_im = s_ref[n_state_slabs + k, seg_rows(w), :]
            if store:
                s_ref[k, seg_rows(w), :] = z_re
                s_ref[n_state_slabs + k, seg_rows(w), :] = z_im
            out.append(a_re[k] * z_re - a_im[k] * z_im + s_re)
            out.append(a_re[k] * z_im + a_im[k] * z_re + s_im)
        return tuple(out)

    zero = jnp.zeros(shape, _F32)
    ends = lax.fori_loop(0, seg, lambda w, c: advance(w, c, False), (zero,) * (2 * n_state_slabs),
                         unroll=SCAN_UNROLL)

    segidx = lax.broadcasted_iota(jnp.int32, shape, 0)
    init = []
    for k in range(n_state_slabs):
        i_re, i_im = zero, zero
        f_re, f_im = ends[2 * k], ends[2 * k + 1]
        for j in range(SSM_SEGMENTS - 1):
            c_re = g_re[k] * i_re - g_im[k] * i_im + f_re
            c_im = g_re[k] * i_im + g_im[k] * i_re + f_im
            i_re = jnp.where(segidx == j + 1, pltpu.roll(c_re, 1, 0), i_re)
            i_im = jnp.where(segidx == j + 1, pltpu.roll(c_im, 1, 0), i_im)
        init += [i_re, i_im]

    lax.fori_loop(0, seg, lambda w, c: advance(w, c, True), tuple(init), unroll=SCAN_UNROLL)

    for col in range(2 * n_state_slabs):
        for j in range(SSM_SEGMENTS):
            xb_ref[j * seg:(j + 1) * seg, col * LANES:(col + 1) * LANES] = (
                s_ref[col, j * pitch:j * pitch + seg, :].astype(_BF16))

    steps = out_w // LANES
    for i in range(T // steps):
        kk = (i + 1) * out_w
        cols = slice(i * out_w, (i + 1) * out_w)
        y = (jnp.dot(lhs_ref[:, :kk], toep_ref[:kk, cols], preferred_element_type=_F32)
             + jnp.dot(xb_ref[...], cp_ref[:, cols], preferred_element_type=_F32))
        z = jax.nn.gelu(y, approximate=True).astype(_BF16)
        for pp in range(steps // 2):
            gt = jnp.dot(z[:, pp * pair_w:(pp + 1) * pair_w], wg_ref[...],
                         preferred_element_type=_F32) + bg_ref[...]
            out = gt[:, :pair_w] * jax.nn.sigmoid(gt[:, pair_w:])
            t0 = i * steps + 2 * pp
            o_ref[t0] = out[:, :LANES].astype(_BF16)
            o_ref[t0 + 1] = out[:, LANES:].astype(_BF16)


def _complex_power(re, im, n):
    out_re, out_im = None, None
    while n:
        if n & 1:
            if out_re is None:
                out_re, out_im = re, im
            else:
                out_re, out_im = out_re * re - out_im * im, out_re * im + out_im * re
        n >>= 1
        if n:
            re, im = re * re - im * im, 2.0 * re * im
    return out_re, out_im


def _ssm_prep_body(seg, rows_ref, bt_ref, ct_ref, d_ref, wt_ref, bglu_ref,
                   wst_ref, kw_ref, cp_ref, wg_ref, bg_ref, ach_ref, aseg_ref):
    T, H = SSM_CHUNK, SSM_GROUP
    a_re, a_im = rows_ref[0:1, :], rows_ref[1:2, :]
    dt = jnp.exp(rows_ref[2:3, :])
    mag = jnp.exp(a_re * dt)
    ab_re = mag * jnp.cos(a_im * dt)
    ab_im = mag * jnp.sin(a_im * dt)
    inv_den = 1.0 / (a_re * a_re + a_im * a_im)
    nr = ab_re - 1.0
    coef_re = (nr * a_re + ab_im * a_im) * inv_den
    coef_im = (ab_im * a_re - nr * a_im) * inv_den

    def block_diag(t16, shape):
        row_g = lax.broadcasted_iota(jnp.int32, shape, 0) // H
        col_g = lax.broadcasted_iota(jnp.int32, shape, 1) // (shape[1] // SLAB_GROUPS)
        return jnp.where(row_g == col_g, jnp.concatenate([t16] * SLAB_GROUPS, axis=0), 0.0)

    cs = (LANES, SLAB_STATES)
    b_re, b_im = block_diag(bt_ref[0], cs), block_diag(bt_ref[1], cs)
    bb_re = coef_re * b_re - coef_im * b_im
    bb_im = coef_re * b_im + coef_im * b_re
    ct_re, ct_im = block_diag(ct_ref[0], cs), block_diag(ct_ref[1], cs)

    pw = [(jnp.ones_like(ab_re), jnp.zeros_like(ab_im))]
    for _ in range(T):
        pr, pi = pw[-1]
        pw.append((pr * ab_re - pi * ab_im, pr * ab_im + pi * ab_re))

    for t in range(T):
        pr, pi = pw[T - 1 - t]
        wst_ref[t * LANES:(t + 1) * LANES, :SLAB_STATES] = (bb_re * pr - bb_im * pi).astype(_BF16)
        wst_ref[t * LANES:(t + 1) * LANES, SLAB_STATES:] = (bb_re * pi + bb_im * pr).astype(_BF16)

    lags = []
    for k in range(T + 1):
        if k < T:
            lags.append(jnp.concatenate([ct_re, ct_im], axis=1))
        if k > 0:
            cp_ref[:SLAB_STATES, (k - 1) * LANES:k * LANES] = jnp.transpose(ct_re).astype(_BF16)
            cp_ref[SLAB_STATES:, (k - 1) * LANES:k * LANES] = jnp.transpose(-ct_im).astype(_BF16)
        ct_re, ct_im = ct_re * ab_re - ct_im * ab_im, ct_re * ab_im + ct_im * ab_re

    def split(a):
        hi = a.astype(_BF16)
        return hi, (a - hi.astype(_F32)).astype(_BF16)

    nt = (((1,), (1,)), ((), ()))
    a_hi, a_lo = split(jnp.concatenate(lags, axis=0))
    b_hi, b_lo = split(jnp.concatenate([bb_re, -bb_im], axis=1))
    kern_t = (lax.dot_general(a_hi, b_hi, nt, preferred_element_type=_F32)
              + lax.dot_general(a_hi, b_lo, nt, preferred_element_type=_F32)
              + lax.dot_general(a_lo, b_hi, nt, preferred_element_type=_F32))
    kern = jnp.transpose(kern_t)
    eye = (lax.broadcasted_iota(jnp.int32, (LANES, LANES), 0)
           == lax.broadcasted_iota(jnp.int32, (LANES, LANES), 1))
    kw_ref[:, :LANES] = (kern[:, :LANES] + jnp.where(eye, d_ref[...], 0.0)).astype(_BF16)
    kw_ref[:, LANES:] = kern[:, LANES:].astype(_BF16)

    cc = (LANES, LANES)
    w_lin = jnp.transpose(block_diag(wt_ref[0], cc)).astype(_BF16)
    w_gate = jnp.transpose(block_diag(wt_ref[1], cc)).astype(_BF16)
    wg_ref[...] = jnp.zeros(wg_ref.shape, _BF16)
    for t in range(2):
        wg_ref[t * LANES:(t + 1) * LANES, t * LANES:(t + 1) * LANES] = w_lin
        wg_ref[t * LANES:(t + 1) * LANES, (2 + t) * LANES:(3 + t) * LANES] = w_gate
    bg_ref[...] = jnp.concatenate([bglu_ref[0:1, :], bglu_ref[0:1, :], bglu_ref[1:2, :], bglu_ref[1:2, :]],
                                  axis=1)

    ach_ref[0:1, :], ach_ref[1:2, :] = pw[T]
    aseg_ref[0:1, :], aseg_ref[1:2, :] = _complex_power(pw[T][0], pw[T][1], seg)


def _ssm_weights(log_dt, a_re, a_im, b_re, b_im, c_re, c_im, d_skip, w_glu, b_glu, seg):
    P, H, T = SSM_STATE, SSM_GROUP, SSM_CHUNK
    SG, NS = SLAB_GROUPS, SSM_GROUPS // SLAB_GROUPS
    rows = jnp.stack([a_re.reshape(NS, SG * P), a_im.reshape(NS, SG * P),
                      jnp.repeat(log_dt, P).reshape(NS, SG * P)], axis=1)

    def per_group_rows(w, lead):
        t = w.reshape((NS, SG) + w.shape[1:])
        t = jnp.moveaxis(t, 2 + lead, 1)
        return t.reshape(NS, t.shape[1], -1)

    bt = jnp.stack([per_group_rows(b_re, 1), per_group_rows(b_im, 1)], axis=1)
    ct = jnp.stack([per_group_rows(c_re, 0), per_group_rows(c_im, 0)], axis=1)
    wt = jnp.stack([per_group_rows(w_glu[..., :H], 1), per_group_rows(w_glu[..., H:], 1)], axis=1)
    bglu = jnp.stack([b_glu[:, :H].reshape(NS, SG * H), b_glu[:, H:].reshape(NS, SG * H)], axis=1)
    d = d_skip.reshape(NS, 1, SG * H)

    blk3 = lambda i: (i, 0, 0)
    blk4 = lambda i: (i, 0, 0, 0)
    return pl.pallas_call(
        functools.partial(_ssm_prep_body, seg),
        grid=(NS,),
        in_specs=[
            pl.BlockSpec((None, 3, SG * P), blk3),
            pl.BlockSpec((None, 2, H, SG * P), blk4),
            pl.BlockSpec((None, 2, H, SG * P), blk4),
            pl.BlockSpec((None, 1, SG * H), blk3),
            pl.BlockSpec((None, 2, H, SG * H), blk4),
            pl.BlockSpec((None, 2, SG * H), blk3),
        ],
        out_specs=[
            pl.BlockSpec((None, T * LANES, 2 * SLAB_STATES), blk3),
            pl.BlockSpec((None, LANES, T * LANES), blk3),
            pl.BlockSpec((None, 2 * SLAB_STATES, T * LANES), blk3),
            pl.BlockSpec((None, 2 * LANES, 4 * LANES), blk3),
            pl.BlockSpec((None, 1, 4 * LANES), blk3),
            pl.BlockSpec((None, 2, SLAB_STATES), blk3),
            pl.BlockSpec((None, 2, SLAB_STATES), blk3),
        ],
        out_shape=[
            jax.ShapeDtypeStruct((NS, T * LANES, 2 * SLAB_STATES), _BF16),
            jax.ShapeDtypeStruct((NS, LANES, T * LANES), _BF16),
            jax.ShapeDtypeStruct((NS, 2 * SLAB_STATES, T * LANES), _BF16),
            jax.ShapeDtypeStruct((NS, 2 * LANES, 4 * LANES), _BF16),
            jax.ShapeDtypeStruct((NS, 1, 4 * LANES), _F32),
            jax.ShapeDtypeStruct((NS, 2, SLAB_STATES), _F32),
            jax.ShapeDtypeStruct((NS, 2, SLAB_STATES), _F32),
        ],
        compiler_params=pltpu.CompilerParams(
            dimension_semantics=("arbitrary",), vmem_limit_bytes=VMEM_LIMIT),
        name="ssm_prep",
    )(rows, bt, ct, d, wt, bglu)


def _ssm(u_t, weights):
    ns, T, n_c, _ = u_t.shape
    seg = n_c // SSM_SEGMENTS
    pitch = seg + SUBLANES
    wst, kw, cp, wg, bg, a_chunk, a_seg = weights
    slab = lambda i: (i, 0, 0, 0)
    blk = lambda i: (i, 0, 0)
    return pl.pallas_call(
        _ssm_body,
        grid=(ns,),
        in_specs=[
            pl.BlockSpec((None, T, n_c, LANES), slab),
            pl.BlockSpec((None, T * LANES, 2 * SLAB_STATES), blk),
            pl.BlockSpec((None, LANES, T * LANES), blk),
            pl.BlockSpec((None, 2 * SLAB_STATES, T * LANES), blk),
            pl.BlockSpec((None, 2 * LANES, 4 * LANES), blk),
            pl.BlockSpec((None, 1, 4 * LANES), blk),
            pl.BlockSpec((None, 2, SLAB_STATES), blk),
            pl.BlockSpec((None, 2, SLAB_STATES), blk),
        ],
        out_specs=pl.BlockSpec((None, T, n_c, LANES), slab),
        out_shape=jax.ShapeDtypeStruct((ns, T, n_c, LANES), _BF16),
        scratch_shapes=[
            pltpu.VMEM((n_c, T * LANES), _BF16),
            pltpu.VMEM((T * LANES, T * LANES), _BF16),
            pltpu.VMEM((2 * SLAB_STATES // LANES, SSM_SEGMENTS * pitch, LANES), _F32),
            pltpu.VMEM((n_c, 2 * SLAB_STATES), _BF16),
        ],
        compiler_params=pltpu.CompilerParams(
            dimension_semantics=("arbitrary",), vmem_limit_bytes=VMEM_LIMIT),
        name="ssm",
    )(u_t, wst, kw, cp, wg, bg, a_chunk, a_seg)


def _outproj_body(n_cast, a_ref, s_ref, x_ref, ag_ref, sg_ref, w_ref, *refs):
    o_ref, sn_ref = refs[n_cast], refs[-1]
    for src, dst in zip(refs[:n_cast], refs[n_cast + 1:-1]):
        dst[...] = src[...].astype(_BF16)
    tm = x_ref.shape[0]
    an = _rms(a_ref[...].astype(_F32), ag_ref[...]).astype(_BF16)
    for t in range(SSM_CHUNK):
        s_t = jnp.concatenate([s_ref[s, t] for s in range(SSM_WIDTH // LANES)], axis=1)
        sn_t = _rms(s_t.astype(_F32), sg_ref[...])
        for s in range(SSM_WIDTH // LANES):
            sn_ref[s, pl.ds(t, tm // SSM_CHUNK, stride=SSM_CHUNK), :] = sn_t[:, s * LANES:(s + 1) * LANES]
    sn = jnp.concatenate([sn_ref[s] for s in range(SSM_WIDTH // LANES)], axis=1).astype(_BF16)
    mixed = jnp.concatenate([an, sn], axis=1)
    o_ref[...] = x_ref[...] + jnp.dot(mixed, w_ref[...], preferred_element_type=_F32)


def _outproj(attn, ssm_t, x, a_gain, s_gain, w_out, to_cast=(), *, tm=512):
    L, D = x.shape
    row = lambda i: (i, 0)
    fixed = lambda i: (0, 0)
    cast_specs, _, cast_shapes = _row_cast_specs(to_cast, L // tm)
    out = pl.pallas_call(
        functools.partial(_outproj_body, len(to_cast)),
        grid=(L // tm,),
        in_specs=[
            pl.BlockSpec((tm, ATTN_WIDTH), row),
            pl.BlockSpec((SSM_WIDTH // LANES, SSM_CHUNK, tm // SSM_CHUNK, LANES), lambda i: (0, 0, i, 0)),
            pl.BlockSpec((tm, D), row),
            pl.BlockSpec((1, ATTN_WIDTH), fixed),
            pl.BlockSpec((1, SSM_WIDTH), fixed),
            pl.BlockSpec((ATTN_WIDTH + SSM_WIDTH, D), fixed),
        ] + cast_specs,
        out_specs=[pl.BlockSpec((tm, D), row)] + cast_specs,
        out_shape=[jax.ShapeDtypeStruct((L, D), _F32)] + cast_shapes,
        scratch_shapes=[pltpu.VMEM((SSM_WIDTH // LANES, tm, LANES), _F32)],
        compiler_params=pltpu.CompilerParams(
            dimension_semantics=("arbitrary",), vmem_limit_bytes=VMEM_LIMIT),
        name="outproj",
    )(attn, ssm_t, x, a_gain.reshape(1, -1), s_gain.reshape(1, -1), w_out, *to_cast)
    return out[0], out[1:]


def _layer(x, pos, invf, p):
    L = x.shape[0]
    seg = L // (SSM_SEGMENTS * SSM_CHUNK)
    w_gate1, w_up1 = _block_major_cast([p['ffn1_w_gate'], p['ffn1_w_up']])
    act, (w_down1,) = _ffn_up(x, p['ffn1_norm'], w_gate1, w_up1, [p['ffn1_w_down']])
    x, (w_in, w_out) = _ffn_down(act, w_down1, x, [p['w_in'], p['w_out']])
    q, k2, v2, u_t = _proj(x, p['mix_norm'], w_in, pos, invf, p['q_norm'], p['k_norm'])
    attn, (w_gate2, w_up2) = _attn(q, k2, v2, p['attn_sinks'], [p['ffn2_w_gate'], p['ffn2_w_up']], FFN_TF)
    weights = _ssm_weights(p['ssm_log_dt'], p['ssm_a_re'], p['ssm_a_im'], p['ssm_b_re'], p['ssm_b_im'],
                           p['ssm_c_re'], p['ssm_c_im'], p['ssm_d'], p['ssm_w_glu'], p['ssm_b_glu'], seg)
    ssm_t = _ssm(u_t, weights)
    x, (w_down2,) = _outproj(attn, ssm_t, x, p['attn_out_norm'], p['ssm_out_norm'], w_out,
                             [p['ffn2_w_down']])
    act, _ = _ffn_up(x, p['ffn2_norm'], w_gate2, w_up2)
    return _ffn_down(act, w_down2, x)[0]


def kernel(x, positions, ffn1_norm, ffn1_w_gate, ffn1_w_up, ffn1_w_down, mix_norm, w_in, q_norm, k_norm,
           attn_sinks, ssm_log_dt, ssm_a_re, ssm_a_im, ssm_b_re, ssm_b_im, ssm_c_re, ssm_c_im, ssm_d,
           ssm_w_glu, ssm_b_glu, attn_out_norm, ssm_out_norm, w_out, ffn2_norm, ffn2_w_gate, ffn2_w_up,
           ffn2_w_down):
    params = dict(
        ffn1_norm=ffn1_norm, ffn1_w_gate=ffn1_w_gate, ffn1_w_up=ffn1_w_up, ffn1_w_down=ffn1_w_down,
        mix_norm=mix_norm, w_in=w_in, q_norm=q_norm, k_norm=k_norm, attn_sinks=attn_sinks,
        ssm_log_dt=ssm_log_dt, ssm_a_re=ssm_a_re, ssm_a_im=ssm_a_im, ssm_b_re=ssm_b_re, ssm_b_im=ssm_b_im,
        ssm_c_re=ssm_c_re, ssm_c_im=ssm_c_im, ssm_d=ssm_d, ssm_w_glu=ssm_w_glu, ssm_b_glu=ssm_b_glu,
        attn_out_norm=attn_out_norm, ssm_out_norm=ssm_out_norm, w_out=w_out,
        ffn2_norm=ffn2_norm, ffn2_w_gate=ffn2_w_gate, ffn2_w_up=ffn2_w_up, ffn2_w_down=ffn2_w_down)
    depth = ffn1_norm.shape[0]
    half = HEAD_DIM // 2
    inv_freq = ROPE_THETA ** (-jnp.arange(half, dtype=_F32) * 2.0 / HEAD_DIM)
    invf = jnp.tile(inv_freq, LANES // half).reshape(1, LANES)
    outs = []
    for b in range(x.shape[0]):
        xb = x[b]
        for i in range(depth):
            xb = _layer(xb, positions[b], invf, {name: val[i] for name, val in params.items()})
        outs.append(xb)
    return jnp.stack(outs, axis=0)
```

```python
import functools
import math

import jax
import jax.numpy as jnp
from jax import lax
from jax.experimental import pallas as pl
from jax.experimental.pallas import tpu as pltpu

HEAD_DIM = 64
N_Q_HEADS = 16
N_KV_HEADS = 4
ATTN_WIDTH = N_Q_HEADS * HEAD_DIM
KV_WIDTH = N_KV_HEADS * HEAD_DIM
BLOCK = 128
ROPE_THETA = 10000.0
SSM_GROUP = 16
SSM_GROUPS = 64
SSM_STATE = 64
SSM_WIDTH = SSM_GROUP * SSM_GROUPS
FFN_RESIDUAL = 0.5
EPS = 1e-6

LANES = 128
SUBLANES = 8
MXU_DIM = 256
SSM_CHUNK = 8
SSM_SEGMENTS = SUBLANES
SLAB_GROUPS = LANES // SSM_GROUP
SLAB_STATES = SLAB_GROUPS * SSM_STATE
SCAN_UNROLL = 4
FFN_TF = 2 * MXU_DIM
VMEM_LIMIT = 56 * 1024 * 1024

_BF16 = jnp.bfloat16
_F32 = jnp.float32


def _rms(x, gain):
    ms = jnp.mean(x * x, axis=-1, keepdims=True)
    return x * lax.rsqrt(ms + EPS) * gain


def _ffn_up_body(n_cast, x_ref, gain_ref, wg_ref, wu_ref, *refs):
    src_refs, a_ref, dst_refs, h_ref = refs[:n_cast], refs[n_cast], refs[n_cast + 1:-1], refs[-1]
    first = pl.program_id(1) == 0

    @pl.when(first)
    def _():
        h_ref[...] = _rms(x_ref[...], gain_ref[...]).astype(_BF16)

    h = h_ref[...]
    g = jnp.dot(h, wg_ref[...], preferred_element_type=_F32)
    u = jnp.dot(h, wu_ref[...], preferred_element_type=_F32)
    a_ref[...] = (g * jax.nn.sigmoid(g) * u).astype(_BF16)

    for src, dst in zip(src_refs, dst_refs):
        dst[...] = src[...].astype(_BF16)


def _ffn_down_body(n_cast, a_ref, wd_ref, x_ref, *refs):
    o_ref = refs[n_cast]
    d = jnp.dot(a_ref[...], wd_ref[...], preferred_element_type=_F32)
    o_ref[...] = x_ref[...] + FFN_RESIDUAL * d
    for src, dst in zip(refs[:n_cast], refs[n_cast + 1:]):
        dst[...] = src[...].astype(_BF16)


def _row_cast_specs(arrays, n, col_block=None):
    pack = 2 * SUBLANES
    for w in arrays:
        assert w.shape[0] % (n * pack) == 0, w.shape
    in_specs = [pl.BlockSpec((w.shape[0] // n, w.shape[1]), lambda i: (i, 0)) for w in arrays]
    if col_block is None:
        return in_specs, in_specs, [jax.ShapeDtypeStruct(w.shape, _BF16) for w in arrays]
    out_specs = [pl.BlockSpec((w.shape[1] // col_block, w.shape[0] // n, col_block), lambda i: (0, i, 0))
                 for w in arrays]
    shapes = [jax.ShapeDtypeStruct((w.shape[1] // col_block, w.shape[0], col_block), _BF16) for w in arrays]
    return in_specs, out_specs, shapes


def _cast_block(src, dst):
    if len(dst.shape) == 2:
        dst[...] = src[...].astype(_BF16)
    else:
        for c in range(dst.shape[0]):
            dst[c] = src[:, c * dst.shape[2]:(c + 1) * dst.shape[2]].astype(_BF16)


def _step_cast_spec(shape, ni, nj):
    R, C = shape
    pack = 2 * SUBLANES
    if R % ni == 0 and (R // ni) % pack == 0 and C % nj == 0 and (C // nj) % LANES == 0:
        return pl.BlockSpec((R // ni, C // nj), lambda i, j: (i, j))
    assert R % (ni * nj) == 0 and (R // (ni * nj)) % pack == 0, shape
    return pl.BlockSpec((R // (ni * nj), C), lambda i, j: (i * nj + j, 0))


def _block_major_cast_body(*refs):
    n = len(refs) // 2
    for src, dst in zip(refs[:n], refs[n:]):
        dst[...] = src[...].astype(_BF16)


def _block_major_cast(weights):
    D, F = weights[0].shape
    return pl.pallas_call(
        _block_major_cast_body,
        grid=(F // FFN_TF,),
        in_specs=[pl.BlockSpec((D, FFN_TF), lambda j: (0, j)) for _ in weights],
        out_specs=[pl.BlockSpec((None, D, FFN_TF), lambda j: (j, 0, 0)) for _ in weights],
        out_shape=[jax.ShapeDtypeStruct((F // FFN_TF, D, FFN_TF), _BF16) for _ in weights],
        compiler_params=pltpu.CompilerParams(
            dimension_semantics=("arbitrary",), vmem_limit_bytes=VMEM_LIMIT),
        name="weight_cast",
    )(*weights)


def _ffn_up(x, gain, wg, wu, to_cast=(), *, tm=1024):
    L, D = x.shape
    nj, _, tf = wg.shape
    F = nj * tf
    tm = min(tm, L)
    ni = L // tm
    cast_specs = [_step_cast_spec(w.shape, ni, nj) for w in to_cast]
    out = pl.pallas_call(
        functools.partial(_ffn_up_body, len(to_cast)),
        grid=(ni, nj),
        in_specs=[
            pl.BlockSpec((tm, D), lambda i, j: (i, 0)),
            pl.BlockSpec((1, D), lambda i, j: (0, 0)),
            pl.BlockSpec((None, D, tf), lambda i, j: (j, 0, 0)),
            pl.BlockSpec((None, D, tf), lambda i, j: (j, 0, 0)),
        ] + cast_specs,
        out_specs=[pl.BlockSpec((tm, tf), lambda i, j: (i, j))] + cast_specs,
        out_shape=[jax.ShapeDtypeStruct((L, F), _BF16)]
        + [jax.ShapeDtypeStruct(w.shape, _BF16) for w in to_cast],
        scratch_shapes=[pltpu.VMEM((tm, D), _BF16)],
        compiler_params=pltpu.CompilerParams(
            dimension_semantics=("arbitrary", "arbitrary"), vmem_limit_bytes=VMEM_LIMIT),
        name="ffn_up",
    )(x, gain.reshape(1, D), wg, wu, *to_cast)
    return out[0], out[1:]


def _ffn_down(act, wd, x, to_cast=(), *, tm=512):
    L, D = x.shape
    F = act.shape[1]
    tm = min(tm, L)
    cast_specs, _, cast_shapes = _row_cast_specs(to_cast, L // tm)
    out = pl.pallas_call(
        functools.partial(_ffn_down_body, len(to_cast)),
        grid=(L // tm,),
        in_specs=[
            pl.BlockSpec((tm, F), lambda i: (i, 0)),
            pl.BlockSpec((F, D), lambda i: (0, 0), pipeline_mode=pl.Buffered(1)),
            pl.BlockSpec((tm, D), lambda i: (i, 0)),
        ] + cast_specs,
        out_specs=[pl.BlockSpec((tm, D), lambda i: (i, 0))] + cast_specs,
        out_shape=[jax.ShapeDtypeStruct((L, D), _F32)] + cast_shapes,
        compiler_params=pltpu.CompilerParams(
            dimension_semantics=("arbitrary",), vmem_limit_bytes=VMEM_LIMIT),
        name="ffn_down",
    )(act, wd, x, *to_cast)
    return out[0], out[1:]


def _proj_body(x_ref, gain_ref, w_ref, pos_ref, invf_ref, qg_ref, kg_ref, pn_ref,
               q_ref, k_ref, v_ref, u_ref, us_ref):
    tm = x_ref.shape[0]
    h = _rms(x_ref[...], gain_ref[...]).astype(_BF16)
    group = 2 * MXU_DIM

    def project(g):
        return jnp.dot(h, w_ref[:, g * group:(g + 1) * group], preferred_element_type=_F32)

    half = HEAD_DIM // 2
    ang = pos_ref[...] * invf_ref[...]
    lane_q = lax.broadcasted_iota(jnp.int32, ang.shape, 1) // half

    def spread(table):
        parts = []
        for qtr in range(LANES // half):
            m = jnp.where(lane_q == qtr, table, 0.0)
            parts.append(m + pltpu.roll(m, half, 1) + pltpu.roll(m, 2 * half, 1) + pltpu.roll(m, 3 * half, 1))
        return jnp.concatenate(parts, axis=0)

    cos = spread(jnp.cos(ang))
    sin = spread(jnp.sin(ang))
    lane = lax.broadcasted_iota(jnp.int32, (tm, LANES), 1)
    first_half = (lane & (HEAD_DIM // 2)) == 0
    low_head = lane < HEAD_DIM
    sin_signed = jnp.where(first_half, -sin, sin)

    def norm_rotary(x4, gain):
        ms4 = jnp.dot((x4 * x4).astype(_BF16), pn_ref[...], preferred_element_type=_F32)
        out = []
        for part in range(2):
            lanes = slice(part * LANES, (part + 1) * LANES)
            y = x4[:, lanes] * lax.rsqrt(ms4[:, lanes] + EPS) * gain
            swapped = jnp.where(first_half, pltpu.roll(y, LANES - HEAD_DIM // 2, 1),
                                pltpu.roll(y, HEAD_DIM // 2, 1))
            out.append(y * cos + swapped * sin_signed)
        return out

    def dup_heads(xc):
        r = pltpu.roll(xc, HEAD_DIM, 1)
        return jnp.where(low_head, xc, r), jnp.where(low_head, r, xc)

    scale = 1.0 / math.sqrt(HEAD_DIM)
    assert KV_WIDTH == MXU_DIM and 2 * KV_WIDTH == group
    for g in range(ATTN_WIDTH // group):
        pg = project(g)
        for c in range(group // MXU_DIM):
            for part, qc in enumerate(norm_rotary(pg[:, c * MXU_DIM:(c + 1) * MXU_DIM], qg_ref[...])):
                at = g * group + c * MXU_DIM + part * LANES
                q_ref[:, at:at + LANES] = (qc * scale).astype(_BF16)
    pg = project(ATTN_WIDTH // group)
    for part, kc in enumerate(norm_rotary(pg[:, :KV_WIDTH], kg_ref[...])):
        ka, kb = dup_heads(kc)
        k_ref[:, 2 * part * LANES:(2 * part + 1) * LANES] = ka.astype(_BF16)
        k_ref[:, (2 * part + 1) * LANES:(2 * part + 2) * LANES] = kb.astype(_BF16)
    for part in range(KV_WIDTH // LANES):
        va, vb = dup_heads(pg[:, KV_WIDTH + part * LANES:KV_WIDTH + (part + 1) * LANES])
        v_ref[:, 2 * part * LANES:(2 * part + 1) * LANES] = va.astype(_BF16)
        v_ref[:, (2 * part + 1) * LANES:(2 * part + 2) * LANES] = vb.astype(_BF16)
    first_u = (ATTN_WIDTH + 2 * KV_WIDTH) // group
    per_group = group // LANES
    for g in range(SSM_WIDTH // group):
        pg = project(first_u + g)
        for s in range(per_group):
            us_ref[g * per_group + s] = pg[:, s * LANES:(s + 1) * LANES]
        for t in range(SSM_CHUNK):
            for s in range(g * per_group, (g + 1) * per_group):
                rows = us_ref[s, pl.ds(t, tm // SSM_CHUNK, stride=SSM_CHUNK), :]
                u_ref[t, :, s * LANES:(s + 1) * LANES] = rows.astype(_BF16)


def _proj(x, gain, w_in, pos, invf, q_gain, k_gain, *, tm=512):
    L, D = x.shape
    C = w_in.shape[1]
    quarters = LANES // (HEAD_DIM // 2)
    pos_f = pos.astype(_F32).reshape(L // tm, quarters, tm // quarters).transpose(0, 2, 1)
    pos_f = jnp.repeat(pos_f, HEAD_DIM // 2, axis=2)
    head_of_lane = jnp.arange(MXU_DIM) // HEAD_DIM
    pn = jnp.where(head_of_lane[:, None] == head_of_lane[None, :], 1.0 / HEAD_DIM, 0.0).astype(_BF16)
    qg = jnp.tile(q_gain, LANES // HEAD_DIM).reshape(1, LANES)
    kg = jnp.tile(k_gain, LANES // HEAD_DIM).reshape(1, LANES)
    row = lambda i: (i, 0)
    fixed = lambda i: (0, 0)
    return pl.pallas_call(
        _proj_body,
        grid=(L // tm,),
        in_specs=[
            pl.BlockSpec((tm, D), row),
            pl.BlockSpec((1, D), fixed),
            pl.BlockSpec((D, C), fixed),
            pl.BlockSpec((None, tm // quarters, LANES), lambda i: (i, 0, 0)),
            pl.BlockSpec((1, LANES), fixed),
            pl.BlockSpec((1, LANES), fixed),
            pl.BlockSpec((1, LANES), fixed),
            pl.BlockSpec((MXU_DIM, MXU_DIM), fixed),
        ],
        out_specs=[
            pl.BlockSpec((tm, ATTN_WIDTH), row),
            pl.BlockSpec((tm, 2 * KV_WIDTH), row),
            pl.BlockSpec((tm, 2 * KV_WIDTH), row),
            pl.BlockSpec((SSM_CHUNK, tm // SSM_CHUNK, SSM_WIDTH), lambda i: (0, i, 0)),
        ],
        out_shape=[
            jax.ShapeDtypeStruct((L, ATTN_WIDTH), _BF16),
            jax.ShapeDtypeStruct((L, 2 * KV_WIDTH), _BF16),
            jax.ShapeDtypeStruct((L, 2 * KV_WIDTH), _BF16),
            jax.ShapeDtypeStruct((SSM_CHUNK, L // SSM_CHUNK, SSM_WIDTH), _BF16),
        ],
        scratch_shapes=[pltpu.VMEM((SSM_WIDTH // LANES, tm, LANES), _F32)],
        compiler_params=pltpu.CompilerParams(
            dimension_semantics=("arbitrary",), vmem_limit_bytes=VMEM_LIMIT),
        name="proj",
    )(x, gain.reshape(1, D), w_in, pos_f, invf, qg, kg, pn)


def _attn_body(n_cast, sink_ref, q_ref, kc_ref, vc_ref, kp_ref, vp_ref, *refs):
    o_ref = refs[n_cast]
    for src, dst in zip(refs[:n_cast], refs[n_cast + 1:]):
        _cast_block(src, dst)
    tq = q_ref.shape[0]
    kj = lax.broadcasted_iota(jnp.int32, (BLOCK, BLOCK), 0)
    qi = lax.broadcasted_iota(jnp.int32, (BLOCK, BLOCK), 1)
    from_prev = kj > qi
    has_prev = pl.program_id(0) > 0
    low_head = lax.broadcasted_iota(jnp.int32, (2 * BLOCK, LANES), 1) < HEAD_DIM
    nt = (((1,), (1,)), ((), ()))
    tn = (((0,), (0,)), ((), ()))

    for b in range(tq // BLOCK):
        rows = slice(b * BLOCK, (b + 1) * BLOCK)
        for hk in range(N_KV_HEADS):
            cols = slice(hk * LANES, (hk + 1) * LANES)
            if b == 0:
                kd = jnp.concatenate([kp_ref[:, cols], kc_ref[0:BLOCK, cols]], axis=0)
                vd = jnp.concatenate([vp_ref[:, cols], vc_ref[0:BLOCK, cols]], axis=0)
            else:
                kd = kc_ref[(b - 1) * BLOCK:(b + 1) * BLOCK, cols]
                vd = vc_ref[(b - 1) * BLOCK:(b + 1) * BLOCK, cols]
            zero = jnp.zeros_like(kd)
            k_half = (jnp.where(low_head, kd, zero), jnp.where(low_head, zero, kd))
            v_half = (jnp.where(low_head, vd, zero), jnp.where(low_head, zero, vd))
            for pp in range(2):
                pair = hk * 2 + pp
                qp = q_ref[rows, pair * LANES:(pair + 1) * LANES]
                acc = None
                for half in range(2):
                    sink = sink_ref[pair * 2 + half]
                    s2 = lax.dot_general(k_half[half], qp, nt, preferred_element_type=_F32)
                    s_prev = s2[:BLOCK]
                    if b == 0:
                        s_prev = jnp.where(has_prev, s_prev, -jnp.inf)
                    s = jnp.where(from_prev, s_prev, s2[BLOCK:])
                    m = jnp.maximum(jnp.max(s, axis=0, keepdims=True), sink)
                    p = jnp.exp(s - m)
                    den = jnp.sum(p, axis=0, keepdims=True) + jnp.exp(sink - m)
                    pb = p.astype(_BF16)
                    pz = jnp.zeros_like(pb)
                    p2 = jnp.concatenate([jnp.where(from_prev, pb, pz), jnp.where(from_prev, pz, pb)], axis=0)
                    o = lax.dot_general(v_half[half], p2, tn, preferred_element_type=_F32) * (1.0 / den)
                    acc = o if acc is None else acc + o
                o_ref[rows, pair * LANES:(pair + 1) * LANES] = jnp.transpose(acc).astype(_BF16)


def _attn(q, k2, v2, sinks, to_cast=(), cast_col_block=None, *, tq=1024):
    L = q.shape[0]
    per = tq // BLOCK
    row = lambda i: (i, 0)
    prev = lambda i: (jnp.maximum(i * per - 1, 0), 0)
    cast_specs, cast_out_specs, cast_shapes = _row_cast_specs(to_cast, L // tq, cast_col_block)
    out = pl.pallas_call(
        functools.partial(_attn_body, len(to_cast)),
        grid=(L // tq,),
        in_specs=[
            pl.BlockSpec(memory_space=pltpu.SMEM),
            pl.BlockSpec((tq, ATTN_WIDTH), row),
            pl.BlockSpec((tq, 2 * KV_WIDTH), row),
            pl.BlockSpec((tq, 2 * KV_WIDTH), row),
            pl.BlockSpec((BLOCK, 2 * KV_WIDTH), prev),
            pl.BlockSpec((BLOCK, 2 * KV_WIDTH), prev),
        ] + cast_specs,
        out_specs=[pl.BlockSpec((tq, ATTN_WIDTH), row)] + cast_out_specs,
        out_shape=[jax.ShapeDtypeStruct((L, ATTN_WIDTH), _BF16)] + cast_shapes,
        compiler_params=pltpu.CompilerParams(
            dimension_semantics=("arbitrary",), vmem_limit_bytes=VMEM_LIMIT),
        name="attn",
    )(sinks, q, k2, v2, k2, v2, *to_cast)
    return out[0], out[1:]


def _ssm_body(u_ref, wst_ref, kw_ref, cp_ref, wg_ref, bg_ref, a_ref, aseg_ref, o_ref,
              lhs_ref, toep_ref, s_ref, xb_ref):
    T = SSM_CHUNK
    n_c = u_ref.shape[1]
    seg = n_c // SSM_SEGMENTS
    pitch = s_ref.shape[1] // SSM_SEGMENTS
    n_state_slabs = SLAB_STATES // LANES
    pair_w = 2 * LANES
    out_w = 2 * MXU_DIM

    for t in range(T):
        lhs_ref[:, t * LANES:(t + 1) * LANES] = u_ref[t]

    toep_ref[...] = jnp.zeros(toep_ref.shape, _BF16)
    for t in range(T):
        for tp in range(t, T):
            toep_ref[t * LANES:(t + 1) * LANES, tp * LANES:(tp + 1) * LANES] = (
                kw_ref[:, (tp - t) * LANES:(tp - t + 1) * LANES])

    per_dot = out_w // LANES
    for nb in range(2 * n_state_slabs // per_dot):
        res = jnp.dot(lhs_ref[...], wst_ref[:, nb * out_w:(nb + 1) * out_w], preferred_element_type=_F32)
        for part in range(per_dot):
            for j in range(SSM_SEGMENTS):
                s_ref[per_dot * nb + part, j * pitch:j * pitch + seg, :] = (
                    res[j * seg:(j + 1) * seg, part * LANES:(part + 1) * LANES])

    shape = (SSM_SEGMENTS, LANES)
    a_re = [jnp.broadcast_to(a_ref[0:1, k * LANES:(k + 1) * LANES], shape) for k in range(n_state_slabs)]
    a_im = [jnp.broadcast_to(a_ref[1:2, k * LANES:(k + 1) * LANES], shape) for k in range(n_state_slabs)]
    g_re = [jnp.broadcast_to(aseg_ref[0:1, k * LANES:(k + 1) * LANES], shape) for k in range(n_state_slabs)]
    g_im = [jnp.broadcast_to(aseg_ref[1:2, k * LANES:(k + 1) * LANES], shape) for k in range(n_state_slabs)]

    def seg_rows(w):
        return pl.ds(w, SSM_SEGMENTS, stride=pitch)

    def advance(w, carry, store):
        out = []
        for k in range(n_state_slabs):
            z_re, z_im = carry[2 * k], carry[2 * k + 1]
            s_re = s_ref[k, seg_rows(w), :]
            s_im = s_ref[n_state_slabs + k, seg_rows(w), :]
            if store:
                s_ref[k, seg_rows(w), :] = z_re
                s_ref[n_state_slabs + k, seg_rows(w), :] = z_im
            out.append(a_re[k] * z_re - a_im[k] * z_im + s_re)
            out.append(a_re[k] * z_im + a_im[k] * z_re + s_im)
        return tuple(out)

    zero = jnp.zeros(shape, _F32)
    ends = lax.fori_loop(0, seg, lambda w, c: advance(w, c, False), (zero,) * (2 * n_state_slabs),
                         unroll=SCAN_UNROLL)

    segidx = lax.broadcasted_iota(jnp.int32, shape, 0)
    init = []
    for k in range(n_state_slabs):
        i_re, i_im = zero, zero
        f_re, f_im = ends[2 * k], ends[2 * k + 1]
        for j in range(SSM_SEGMENTS - 1):
            c_re = g_re[k] * i_re - g_im[k] * i_im + f_re
            c_im = g_re[k] * i_im + g_im[k] * i_re + f_im
            i_re = jnp.where(segidx == j + 1, pltpu.roll(c_re, 1, 0), i_re)
            i_im = jnp.where(segidx == j + 1, pltpu.roll(c_im, 1, 0), i_im)
        init += [i_re, i_im]

    lax.fori_loop(0, seg, lambda w, c: advance(w, c, True), tuple(init), unroll=SCAN_UNROLL)

    for col in range(2 * n_state_slabs):
        for j in range(SSM_SEGMENTS):
            xb_ref[j * seg:(j + 1) * seg, col * LANES:(col + 1) * LANES] = (
                s_ref[col, j * pitch:j * pitch + seg, :].astype(_BF16))

    steps = out_w // LANES
    for i in range(T // steps):
        kk = (i + 1) * out_w
        cols = slice(i * out_w, (i + 1) * out_w)
        y = (jnp.dot(lhs_ref[:, :kk], toep_ref[:kk, cols], preferred_element_type=_F32)
             + jnp.dot(xb_ref[...], cp_ref[:, cols], preferred_element_type=_F32))
        z = jax.nn.gelu(y, approximate=True).astype(_BF16)
        for pp in range(steps // 2):
            gt = jnp.dot(z[:, pp * pair_w:(pp + 1) * pair_w], wg_ref[...],
                         preferred_element_type=_F32) + bg_ref[...]
            out = gt[:, :pair_w] * jax.nn.sigmoid(gt[:, pair_w:])
            t0 = i * steps + 2 * pp
            o_ref[t0] = out[:, :LANES].astype(_BF16)
            o_ref[t0 + 1] = out[:, LANES:].astype(_BF16)


def _complex_power(re, im, n):
    out_re, out_im = None, None
    while n:
        if n & 1:
            if out_re is None:
                out_re, out_im = re, im
            else:
                out_re, out_im = out_re * re - out_im * im, out_re * im + out_im * re
        n >>= 1
        if n:
            re, im = re * re - im * im, 2.0 * re * im
    return out_re, out_im


def _ssm_prep_body(seg, rows_ref, bt_ref, ct_ref, d_ref, wt_ref, bglu_ref,
                   wst_ref, kw_ref, cp_ref, wg_ref, bg_ref, ach_ref, aseg_ref):
    T, H = SSM_CHUNK, SSM_GROUP
    a_re, a_im = rows_ref[0:1, :], rows_ref[1:2, :]
    dt = jnp.exp(rows_ref[2:3, :])
    mag = jnp.exp(a_re * dt)
    ab_re = mag * jnp.cos(a_im * dt)
    ab_im = mag * jnp.sin(a_im * dt)
    inv_den = 1.0 / (a_re * a_re + a_im * a_im)
    nr = ab_re - 1.0
    coef_re = (nr * a_re + ab_im * a_im) * inv_den
    coef_im = (ab_im * a_re - nr * a_im) * inv_den

    def block_diag(t16, shape):
        row_g = lax.broadcasted_iota(jnp.int32, shape, 0) // H
        col_g = lax.broadcasted_iota(jnp.int32, shape, 1) // (shape[1] // SLAB_GROUPS)
        return jnp.where(row_g == col_g, jnp.concatenate([t16] * SLAB_GROUPS, axis=0), 0.0)

    cs = (LANES, SLAB_STATES)
    b_re, b_im = block_diag(bt_ref[0], cs), block_diag(bt_ref[1], cs)
    bb_re = coef_re * b_re - coef_im * b_im
    bb_im = coef_re * b_im + coef_im * b_re
    ct_re, ct_im = block_diag(ct_ref[0], cs), block_diag(ct_ref[1], cs)

    pw = [(jnp.ones_like(ab_re), jnp.zeros_like(ab_im))]
    for _ in range(T):
        pr, pi = pw[-1]
        pw.append((pr * ab_re - pi * ab_im, pr * ab_im + pi * ab_re))

    for t in range(T):
        pr, pi = pw[T - 1 - t]
        wst_ref[t * LANES:(t + 1) * LANES, :SLAB_STATES] = (bb_re * pr - bb_im * pi).astype(_BF16)
        wst_ref[t * LANES:(t + 1) * LANES, SLAB_STATES:] = (bb_re * pi + bb_im * pr).astype(_BF16)

    lags = []
    for k in range(T + 1):
        if k < T:
            lags.append(jnp.concatenate([ct_re, ct_im], axis=1))
        if k > 0:
            cp_ref[:SLAB_STATES, (k - 1) * LANES:k * LANES] = jnp.transpose(ct_re).astype(_BF16)
            cp_ref[SLAB_STATES:, (k - 1) * LANES:k * LANES] = jnp.transpose(-ct_im).astype(_BF16)
        ct_re, ct_im = ct_re * ab_re - ct_im * ab_im, ct_re * ab_im + ct_im * ab_re

    def split(a):
        hi = a.astype(_BF16)
        return hi, (a - hi.astype(_F32)).astype(_BF16)

    nt = (((1,), (1,)), ((), ()))
    a_hi, a_lo = split(jnp.concatenate(lags, axis=0))
    b_hi, b_lo = split(jnp.concatenate([bb_re, -bb_im], axis=1))
    kern_t = (lax.dot_general(a_hi, b_hi, nt, preferred_element_type=_F32)
              + lax.dot_general(a_hi, b_lo, nt, preferred_element_type=_F32)
              + lax.dot_general(a_lo, b_hi, nt, preferred_element_type=_F32))
    kern = jnp.transpose(kern_t)
    eye = (lax.broadcasted_iota(jnp.int32, (LANES, LANES), 0)
           == lax.broadcasted_iota(jnp.int32, (LANES, LANES), 1))
    kw_ref[:, :LANES] = (kern[:, :LANES] + jnp.where(eye, d_ref[...], 0.0)).astype(_BF16)
    kw_ref[:, LANES:] = kern[:, LANES:].astype(_BF16)

    cc = (LANES, LANES)
    w_lin = jnp.transpose(block_diag(wt_ref[0], cc)).astype(_BF16)
    w_gate = jnp.transpose(block_diag(wt_ref[1], cc)).astype(_BF16)
    wg_ref[...] = jnp.zeros(wg_ref.shape, _BF16)
    for t in range(2):
        wg_ref[t * LANES:(t + 1) * LANES, t * LANES:(t + 1) * LANES] = w_lin
        wg_ref[t * LANES:(t + 1) * LANES, (2 + t) * LANES:(3 + t) * LANES] = w_gate
    bg_ref[...] = jnp.concatenate([bglu_ref[0:1, :], bglu_ref[0:1, :], bglu_ref[1:2, :], bglu_ref[1:2, :]],
                                  axis=1)

    ach_ref[0:1, :], ach_ref[1:2, :] = pw[T]
    aseg_ref[0:1, :], aseg_ref[1:2, :] = _complex_power(pw[T][0], pw[T][1], seg)


def _ssm_weights(log_dt, a_re, a_im, b_re, b_im, c_re, c_im, d_skip, w_glu, b_glu, seg):
    P, H, T = SSM_STATE, SSM_GROUP, SSM_CHUNK
    SG, NS = SLAB_GROUPS, SSM_GROUPS // SLAB_GROUPS
    rows = jnp.stack([a_re.reshape(NS, SG * P), a_im.reshape(NS, SG * P),
                      jnp.repeat(log_dt, P).reshape(NS, SG * P)], axis=1)

    def per_group_rows(w, lead):
        t = w.reshape((NS, SG) + w.shape[1:])
        t = jnp.moveaxis(t, 2 + lead, 1)
        return t.reshape(NS, t.shape[1], -1)

    bt = jnp.stack([per_group_rows(b_re, 1), per_group_rows(b_im, 1)], axis=1)
    ct = jnp.stack([per_group_rows(c_re, 0), per_group_rows(c_im, 0)], axis=1)
    wt = jnp.stack([per_group_rows(w_glu[..., :H], 1), per_group_rows(w_glu[..., H:], 1)], axis=1)
    bglu = jnp.stack([b_glu[:, :H].reshape(NS, SG * H), b_glu[:, H:].reshape(NS, SG * H)], axis=1)
    d = d_skip.reshape(NS, 1, SG * H)

    blk3 = lambda i: (i, 0, 0)
    blk4 = lambda i: (i, 0, 0, 0)
    return pl.pallas_call(
        functools.partial(_ssm_prep_body, seg),
        grid=(NS,),
        in_specs=[
            pl.BlockSpec((None, 3, SG * P), blk3),
            pl.BlockSpec((None, 2, H, SG * P), blk4),
            pl.BlockSpec((None, 2, H, SG * P), blk4),
            pl.BlockSpec((None, 1, SG * H), blk3),
            pl.BlockSpec((None, 2, H, SG * H), blk4),
            pl.BlockSpec((None, 2, SG * H), blk3),
        ],
        out_specs=[
            pl.BlockSpec((None, T * LANES, 2 * SLAB_STATES), blk3),
            pl.BlockSpec((None, LANES, T * LANES), blk3),
            pl.BlockSpec((None, 2 * SLAB_STATES, T * LANES), blk3),
            pl.BlockSpec((None, 2 * LANES, 4 * LANES), blk3),
            pl.BlockSpec((None, 1, 4 * LANES), blk3),
            pl.BlockSpec((None, 2, SLAB_STATES), blk3),
            pl.BlockSpec((None, 2, SLAB_STATES), blk3),
        ],
        out_shape=[
            jax.ShapeDtypeStruct((NS, T * LANES, 2 * SLAB_STATES), _BF16),
            jax.ShapeDtypeStruct((NS, LANES, T * LANES), _BF16),
            jax.ShapeDtypeStruct((NS, 2 * SLAB_STATES, T * LANES), _BF16),
            jax.ShapeDtypeStruct((NS, 2 * LANES, 4 * LANES), _BF16),
            jax.ShapeDtypeStruct((NS, 1, 4 * LANES), _F32),
            jax.ShapeDtypeStruct((NS, 2, SLAB_STATES), _F32),
            jax.ShapeDtypeStruct((NS, 2, SLAB_STATES), _F32),
        ],
        compiler_params=pltpu.CompilerParams(
            dimension_semantics=("arbitrary",), vmem_limit_bytes=VMEM_LIMIT),
        name="ssm_prep",
    )(rows, bt, ct, d, wt, bglu)


def _ssm(u_t, weights):
    T, n_c, W = u_t.shape
    seg = n_c // SSM_SEGMENTS
    pitch = seg + SUBLANES
    ns = W // LANES
    wst, kw, cp, wg, bg, a_chunk, a_seg = weights
    slab = lambda i: (0, 0, i)
    blk = lambda i: (i, 0, 0)
    return pl.pallas_call(
        _ssm_body,
        grid=(ns,),
        in_specs=[
            pl.BlockSpec((T, n_c, LANES), slab),
            pl.BlockSpec((None, T * LANES, 2 * SLAB_STATES), blk),
            pl.BlockSpec((None, LANES, T * LANES), blk),
            pl.BlockSpec((None, 2 * SLAB_STATES, T * LANES), blk),
            pl.BlockSpec((None, 2 * LANES, 4 * LANES), blk),
            pl.BlockSpec((None, 1, 4 * LANES), blk),
            pl.BlockSpec((None, 2, SLAB_STATES), blk),
            pl.BlockSpec((None, 2, SLAB_STATES), blk),
        ],
        out_specs=pl.BlockSpec((T, n_c, LANES), slab),
        out_shape=jax.ShapeDtypeStruct((T, n_c, W), _BF16),
        scratch_shapes=[
            pltpu.VMEM((n_c, T * LANES), _BF16),
            pltpu.VMEM((T * LANES, T * LANES), _BF16),
            pltpu.VMEM((2 * SLAB_STATES // LANES, SSM_SEGMENTS * pitch, LANES), _F32),
            pltpu.VMEM((n_c, 2 * SLAB_STATES), _BF16),
        ],
        compiler_params=pltpu.CompilerParams(
            dimension_semantics=("arbitrary",), vmem_limit_bytes=VMEM_LIMIT),
        name="ssm",
    )(u_t, wst, kw, cp, wg, bg, a_chunk, a_seg)


def _outproj_body(n_cast, a_ref, s_ref, x_ref, ag_ref, sg_ref, w_ref, *refs):
    o_ref, sn_ref = refs[n_cast], refs[-1]
    for src, dst in zip(refs[:n_cast], refs[n_cast + 1:-1]):
        dst[...] = src[...].astype(_BF16)
    tm = x_ref.shape[0]
    an = _rms(a_ref[...].astype(_F32), ag_ref[...]).astype(_BF16)
    for t in range(SSM_CHUNK):
        sn_t = _rms(s_ref[t].astype(_F32), sg_ref[...])
        for s in range(SSM_WIDTH // LANES):
            sn_ref[s, pl.ds(t, tm // SSM_CHUNK, stride=SSM_CHUNK), :] = sn_t[:, s * LANES:(s + 1) * LANES]
    sn = jnp.concatenate([sn_ref[s] for s in range(SSM_WIDTH // LANES)], axis=1).astype(_BF16)
    mixed = jnp.concatenate([an, sn], axis=1)
    o_ref[...] = x_ref[...] + jnp.dot(mixed, w_ref[...], preferred_element_type=_F32)


def _outproj(attn, ssm_t, x, a_gain, s_gain, w_out, to_cast=(), *, tm=512):
    L, D = x.shape
    row = lambda i: (i, 0)
    fixed = lambda i: (0, 0)
    cast_specs, _, cast_shapes = _row_cast_specs(to_cast, L // tm)
    out = pl.pallas_call(
        functools.partial(_outproj_body, len(to_cast)),
        grid=(L // tm,),
        in_specs=[
            pl.BlockSpec((tm, ATTN_WIDTH), row),
            pl.BlockSpec((SSM_CHUNK, tm // SSM_CHUNK, SSM_WIDTH), lambda i: (0, i, 0)),
            pl.BlockSpec((tm, D), row),
            pl.BlockSpec((1, ATTN_WIDTH), fixed),
            pl.BlockSpec((1, SSM_WIDTH), fixed),
            pl.BlockSpec((ATTN_WIDTH + SSM_WIDTH, D), fixed),
        ] + cast_specs,
        out_specs=[pl.BlockSpec((tm, D), row)] + cast_specs,
        out_shape=[jax.ShapeDtypeStruct((L, D), _F32)] + cast_shapes,
        scratch_shapes=[pltpu.VMEM((SSM_WIDTH // LANES, tm, LANES), _F32)],
        compiler_params=pltpu.CompilerParams(
            dimension_semantics=("arbitrary",), vmem_limit_bytes=VMEM_LIMIT),
        name="outproj",
    )(attn, ssm_t, x, a_gain.reshape(1, -1), s_gain.reshape(1, -1), w_out, *to_cast)
    return out[0], out[1:]


def _layer(x, pos, invf, p):
    L = x.shape[0]
    seg = L // (SSM_SEGMENTS * SSM_CHUNK)
    w_gate1, w_up1 = _block_major_cast([p['ffn1_w_gate'], p['ffn1_w_up']])
    act, (w_down1,) = _ffn_up(x, p['ffn1_norm'], w_gate1, w_up1, [p['ffn1_w_down']])
    x, (w_in, w_out) = _ffn_down(act, w_down1, x, [p['w_in'], p['w_out']])
    q, k2, v2, u_t = _proj(x, p['mix_norm'], w_in, pos, invf, p['q_norm'], p['k_norm'])
    attn, (w_gate2, w_up2) = _attn(q, k2, v2, p['attn_sinks'], [p['ffn2_w_gate'], p['ffn2_w_up']], FFN_TF)
    weights = _ssm_weights(p['ssm_log_dt'], p['ssm_a_re'], p['ssm_a_im'], p['ssm_b_re'], p['ssm_b_im'],
                           p['ssm_c_re'], p['ssm_c_im'], p['ssm_d'], p['ssm_w_glu'], p['ssm_b_glu'], seg)
    ssm_t = _ssm(u_t, weights)
    x, (w_down2,) = _outproj(attn, ssm_t, x, p['attn_out_norm'], p['ssm_out_norm'], w_out,
                             [p['ffn2_w_down']])
    act, _ = _ffn_up(x, p['ffn2_norm'], w_gate2, w_up2)
    return _ffn_down(act, w_down2, x)[0]


def kernel(x, positions, ffn1_norm, ffn1_w_gate, ffn1_w_up, ffn1_w_down, mix_norm, w_in, q_norm, k_norm,
           attn_sinks, ssm_log_dt, ssm_a_re, ssm_a_im, ssm_b_re, ssm_b_im, ssm_c_re, ssm_c_im, ssm_d,
           ssm_w_glu, ssm_b_glu, attn_out_norm, ssm_out_norm, w_out, ffn2_norm, ffn2_w_gate, ffn2_w_up,
           ffn2_w_down):
    params = dict(
        ffn1_norm=ffn1_norm, ffn1_w_gate=ffn1_w_gate, ffn1_w_up=ffn1_w_up, ffn1_w_down=ffn1_w_down,
        mix_norm=mix_norm, w_in=w_in, q_norm=q_norm, k_norm=k_norm, attn_sinks=attn_sinks,
        ssm_log_dt=ssm_log_dt, ssm_a_re=ssm_a_re, ssm_a_im=ssm_a_im, ssm_b_re=ssm_b_re, ssm_b_im=ssm_b_im,
        ssm_c_re=ssm_c_re, ssm_c_im=ssm_c_im, ssm_d=ssm_d, ssm_w_glu=ssm_w_glu, ssm_b_glu=ssm_b_glu,
        attn_out_norm=attn_out_norm, ssm_out_norm=ssm_out_norm, w_out=w_out,
        ffn2_norm=ffn2_norm, ffn2_w_gate=ffn2_w_gate, ffn2_w_up=ffn2_w_up, ffn2_w_down=ffn2_w_down)
    depth = ffn1_norm.shape[0]
    half = HEAD_DIM // 2
    inv_freq = ROPE_THETA ** (-jnp.arange(half, dtype=_F32) * 2.0 / HEAD_DIM)
    invf = jnp.tile(inv_freq, LANES // half).reshape(1, LANES)
    outs = []
    for b in range(x.shape[0]):
        xb = x[b]
        for i in range(depth):
            xb = _layer(xb, positions[b], invf, {name: val[i] for name, val in params.items()})
        outs.append(xb)
    return jnp.stack(outs, axis=0)
```

```python
import functools
import math

import jax
import jax.numpy as jnp
from jax import lax
from jax.experimental import pallas as pl
from jax.experimental.pallas import tpu as pltpu

HEAD_DIM = 64
N_Q_HEADS = 16
N_KV_HEADS = 4
ATTN_WIDTH = N_Q_HEADS * HEAD_DIM
KV_WIDTH = N_KV_HEADS * HEAD_DIM
BLOCK = 128
ROPE_THETA = 10000.0
SSM_GROUP = 16
SSM_GROUPS = 64
SSM_STATE = 64
SSM_WIDTH = SSM_GROUP * SSM_GROUPS
FFN_RESIDUAL = 0.5
EPS = 1e-6

LANES = 128
SUBLANES = 8
MXU_DIM = 256
SSM_CHUNK = 8
SSM_SEGMENTS = SUBLANES
SLAB_GROUPS = LANES // SSM_GROUP
SLAB_STATES = SLAB_GROUPS * SSM_STATE
SCAN_UNROLL = 8
FFN_TF = 2 * MXU_DIM
VMEM_LIMIT = 56 * 1024 * 1024

_BF16 = jnp.bfloat16
_F32 = jnp.float32


def _rms(x, gain):
    ms = jnp.mean(x * x, axis=-1, keepdims=True)
    return x * lax.rsqrt(ms + EPS) * gain


def _ffn_up_body(n_cast, x_ref, gain_ref, wg_ref, wu_ref, *refs):
    src_refs, a_ref, dst_refs, h_ref = refs[:n_cast], refs[n_cast], refs[n_cast + 1:-1], refs[-1]
    first = pl.program_id(1) == 0

    @pl.when(first)
    def _():
        h_ref[...] = _rms(x_ref[...], gain_ref[...]).astype(_BF16)

    h = h_ref[...]
    g = jnp.dot(h, wg_ref[...], preferred_element_type=_F32)
    u = jnp.dot(h, wu_ref[...], preferred_element_type=_F32)
    a_ref[...] = (g * jax.nn.sigmoid(g) * u).astype(_BF16)

    for src, dst in zip(src_refs, dst_refs):
        dst[...] = src[...].astype(_BF16)


def _ffn_down_body(n_cast, a_ref, wd_ref, x_ref, *refs):
    o_ref = refs[n_cast]
    d = jnp.dot(a_ref[...], wd_ref[...], preferred_element_type=_F32)
    o_ref[...] = x_ref[...] + FFN_RESIDUAL * d
    for src, dst in zip(refs[:n_cast], refs[n_cast + 1:]):
        dst[...] = src[...].astype(_BF16)


def _row_cast_specs(arrays, n, col_block=None):
    pack = 2 * SUBLANES
    for w in arrays:
        assert w.shape[0] % (n * pack) == 0, w.shape
    in_specs = [pl.BlockSpec((w.shape[0] // n, w.shape[1]), lambda i: (i, 0)) for w in arrays]
    if col_block is None:
        return in_specs, in_specs, [jax.ShapeDtypeStruct(w.shape, _BF16) for w in arrays]
    out_specs = [pl.BlockSpec((w.shape[1] // col_block, w.shape[0] // n, col_block), lambda i: (0, i, 0))
                 for w in arrays]
    shapes = [jax.ShapeDtypeStruct((w.shape[1] // col_block, w.shape[0], col_block), _BF16) for w in arrays]
    return in_specs, out_specs, shapes


def _cast_block(src, dst):
    if len(dst.shape) == 2:
        dst[...] = src[...].astype(_BF16)
    else:
        for c in range(dst.shape[0]):
            dst[c] = src[:, c * dst.shape[2]:(c + 1) * dst.shape[2]].astype(_BF16)


def _step_cast_spec(shape, ni, nj):
    R, C = shape
    pack = 2 * SUBLANES
    if R % ni == 0 and (R // ni) % pack == 0 and C % nj == 0 and (C // nj) % LANES == 0:
        return pl.BlockSpec((R // ni, C // nj), lambda i, j: (i, j))
    assert R % (ni * nj) == 0 and (R // (ni * nj)) % pack == 0, shape
    return pl.BlockSpec((R // (ni * nj), C), lambda i, j: (i * nj + j, 0))


def _block_major_cast_body(*refs):
    n = len(refs) // 2
    for src, dst in zip(refs[:n], refs[n:]):
        dst[...] = src[...].astype(_BF16)


def _block_major_cast(weights):
    D, F = weights[0].shape
    return pl.pallas_call(
        _block_major_cast_body,
        grid=(F // FFN_TF,),
        in_specs=[pl.BlockSpec((D, FFN_TF), lambda j: (0, j)) for _ in weights],
        out_specs=[pl.BlockSpec((None, D, FFN_TF), lambda j: (j, 0, 0)) for _ in weights],
        out_shape=[jax.ShapeDtypeStruct((F // FFN_TF, D, FFN_TF), _BF16) for _ in weights],
        compiler_params=pltpu.CompilerParams(
            dimension_semantics=("arbitrary",), vmem_limit_bytes=VMEM_LIMIT),
        name="weight_cast",
    )(*weights)


def _ffn_up(x, gain, wg, wu, to_cast=(), *, tm=1024):
    L, D = x.shape
    nj, _, tf = wg.shape
    F = nj * tf
    tm = min(tm, L)
    ni = L // tm
    cast_specs = [_step_cast_spec(w.shape, ni, nj) for w in to_cast]
    out = pl.pallas_call(
        functools.partial(_ffn_up_body, len(to_cast)),
        grid=(ni, nj),
        in_specs=[
            pl.BlockSpec((tm, D), lambda i, j: (i, 0)),
            pl.BlockSpec((1, D), lambda i, j: (0, 0)),
            pl.BlockSpec((None, D, tf), lambda i, j: (j, 0, 0)),
            pl.BlockSpec((None, D, tf), lambda i, j: (j, 0, 0)),
        ] + cast_specs,
        out_specs=[pl.BlockSpec((tm, tf), lambda i, j: (i, j))] + cast_specs,
        out_shape=[jax.ShapeDtypeStruct((L, F), _BF16)]
        + [jax.ShapeDtypeStruct(w.shape, _BF16) for w in to_cast],
        scratch_shapes=[pltpu.VMEM((tm, D), _BF16)],
        compiler_params=pltpu.CompilerParams(
            dimension_semantics=("arbitrary", "arbitrary"), vmem_limit_bytes=VMEM_LIMIT),
        name="ffn_up",
    )(x, gain.reshape(1, D), wg, wu, *to_cast)
    return out[0], out[1:]


def _ffn_down(act, wd, x, to_cast=(), *, tm=512):
    L, D = x.shape
    F = act.shape[1]
    tm = min(tm, L)
    cast_specs, _, cast_shapes = _row_cast_specs(to_cast, L // tm)
    out = pl.pallas_call(
        functools.partial(_ffn_down_body, len(to_cast)),
        grid=(L // tm,),
        in_specs=[
            pl.BlockSpec((tm, F), lambda i: (i, 0)),
            pl.BlockSpec((F, D), lambda i: (0, 0), pipeline_mode=pl.Buffered(1)),
            pl.BlockSpec((tm, D), lambda i: (i, 0)),
        ] + cast_specs,
        out_specs=[pl.BlockSpec((tm, D), lambda i: (i, 0))] + cast_specs,
        out_shape=[jax.ShapeDtypeStruct((L, D), _F32)] + cast_shapes,
        compiler_params=pltpu.CompilerParams(
            dimension_semantics=("arbitrary",), vmem_limit_bytes=VMEM_LIMIT),
        name="ffn_down",
    )(act, wd, x, *to_cast)
    return out[0], out[1:]


def _proj_body(x_ref, gain_ref, w_ref, pos_ref, invf_ref, qg_ref, kg_ref, pn_ref,
               q_ref, k_ref, v_ref, u_ref, us_ref):
    tm = x_ref.shape[0]
    h = _rms(x_ref[...], gain_ref[...]).astype(_BF16)
    group = 2 * MXU_DIM

    def project(g):
        return jnp.dot(h, w_ref[:, g * group:(g + 1) * group], preferred_element_type=_F32)

    half = HEAD_DIM // 2
    ang = pos_ref[...] * invf_ref[...]
    lane_q = lax.broadcasted_iota(jnp.int32, ang.shape, 1) // half

    def spread(table):
        parts = []
        for qtr in range(LANES // half):
            m = jnp.where(lane_q == qtr, table, 0.0)
            parts.append(m + pltpu.roll(m, half, 1) + pltpu.roll(m, 2 * half, 1) + pltpu.roll(m, 3 * half, 1))
        return jnp.concatenate(parts, axis=0)

    cos = spread(jnp.cos(ang))
    sin = spread(jnp.sin(ang))
    lane = lax.broadcasted_iota(jnp.int32, (tm, LANES), 1)
    first_half = (lane & (HEAD_DIM // 2)) == 0
    low_head = lane < HEAD_DIM
    sin_signed = jnp.where(first_half, -sin, sin)

    def norm_rotary(x4, gain):
        ms4 = jnp.dot((x4 * x4).astype(_BF16), pn_ref[...], preferred_element_type=_F32)
        out = []
        for part in range(2):
            lanes = slice(part * LANES, (part + 1) * LANES)
            y = x4[:, lanes] * lax.rsqrt(ms4[:, lanes] + EPS) * gain
            swapped = jnp.where(first_half, pltpu.roll(y, LANES - HEAD_DIM // 2, 1),
                                pltpu.roll(y, HEAD_DIM // 2, 1))
            out.append(y * cos + swapped * sin_signed)
        return out

    def dup_heads(xc):
        r = pltpu.roll(xc, HEAD_DIM, 1)
        return jnp.where(low_head, xc, r), jnp.where(low_head, r, xc)

    scale = 1.0 / math.sqrt(HEAD_DIM)
    assert KV_WIDTH == MXU_DIM and 2 * KV_WIDTH == group
    for g in range(ATTN_WIDTH // group):
        pg = project(g)
        for c in range(group // MXU_DIM):
            for part, qc in enumerate(norm_rotary(pg[:, c * MXU_DIM:(c + 1) * MXU_DIM], qg_ref[...])):
                at = g * group + c * MXU_DIM + part * LANES
                q_ref[:, at:at + LANES] = (qc * scale).astype(_BF16)
    pg = project(ATTN_WIDTH // group)
    for part, kc in enumerate(norm_rotary(pg[:, :KV_WIDTH], kg_ref[...])):
        ka, kb = dup_heads(kc)
        k_ref[:, 2 * part * LANES:(2 * part + 1) * LANES] = ka.astype(_BF16)
        k_ref[:, (2 * part + 1) * LANES:(2 * part + 2) * LANES] = kb.astype(_BF16)
    for part in range(KV_WIDTH // LANES):
        va, vb = dup_heads(pg[:, KV_WIDTH + part * LANES:KV_WIDTH + (part + 1) * LANES])
        v_ref[:, 2 * part * LANES:(2 * part + 1) * LANES] = va.astype(_BF16)
        v_ref[:, (2 * part + 1) * LANES:(2 * part + 2) * LANES] = vb.astype(_BF16)
    first_u = (ATTN_WIDTH + 2 * KV_WIDTH) // group
    per_group = group // LANES
    for g in range(SSM_WIDTH // group):
        pg = project(first_u + g)
        for s in range(per_group):
            us_ref[g * per_group + s] = pg[:, s * LANES:(s + 1) * LANES]
        for t in range(SSM_CHUNK):
            for s in range(g * per_group, (g + 1) * per_group):
                rows = us_ref[s, pl.ds(t, tm // SSM_CHUNK, stride=SSM_CHUNK), :]
                u_ref[t, :, s * LANES:(s + 1) * LANES] = rows.astype(_BF16)


def _proj(x, gain, w_in, pos, invf, q_gain, k_gain, *, tm=512):
    L, D = x.shape
    C = w_in.shape[1]
    quarters = LANES // (HEAD_DIM // 2)
    pos_f = pos.astype(_F32).reshape(L // tm, quarters, tm // quarters).transpose(0, 2, 1)
    pos_f = jnp.repeat(pos_f, HEAD_DIM // 2, axis=2)
    head_of_lane = jnp.arange(MXU_DIM) // HEAD_DIM
    pn = jnp.where(head_of_lane[:, None] == head_of_lane[None, :], 1.0 / HEAD_DIM, 0.0).astype(_BF16)
    qg = jnp.tile(q_gain, LANES // HEAD_DIM).reshape(1, LANES)
    kg = jnp.tile(k_gain, LANES // HEAD_DIM).reshape(1, LANES)
    row = lambda i: (i, 0)
    fixed = lambda i: (0, 0)
    return pl.pallas_call(
        _proj_body,
        grid=(L // tm,),
        in_specs=[
            pl.BlockSpec((tm, D), row),
            pl.BlockSpec((1, D), fixed),
            pl.BlockSpec((D, C), fixed),
            pl.BlockSpec((None, tm // quarters, LANES), lambda i: (i, 0, 0)),
            pl.BlockSpec((1, LANES), fixed),
            pl.BlockSpec((1, LANES), fixed),
            pl.BlockSpec((1, LANES), fixed),
            pl.BlockSpec((MXU_DIM, MXU_DIM), fixed),
        ],
        out_specs=[
            pl.BlockSpec((tm, ATTN_WIDTH), row),
            pl.BlockSpec((tm, 2 * KV_WIDTH), row),
            pl.BlockSpec((tm, 2 * KV_WIDTH), row),
            pl.BlockSpec((SSM_CHUNK, tm // SSM_CHUNK, SSM_WIDTH), lambda i: (0, i, 0)),
        ],
        out_shape=[
            jax.ShapeDtypeStruct((L, ATTN_WIDTH), _BF16),
            jax.ShapeDtypeStruct((L, 2 * KV_WIDTH), _BF16),
            jax.ShapeDtypeStruct((L, 2 * KV_WIDTH), _BF16),
            jax.ShapeDtypeStruct((SSM_CHUNK, L // SSM_CHUNK, SSM_WIDTH), _BF16),
        ],
        scratch_shapes=[pltpu.VMEM((SSM_WIDTH // LANES, tm, LANES), _F32)],
        compiler_params=pltpu.CompilerParams(
            dimension_semantics=("arbitrary",), vmem_limit_bytes=VMEM_LIMIT),
        name="proj",
    )(x, gain.reshape(1, D), w_in, pos_f, invf, qg, kg, pn)


def _attn_body(n_cast, sink_ref, q_ref, kc_ref, vc_ref, kp_ref, vp_ref, *refs):
    o_ref = refs[n_cast]
    for src, dst in zip(refs[:n_cast], refs[n_cast + 1:]):
        _cast_block(src, dst)
    tq = q_ref.shape[0]
    kj = lax.broadcasted_iota(jnp.int32, (BLOCK, BLOCK), 0)
    qi = lax.broadcasted_iota(jnp.int32, (BLOCK, BLOCK), 1)
    from_prev = kj > qi
    has_prev = pl.program_id(0) > 0
    low_head = lax.broadcasted_iota(jnp.int32, (2 * BLOCK, LANES), 1) < HEAD_DIM
    nt = (((1,), (1,)), ((), ()))
    tn = (((0,), (0,)), ((), ()))

    for b in range(tq // BLOCK):
        rows = slice(b * BLOCK, (b + 1) * BLOCK)
        for hk in range(N_KV_HEADS):
            cols = slice(hk * LANES, (hk + 1) * LANES)
            if b == 0:
                kd = jnp.concatenate([kp_ref[:, cols], kc_ref[0:BLOCK, cols]], axis=0)
                vd = jnp.concatenate([vp_ref[:, cols], vc_ref[0:BLOCK, cols]], axis=0)
            else:
                kd = kc_ref[(b - 1) * BLOCK:(b + 1) * BLOCK, cols]
                vd = vc_ref[(b - 1) * BLOCK:(b + 1) * BLOCK, cols]
            zero = jnp.zeros_like(kd)
            k_half = (jnp.where(low_head, kd, zero), jnp.where(low_head, zero, kd))
            v_half = (jnp.where(low_head, vd, zero), jnp.where(low_head, zero, vd))
            for pp in range(2):
                pair = hk * 2 + pp
                qp = q_ref[rows, pair * LANES:(pair + 1) * LANES]
                acc = None
                for half in range(2):
                    sink = sink_ref[pair * 2 + half]
                    s2 = lax.dot_general(k_half[half], qp, nt, preferred_element_type=_F32)
                    s_prev = s2[:BLOCK]
                    if b == 0:
                        s_prev = jnp.where(has_prev, s_prev, -jnp.inf)
                    s = jnp.where(from_prev, s_prev, s2[BLOCK:])
                    m = jnp.maximum(jnp.max(s, axis=0, keepdims=True), sink)
                    p = jnp.exp(s - m)
                    den = jnp.sum(p, axis=0, keepdims=True) + jnp.exp(sink - m)
                    pb = p.astype(_BF16)
                    pz = jnp.zeros_like(pb)
                    p2 = jnp.concatenate([jnp.where(from_prev, pb, pz), jnp.where(from_prev, pz, pb)], axis=0)
                    o = lax.dot_general(v_half[half], p2, tn, preferred_element_type=_F32) * (1.0 / den)
                    acc = o if acc is None else acc + o
                o_ref[rows, pair * LANES:(pair + 1) * LANES] = jnp.transpose(acc).astype(_BF16)


def _attn(q, k2, v2, sinks, to_cast=(), cast_col_block=None, *, tq=1024):
    L = q.shape[0]
    per = tq // BLOCK
    row = lambda i: (i, 0)
    prev = lambda i: (jnp.maximum(i * per - 1, 0), 0)
    cast_specs, cast_out_specs, cast_shapes = _row_cast_specs(to_cast, L // tq, cast_col_block)
    out = pl.pallas_call(
        functools.partial(_attn_body, len(to_cast)),
        grid=(L // tq,),
        in_specs=[
            pl.BlockSpec(memory_space=pltpu.SMEM),
            pl.BlockSpec((tq, ATTN_WIDTH), row),
            pl.BlockSpec((tq, 2 * KV_WIDTH), row),
            pl.BlockSpec((tq, 2 * KV_WIDTH), row),
            pl.BlockSpec((BLOCK, 2 * KV_WIDTH), prev),
            pl.BlockSpec((BLOCK, 2 * KV_WIDTH), prev),
        ] + cast_specs,
        out_specs=[pl.BlockSpec((tq, ATTN_WIDTH), row)] + cast_out_specs,
        out_shape=[jax.ShapeDtypeStruct((L, ATTN_WIDTH), _BF16)] + cast_shapes,
        compiler_params=pltpu.CompilerParams(
            dimension_semantics=("arbitrary",), vmem_limit_bytes=VMEM_LIMIT),
        name="attn",
    )(sinks, q, k2, v2, k2, v2, *to_cast)
    return out[0], out[1:]


def _ssm_body(u_ref, wst_ref, kw_ref, cp_ref, wg_ref, bg_ref, a_ref, aseg_ref, o_ref,
              lhs_ref, toep_ref, s_ref, xb_ref):
    T = SSM_CHUNK
    n_c = u_ref.shape[1]
    seg = n_c // SSM_SEGMENTS
    pitch = s_ref.shape[1] // SSM_SEGMENTS
    n_state_slabs = SLAB_STATES // LANES
    pair_w = 2 * LANES
    out_w = 2 * MXU_DIM

    for t in range(T):
        lhs_ref[:, t * LANES:(t + 1) * LANES] = u_ref[t]

    toep_ref[...] = jnp.zeros(toep_ref.shape, _BF16)
    for t in range(T):
        for tp in range(t, T):
            toep_ref[t * LANES:(t + 1) * LANES, tp * LANES:(tp + 1) * LANES] = (
                kw_ref[:, (tp - t) * LANES:(tp - t + 1) * LANES])

    per_dot = out_w // LANES
    for nb in range(2 * n_state_slabs // per_dot):
        res = jnp.dot(lhs_ref[...], wst_ref[:, nb * out_w:(nb + 1) * out_w], preferred_element_type=_F32)
        for part in range(per_dot):
            for j in range(SSM_SEGMENTS):
                s_ref[per_dot * nb + part, j * pitch:j * pitch + seg, :] = (
                    res[j * seg:(j + 1) * seg, part * LANES:(part + 1) * LANES])

    shape = (SSM_SEGMENTS, LANES)
    a_re = [jnp.broadcast_to(a_ref[0:1, k * LANES:(k + 1) * LANES], shape) for k in range(n_state_slabs)]
    a_im = [jnp.broadcast_to(a_ref[1:2, k * LANES:(k + 1) * LANES], shape) for k in range(n_state_slabs)]
    g_re = [jnp.broadcast_to(aseg_ref[0:1, k * LANES:(k + 1) * LANES], shape) for k in range(n_state_slabs)]
    g_im = [jnp.broadcast_to(aseg_ref[1:2, k * LANES:(k + 1) * LANES], shape) for k in range(n_state_slabs)]

    def seg_rows(w):
        return pl.ds(w, SSM_SEGMENTS, stride=pitch)

    def advance(w, carry, store):
        out = []
        for k in range(n_state_slabs):
            z_re, z_im = carry[2 * k], carry[2 * k + 1]
            s_re = s_ref[k, seg_rows(w), :]
            s_im = s_ref[n_state_slabs + k, seg_rows(w), :]
            if store:
                s_ref[k, seg_rows(w), :] = z_re
                s_ref[n_state_slabs + k, seg_rows(w), :] = z_im
            out.append(a_re[k] * z_re - a_im[k] * z_im + s_re)
            out.append(a_re[k] * z_im + a_im[k] * z_re + s_im)
        return tuple(out)

    zero = jnp.zeros(shape, _F32)
    ends = lax.fori_loop(0, seg, lambda w, c: advance(w, c, False), (zero,) * (2 * n_state_slabs),
                         unroll=SCAN_UNROLL)

    segidx = lax.broadcasted_iota(jnp.int32, shape, 0)
    init = []
    for k in range(n_state_slabs):
        i_re, i_im = zero, zero
        f_re, f_im = ends[2 * k], ends[2 * k + 1]
        for j in range(SSM_SEGMENTS - 1):
            c_re = g_re[k] * i_re - g_im[k] * i_im + f_re
            c_im = g_re[k] * i_im + g_im[k] * i_re + f_im
            i_re = jnp.where(segidx == j + 1, pltpu.roll(c_re, 1, 0), i_re)
            i_im = jnp.where(segidx == j + 1, pltpu.roll(c_im, 1, 0), i_im)
        init += [i_re, i_im]

    lax.fori_loop(0, seg, lambda w, c: advance(w, c, True), tuple(init), unroll=SCAN_UNROLL)

    for col in range(2 * n_state_slabs):
        for j in range(SSM_SEGMENTS):
            xb_ref[j * seg:(j + 1) * seg, col * LANES:(col + 1) * LANES] = (
                s_ref[col, j * pitch:j * pitch + seg, :].astype(_BF16))

    steps = out_w // LANES
    for i in range(T // steps):
        kk = (i + 1) * out_w
        cols = slice(i * out_w, (i + 1) * out_w)
        y = (jnp.dot(lhs_ref[:, :kk], toep_ref[:kk, cols], preferred_element_type=_F32)
             + jnp.dot(xb_ref[...], cp_ref[:, cols], preferred_element_type=_F32))
        z = jax.nn.gelu(y, approximate=True).astype(_BF16)
        for pp in range(steps // 2):
            gt = jnp.dot(z[:, pp * pair_w:(pp + 1) * pair_w], wg_ref[...],
                         preferred_element_type=_F32) + bg_ref[...]
            out = gt[:, :pair_w] * jax.nn.sigmoid(gt[:, pair_w:])
            t0 = i * steps + 2 * pp
            o_ref[t0] = out[:, :LANES].astype(_BF16)
            o_ref[t0 + 1] = out[:, LANES:].astype(_BF16)


def _complex_power(re, im, n):
    out_re, out_im = None, None
    while n:
        if n & 1:
            if out_re is None:
                out_re, out_im = re, im
            else:
                out_re, out_im = out_re * re - out_im * im, out_re * im + out_im * re
        n >>= 1
        if n:
            re, im = re * re - im * im, 2.0 * re * im
    return out_re, out_im


def _ssm_prep_body(seg, rows_ref, bt_ref, ct_ref, d_ref, wt_ref, bglu_ref,
                   wst_ref, kw_ref, cp_ref, wg_ref, bg_ref, ach_ref, aseg_ref):
    T, H = SSM_CHUNK, SSM_GROUP
    a_re, a_im = rows_ref[0:1, :], rows_ref[1:2, :]
    dt = jnp.exp(rows_ref[2:3, :])
    mag = jnp.exp(a_re * dt)
    ab_re = mag * jnp.cos(a_im * dt)
    ab_im = mag * jnp.sin(a_im * dt)
    inv_den = 1.0 / (a_re * a_re + a_im * a_im)
    nr = ab_re - 1.0
    coef_re = (nr * a_re + ab_im * a_im) * inv_den
    coef_im = (ab_im * a_re - nr * a_im) * inv_den

    def block_diag(t16, shape):
        row_g = lax.broadcasted_iota(jnp.int32, shape, 0) // H
        col_g = lax.broadcasted_iota(jnp.int32, shape, 1) // (shape[1] // SLAB_GROUPS)
        return jnp.where(row_g == col_g, jnp.concatenate([t16] * SLAB_GROUPS, axis=0), 0.0)

    cs = (LANES, SLAB_STATES)
    b_re, b_im = block_diag(bt_ref[0], cs), block_diag(bt_ref[1], cs)
    bb_re = coef_re * b_re - coef_im * b_im
    bb_im = coef_re * b_im + coef_im * b_re
    ct_re, ct_im = block_diag(ct_ref[0], cs), block_diag(ct_ref[1], cs)

    pw = [(jnp.ones_like(ab_re), jnp.zeros_like(ab_im))]
    for _ in range(T):
        pr, pi = pw[-1]
        pw.append((pr * ab_re - pi * ab_im, pr * ab_im + pi * ab_re))

    for t in range(T):
        pr, pi = pw[T - 1 - t]
        wst_ref[t * LANES:(t + 1) * LANES, :SLAB_STATES] = (bb_re * pr - bb_im * pi).astype(_BF16)
        wst_ref[t * LANES:(t + 1) * LANES, SLAB_STATES:] = (bb_re * pi + bb_im * pr).astype(_BF16)

    lags = []
    for k in range(T + 1):
        if k < T:
            lags.append(jnp.concatenate([ct_re, ct_im], axis=1))
        if k > 0:
            cp_ref[:SLAB_STATES, (k - 1) * LANES:k * LANES] = jnp.transpose(ct_re).astype(_BF16)
            cp_ref[SLAB_STATES:, (k - 1) * LANES:k * LANES] = jnp.transpose(-ct_im).astype(_BF16)
        ct_re, ct_im = ct_re * ab_re - ct_im * ab_im, ct_re * ab_im + ct_im * ab_re

    def split(a):
        hi = a.astype(_BF16)
        return hi, (a - hi.astype(_F32)).astype(_BF16)

    nt = (((1,), (1,)), ((), ()))
    a_hi, a_lo = split(jnp.concatenate(lags, axis=0))
    b_hi, b_lo = split(jnp.concatenate([bb_re, -bb_im], axis=1))
    kern_t = (lax.dot_general(a_hi, b_hi, nt, preferred_element_type=_F32)
              + lax.dot_general(a_hi, b_lo, nt, preferred_element_type=_F32)
              + lax.dot_general(a_lo, b_hi, nt, preferred_element_type=_F32))
    kern = jnp.transpose(kern_t)
    eye = (lax.broadcasted_iota(jnp.int32, (LANES, LANES), 0)
           == lax.broadcasted_iota(jnp.int32, (LANES, LANES), 1))
    kw_ref[:, :LANES] = (kern[:, :LANES] + jnp.where(eye, d_ref[...], 0.0)).astype(_BF16)
    kw_ref[:, LANES:] = kern[:, LANES:].astype(_BF16)

    cc = (LANES, LANES)
    w_lin = jnp.transpose(block_diag(wt_ref[0], cc)).astype(_BF16)
    w_gate = jnp.transpose(block_diag(wt_ref[1], cc)).astype(_BF16)
    wg_ref[...] = jnp.zeros(wg_ref.shape, _BF16)
    for t in range(2):
        wg_ref[t * LANES:(t + 1) * LANES, t * LANES:(t + 1) * LANES] = w_lin
        wg_ref[t * LANES:(t + 1) * LANES, (2 + t) * LANES:(3 + t) * LANES] = w_gate
    bg_ref[...] = jnp.concatenate([bglu_ref[0:1, :], bglu_ref[0:1, :], bglu_ref[1:2, :], bglu_ref[1:2, :]],
                                  axis=1)

    ach_ref[0:1, :], ach_ref[1:2, :] = pw[T]
    aseg_ref[0:1, :], aseg_ref[1:2, :] = _complex_power(pw[T][0], pw[T][1], seg)


def _ssm_weights(log_dt, a_re, a_im, b_re, b_im, c_re, c_im, d_skip, w_glu, b_glu, seg):
    P, H, T = SSM_STATE, SSM_GROUP, SSM_CHUNK
    SG, NS = SLAB_GROUPS, SSM_GROUPS // SLAB_GROUPS
    rows = jnp.stack([a_re.reshape(NS, SG * P), a_im.reshape(NS, SG * P),
                      jnp.repeat(log_dt, P).reshape(NS, SG * P)], axis=1)

    def per_group_rows(w, lead):
        t = w.reshape((NS, SG) + w.shape[1:])
        t = jnp.moveaxis(t, 2 + lead, 1)
        return t.reshape(NS, t.shape[1], -1)

    bt = jnp.stack([per_group_rows(b_re, 1), per_group_rows(b_im, 1)], axis=1)
    ct = jnp.stack([per_group_rows(c_re, 0), per_group_rows(c_im, 0)], axis=1)
    wt = jnp.stack([per_group_rows(w_glu[..., :H], 1), per_group_rows(w_glu[..., H:], 1)], axis=1)
    bglu = jnp.stack([b_glu[:, :H].reshape(NS, SG * H), b_glu[:, H:].reshape(NS, SG * H)], axis=1)
    d = d_skip.reshape(NS, 1, SG * H)

    blk3 = lambda i: (i, 0, 0)
    blk4 = lambda i: (i, 0, 0, 0)
    return pl.pallas_call(
        functools.partial(_ssm_prep_body, seg),
        grid=(NS,),
        in_specs=[
            pl.BlockSpec((None, 3, SG * P), blk3),
            pl.BlockSpec((None, 2, H, SG * P), blk4),
            pl.BlockSpec((None, 2, H, SG * P), blk4),
            pl.BlockSpec((None, 1, SG * H), blk3),
            pl.BlockSpec((None, 2, H, SG * H), blk4),
            pl.BlockSpec((None, 2, SG * H), blk3),
        ],
        out_specs=[
            pl.BlockSpec((None, T * LANES, 2 * SLAB_STATES), blk3),
            pl.BlockSpec((None, LANES, T * LANES), blk3),
            pl.BlockSpec((None, 2 * SLAB_STATES, T * LANES), blk3),
            pl.BlockSpec((None, 2 * LANES, 4 * LANES), blk3),
            pl.BlockSpec((None, 1, 4 * LANES), blk3),
            pl.BlockSpec((None, 2, SLAB_STATES), blk3),
            pl.BlockSpec((None, 2, SLAB_STATES), blk3),
        ],
        out_shape=[
            jax.ShapeDtypeStruct((NS, T * LANES, 2 * SLAB_STATES), _BF16),
            jax.ShapeDtypeStruct((NS, LANES, T * LANES), _BF16),
            jax.ShapeDtypeStruct((NS, 2 * SLAB_STATES, T * LANES), _BF16),
            jax.ShapeDtypeStruct((NS, 2 * LANES, 4 * LANES), _BF16),
            jax.ShapeDtypeStruct((NS, 1, 4 * LANES), _F32),
            jax.ShapeDtypeStruct((NS, 2, SLAB_STATES), _F32),
            jax.ShapeDtypeStruct((NS, 2, SLAB_STATES), _F32),
        ],
        compiler_params=pltpu.CompilerParams(
            dimension_semantics=("arbitrary",), vmem_limit_bytes=VMEM_LIMIT),
        name="ssm_prep",
    )(rows, bt, ct, d, wt, bglu)


def _ssm(u_t, weights):
    T, n_c, W = u_t.shape
    seg = n_c // SSM_SEGMENTS
    pitch = seg + SUBLANES
    ns = W // LANES
    wst, kw, cp, wg, bg, a_chunk, a_seg = weights
    slab = lambda i: (0, 0, i)
    blk = lambda i: (i, 0, 0)
    return pl.pallas_call(
        _ssm_body,
        grid=(ns,),
        in_specs=[
            pl.BlockSpec((T, n_c, LANES), slab),
            pl.BlockSpec((None, T * LANES, 2 * SLAB_STATES), blk),
            pl.BlockSpec((None, LANES, T * LANES), blk),
            pl.BlockSpec((None, 2 * SLAB_STATES, T * LANES), blk),
            pl.BlockSpec((None, 2 * LANES, 4 * LANES), blk),
            pl.BlockSpec((None, 1, 4 * LANES), blk),
            pl.BlockSpec((None, 2, SLAB_STATES), blk),
            pl.BlockSpec((None, 2, SLAB_STATES), blk),
        ],
        out_specs=pl.BlockSpec((T, n_c, LANES), slab),
        out_shape=jax.ShapeDtypeStruct((T, n_c, W), _BF16),
        scratch_shapes=[
            pltpu.VMEM((n_c, T * LANES), _BF16),
            pltpu.VMEM((T * LANES, T * LANES), _BF16),
            pltpu.VMEM((2 * SLAB_STATES // LANES, SSM_SEGMENTS * pitch, LANES), _F32),
            pltpu.VMEM((n_c, 2 * SLAB_STATES), _BF16),
        ],
        compiler_params=pltpu.CompilerParams(
            dimension_semantics=("arbitrary",), vmem_limit_bytes=VMEM_LIMIT),
        name="ssm",
    )(u_t, wst, kw, cp, wg, bg, a_chunk, a_seg)


def _outproj_body(n_cast, a_ref, s_ref, x_ref, ag_ref, sg_ref, w_ref, *refs):
    o_ref, sn_ref = refs[n_cast], refs[-1]
    for src, dst in zip(refs[:n_cast], refs[n_cast + 1:-1]):
        dst[...] = src[...].astype(_BF16)
    tm = x_ref.shape[0]
    an = _rms(a_ref[...].astype(_F32), ag_ref[...]).astype(_BF16)
    for t in range(SSM_CHUNK):
        sn_t = _rms(s_ref[t].astype(_F32), sg_ref[...])
        for s in range(SSM_WIDTH // LANES):
            sn_ref[s, pl.ds(t, tm // SSM_CHUNK, stride=SSM_CHUNK), :] = sn_t[:, s * LANES:(s + 1) * LANES]
    sn = jnp.concatenate([sn_ref[s] for s in range(SSM_WIDTH // LANES)], axis=1).astype(_BF16)
    mixed = jnp.concatenate([an, sn], axis=1)
    o_ref[...] = x_ref[...] + jnp.dot(mixed, w_ref[...], preferred_element_type=_F32)


def _outproj(attn, ssm_t, x, a_gain, s_gain, w_out, to_cast=(), *, tm=512):
    L, D = x.shape
    row = lambda i: (i, 0)
    fixed = lambda i: (0, 0)
    cast_specs, _, cast_shapes = _row_cast_specs(to_cast, L // tm)
    out = pl.pallas_call(
        functools.partial(_outproj_body, len(to_cast)),
        grid=(L // tm,),
        in_specs=[
            pl.BlockSpec((tm, ATTN_WIDTH), row),
            pl.BlockSpec((SSM_CHUNK, tm // SSM_CHUNK, SSM_WIDTH), lambda i: (0, i, 0)),
            pl.BlockSpec((tm, D), row),
            pl.BlockSpec((1, ATTN_WIDTH), fixed),
            pl.BlockSpec((1, SSM_WIDTH), fixed),
            pl.BlockSpec((ATTN_WIDTH + SSM_WIDTH, D), fixed),
        ] + cast_specs,
        out_specs=[pl.BlockSpec((tm, D), row)] + cast_specs,
        out_shape=[jax.ShapeDtypeStruct((L, D), _F32)] + cast_shapes,
        scratch_shapes=[pltpu.VMEM((SSM_WIDTH // LANES, tm, LANES), _F32)],
        compiler_params=pltpu.CompilerParams(
            dimension_semantics=("arbitrary",), vmem_limit_bytes=VMEM_LIMIT),
        name="outproj",
    )(attn, ssm_t, x, a_gain.reshape(1, -1), s_gain.reshape(1, -1), w_out, *to_cast)
    return out[0], out[1:]


def _layer(x, pos, invf, p):
    L = x.shape[0]
    seg = L // (SSM_SEGMENTS * SSM_CHUNK)
    w_gate1, w_up1 = _block_major_cast([p['ffn1_w_gate'], p['ffn1_w_up']])
    act, (w_down1,) = _ffn_up(x, p['ffn1_norm'], w_gate1, w_up1, [p['ffn1_w_down']])
    x, (w_in, w_out) = _ffn_down(act, w_down1, x, [p['w_in'], p['w_out']])
    q, k2, v2, u_t = _proj(x, p['mix_norm'], w_in, pos, invf, p['q_norm'], p['k_norm'])
    attn, (w_gate2, w_up2) = _attn(q, k2, v2, p['attn_sinks'], [p['ffn2_w_gate'], p['ffn2_w_up']], FFN_TF)
    weights = _ssm_weights(p['ssm_log_dt'], p['ssm_a_re'], p['ssm_a_im'], p['ssm_b_re'], p['ssm_b_im'],
                           p['ssm_c_re'], p['ssm_c_im'], p['ssm_d'], p['ssm_w_glu'], p['ssm_b_glu'], seg)
    ssm_t = _ssm(u_t, weights)
    x, (w_down2,) = _outproj(attn, ssm_t, x, p['attn_out_norm'], p['ssm_out_norm'], w_out,
                             [p['ffn2_w_down']])
    act, _ = _ffn_up(x, p['ffn2_norm'], w_gate2, w_up2)
    return _ffn_down(act, w_down2, x)[0]


def kernel(x, positions, ffn1_norm, ffn1_w_gate, ffn1_w_up, ffn1_w_down, mix_norm, w_in, q_norm, k_norm,
           attn_sinks, ssm_log_dt, ssm_a_re, ssm_a_im, ssm_b_re, ssm_b_im, ssm_c_re, ssm_c_im, ssm_d,
           ssm_w_glu, ssm_b_glu, attn_out_norm, ssm_out_norm, w_out, ffn2_norm, ffn2_w_gate, ffn2_w_up,
           ffn2_w_down):
    params = dict(
        ffn1_norm=ffn1_norm, ffn1_w_gate=ffn1_w_gate, ffn1_w_up=ffn1_w_up, ffn1_w_down=ffn1_w_down,
        mix_norm=mix_norm, w_in=w_in, q_norm=q_norm, k_norm=k_norm, attn_sinks=attn_sinks,
        ssm_log_dt=ssm_log_dt, ssm_a_re=ssm_a_re, ssm_a_im=ssm_a_im, ssm_b_re=ssm_b_re, ssm_b_im=ssm_b_im,
        ssm_c_re=ssm_c_re, ssm_c_im=ssm_c_im, ssm_d=ssm_d, ssm_w_glu=ssm_w_glu, ssm_b_glu=ssm_b_glu,
        attn_out_norm=attn_out_norm, ssm_out_norm=ssm_out_norm, w_out=w_out,
        ffn2_norm=ffn2_norm, ffn2_w_gate=ffn2_w_gate, ffn2_w_up=ffn2_w_up, ffn2_w_down=ffn2_w_down)
    depth = ffn1_norm.shape[0]
    half = HEAD_DIM // 2
    inv_freq = ROPE_THETA ** (-jnp.arange(half, dtype=_F32) * 2.0 / HEAD_DIM)
    invf = jnp.tile(inv_freq, LANES // half).reshape(1, LANES)
    outs = []
    for b in range(x.shape[0]):
        xb = x[b]
        for i in range(depth):
            xb = _layer(xb, positions[b], invf, {name: val[i] for name, val in params.items()})
        outs.append(xb)
    return jnp.stack(outs, axis=0)
```

```python
import functools
import math

import jax
import jax.numpy as jnp
from jax import lax
from jax.experimental import pallas as pl
from jax.experimental.pallas import tpu as pltpu

HEAD_DIM = 64
N_Q_HEADS = 16
N_KV_HEADS = 4
ATTN_WIDTH = N_Q_HEADS * HEAD_DIM
KV_WIDTH = N_KV_HEADS * HEAD_DIM
BLOCK = 128
ROPE_THETA = 10000.0
SSM_GROUP = 16
SSM_GROUPS = 64
SSM_STATE = 64
SSM_WIDTH = SSM_GROUP * SSM_GROUPS
FFN_RESIDUAL = 0.5
EPS = 1e-6

LANES = 128
SUBLANES = 8
MXU_DIM = 256
SSM_CHUNK = 8
SSM_SEGMENTS = SUBLANES
SLAB_GROUPS = LANES // SSM_GROUP
SLAB_STATES = SLAB_GROUPS * SSM_STATE
SCAN_UNROLL = 8
FFN_TF = 2 * MXU_DIM
VMEM_LIMIT = 56 * 1024 * 1024

_BF16 = jnp.bfloat16
_F32 = jnp.float32


def _rms(x, gain):
    ms = jnp.mean(x * x, axis=-1, keepdims=True)
    return x * lax.rsqrt(ms + EPS) * gain


def _ffn_up_body(n_cast, x_ref, gain_ref, wg_ref, wu_ref, *refs):
    src_refs, a_ref, dst_refs, h_ref = refs[:n_cast], refs[n_cast], refs[n_cast + 1:-1], refs[-1]
    first = pl.program_id(1) == 0

    @pl.when(first)
    def _():
        h_ref[...] = _rms(x_ref[...], gain_ref[...]).astype(_BF16)

    h = h_ref[...]
    g = jnp.dot(h, wg_ref[...], preferred_element_type=_F32)
    u = jnp.dot(h, wu_ref[...], preferred_element_type=_F32)
    a_ref[...] = (g * jax.nn.sigmoid(g) * u).astype(_BF16)

    for src, dst in zip(src_refs, dst_refs):
        dst[...] = src[...].astype(_BF16)


def _ffn_down_body(n_cast, a_ref, wd_ref, x_ref, *refs):
    o_ref = refs[n_cast]
    d = jnp.dot(a_ref[...], wd_ref[...], preferred_element_type=_F32)
    o_ref[...] = x_ref[...] + FFN_RESIDUAL * d
    for src, dst in zip(refs[:n_cast], refs[n_cast + 1:]):
        dst[...] = src[...].astype(_BF16)


def _row_cast_specs(arrays, n, col_block=None):
    pack = 2 * SUBLANES
    for w in arrays:
        assert w.shape[0] % (n * pack) == 0, w.shape
    in_specs = [pl.BlockSpec((w.shape[0] // n, w.shape[1]), lambda i: (i, 0)) for w in arrays]
    if col_block is None:
        return in_specs, in_specs, [jax.ShapeDtypeStruct(w.shape, _BF16) for w in arrays]
    out_specs = [pl.BlockSpec((w.shape[1] // col_block, w.shape[0] // n, col_block), lambda i: (0, i, 0))
                 for w in arrays]
    shapes = [jax.ShapeDtypeStruct((w.shape[1] // col_block, w.shape[0], col_block), _BF16) for w in arrays]
    return in_specs, out_specs, shapes


def _cast_block(src, dst):
    if len(dst.shape) == 2:
        dst[...] = src[...].astype(_BF16)
    else:
        for c in range(dst.shape[0]):
            dst[c] = src[:, c * dst.shape[2]:(c + 1) * dst.shape[2]].astype(_BF16)


def _step_cast_spec(shape, ni, nj):
    R, C = shape
    pack = 2 * SUBLANES
    if R % ni == 0 and (R // ni) % pack == 0 and C % nj == 0 and (C // nj) % LANES == 0:
        return pl.BlockSpec((R // ni, C // nj), lambda i, j: (i, j))
    assert R % (ni * nj) == 0 and (R // (ni * nj)) % pack == 0, shape
    return pl.BlockSpec((R // (ni * nj), C), lambda i, j: (i * nj + j, 0))


def _block_major_cast_body(*refs):
    n = len(refs) // 2
    for src, dst in zip(refs[:n], refs[n:]):
        dst[...] = src[...].astype(_BF16)


def _block_major_cast(weights):
    D, F = weights[0].shape
    return pl.pallas_call(
        _block_major_cast_body,
        grid=(F // FFN_TF,),
        in_specs=[pl.BlockSpec((D, FFN_TF), lambda j: (0, j)) for _ in weights],
        out_specs=[pl.BlockSpec((None, D, FFN_TF), lambda j: (j, 0, 0)) for _ in weights],
        out_shape=[jax.ShapeDtypeStruct((F // FFN_TF, D, FFN_TF), _BF16) for _ in weights],
        compiler_params=pltpu.CompilerParams(
            dimension_semantics=("arbitrary",), vmem_limit_bytes=VMEM_LIMIT),
        name="weight_cast",
    )(*weights)


def _ffn_up(x, gain, wg, wu, to_cast=(), *, tm=1024):
    L, D = x.shape
    nj, _, tf = wg.shape
    F = nj * tf
    tm = min(tm, L)
    ni = L // tm
    cast_specs = [_step_cast_spec(w.shape, ni, nj) for w in to_cast]
    out = pl.pallas_call(
        functools.partial(_ffn_up_body, len(to_cast)),
        grid=(ni, nj),
        in_specs=[
            pl.BlockSpec((tm, D), lambda i, j: (i, 0)),
            pl.BlockSpec((1, D), lambda i, j: (0, 0)),
            pl.BlockSpec((None, D, tf), lambda i, j: (j, 0, 0)),
            pl.BlockSpec((None, D, tf), lambda i, j: (j, 0, 0)),
        ] + cast_specs,
        out_specs=[pl.BlockSpec((tm, tf), lambda i, j: (i, j))] + cast_specs,
        out_shape=[jax.ShapeDtypeStruct((L, F), _BF16)]
        + [jax.ShapeDtypeStruct(w.shape, _BF16) for w in to_cast],
        scratch_shapes=[pltpu.VMEM((tm, D), _BF16)],
        compiler_params=pltpu.CompilerParams(
            dimension_semantics=("arbitrary", "arbitrary"), vmem_limit_bytes=VMEM_LIMIT),
        name="ffn_up",
    )(x, gain.reshape(1, D), wg, wu, *to_cast)
    return out[0], out[1:]


def _ffn_down(act, wd, x, to_cast=(), *, tm=512):
    L, D = x.shape
    F = act.shape[1]
    tm = min(tm, L)
    cast_specs, _, cast_shapes = _row_cast_specs(to_cast, L // tm)
    out = pl.pallas_call(
        functools.partial(_ffn_down_body, len(to_cast)),
        grid=(L // tm,),
        in_specs=[
            pl.BlockSpec((tm, F), lambda i: (i, 0)),
            pl.BlockSpec((F, D), lambda i: (0, 0), pipeline_mode=pl.Buffered(1)),
            pl.BlockSpec((tm, D), lambda i: (i, 0)),
        ] + cast_specs,
        out_specs=[pl.BlockSpec((tm, D), lambda i: (i, 0))] + cast_specs,
        out_shape=[jax.ShapeDtypeStruct((L, D), _F32)] + cast_shapes,
        compiler_params=pltpu.CompilerParams(
            dimension_semantics=("arbitrary",), vmem_limit_bytes=VMEM_LIMIT),
        name="ffn_down",
    )(act, wd, x, *to_cast)
    return out[0], out[1:]


def _proj_body(x_ref, gain_ref, w_ref, pos_ref, invf_ref, qg_ref, kg_ref, pn_ref,
               q_ref, k_ref, v_ref, u_ref, us_ref):
    tm = x_ref.shape[0]
    h = _rms(x_ref[...], gain_ref[...]).astype(_BF16)
    group = 2 * MXU_DIM

    def project(g):
        return jnp.dot(h, w_ref[:, g * group:(g + 1) * group], preferred_element_type=_F32)

    half = HEAD_DIM // 2
    ang = pos_ref[...] * invf_ref[...]
    lane_q = lax.broadcasted_iota(jnp.int32, ang.shape, 1) // half

    def spread(table):
        parts = []
        for qtr in range(LANES // half):
            m = jnp.where(lane_q == qtr, table, 0.0)
            parts.append(m + pltpu.roll(m, half, 1) + pltpu.roll(m, 2 * half, 1) + pltpu.roll(m, 3 * half, 1))
        return jnp.concatenate(parts, axis=0)

    cos = spread(jnp.cos(ang))
    sin = spread(jnp.sin(ang))
    lane = lax.broadcasted_iota(jnp.int32, (tm, LANES), 1)
    first_half = (lane & (HEAD_DIM // 2)) == 0
    low_head = lane < HEAD_DIM
    sin_signed = jnp.where(first_half, -sin, sin)

    def norm_rotary(x4, gain):
        ms4 = jnp.dot((x4 * x4).astype(_BF16), pn_ref[...], preferred_element_type=_F32)
        out = []
        for part in range(2):
            lanes = slice(part * LANES, (part + 1) * LANES)
            y = x4[:, lanes] * lax.rsqrt(ms4[:, lanes] + EPS) * gain
            swapped = jnp.where(first_half, pltpu.roll(y, LANES - HEAD_DIM // 2, 1),
                                pltpu.roll(y, HEAD_DIM // 2, 1))
            out.append(y * cos + swapped * sin_signed)
        return out

    def dup_heads(xc):
        r = pltpu.roll(xc, HEAD_DIM, 1)
        return jnp.where(low_head, xc, r), jnp.where(low_head, r, xc)

    scale = 1.0 / math.sqrt(HEAD_DIM)
    assert KV_WIDTH == MXU_DIM and 2 * KV_WIDTH == group
    for g in range(ATTN_WIDTH // group):
        pg = project(g)
        for c in range(group // MXU_DIM):
            for part, qc in enumerate(norm_rotary(pg[:, c * MXU_DIM:(c + 1) * MXU_DIM], qg_ref[...])):
                at = g * group + c * MXU_DIM + part * LANES
                q_ref[:, at:at + LANES] = (qc * scale).astype(_BF16)
    pg = project(ATTN_WIDTH // group)
    for part, kc in enumerate(norm_rotary(pg[:, :KV_WIDTH], kg_ref[...])):
        ka, kb = dup_heads(kc)
        k_ref[:, 2 * part * LANES:(2 * part + 1) * LANES] = ka.astype(_BF16)
        k_ref[:, (2 * part + 1) * LANES:(2 * part + 2) * LANES] = kb.astype(_BF16)
    for part in range(KV_WIDTH // LANES):
        va, vb = dup_heads(pg[:, KV_WIDTH + part * LANES:KV_WIDTH + (part + 1) * LANES])
        v_ref[:, 2 * part * LANES:(2 * part + 1) * LANES] = va.astype(_BF16)
        v_ref[:, (2 * part + 1) * LANES:(2 * part + 2) * LANES] = vb.astype(_BF16)
    first_u = (ATTN_WIDTH + 2 * KV_WIDTH) // group
    per_group = group // LANES
    for g in range(SSM_WIDTH // group):
        pg = project(first_u + g)
        for s in range(per_group):
            us_ref[g * per_group + s] = pg[:, s * LANES:(s + 1) * LANES]
        for t in range(SSM_CHUNK):
            for s in range(g * per_group, (g + 1) * per_group):
                rows = us_ref[s, pl.ds(t, tm // SSM_CHUNK, stride=SSM_CHUNK), :]
                u_ref[t, :, s * LANES:(s + 1) * LANES] = rows.astype(_BF16)


def _proj(x, gain, w_in, pos, invf, q_gain, k_gain, *, tm=512):
    L, D = x.shape
    C = w_in.shape[1]
    quarters = LANES // (HEAD_DIM // 2)
    pos_f = pos.astype(_F32).reshape(L // tm, quarters, tm // quarters).transpose(0, 2, 1)
    pos_f = jnp.repeat(pos_f, HEAD_DIM // 2, axis=2)
    head_of_lane = jnp.arange(MXU_DIM) // HEAD_DIM
    pn = jnp.where(head_of_lane[:, None] == head_of_lane[None, :], 1.0 / HEAD_DIM, 0.0).astype(_BF16)
    qg = jnp.tile(q_gain, LANES // HEAD_DIM).reshape(1, LANES)
    kg = jnp.tile(k_gain, LANES // HEAD_DIM).reshape(1, LANES)
    row = lambda i: (i, 0)
    fixed = lambda i: (0, 0)
    return pl.pallas_call(
        _proj_body,
        grid=(L // tm,),
        in_specs=[
            pl.BlockSpec((tm, D), row),
            pl.BlockSpec((1, D), fixed),
            pl.BlockSpec((D, C), fixed),
            pl.BlockSpec((None, tm // quarters, LANES), lambda i: (i, 0, 0)),
            pl.BlockSpec((1, LANES), fixed),
            pl.BlockSpec((1, LANES), fixed),
            pl.BlockSpec((1, LANES), fixed),
            pl.BlockSpec((MXU_DIM, MXU_DIM), fixed),
        ],
        out_specs=[
            pl.BlockSpec((tm, ATTN_WIDTH), row),
            pl.BlockSpec((tm, 2 * KV_WIDTH), row),
            pl.BlockSpec((tm, 2 * KV_WIDTH), row),
            pl.BlockSpec((SSM_CHUNK, tm // SSM_CHUNK, SSM_WIDTH), lambda i: (0, i, 0)),
        ],
        out_shape=[
            jax.ShapeDtypeStruct((L, ATTN_WIDTH), _BF16),
            jax.ShapeDtypeStruct((L, 2 * KV_WIDTH), _BF16),
            jax.ShapeDtypeStruct((L, 2 * KV_WIDTH), _BF16),
            jax.ShapeDtypeStruct((SSM_CHUNK, L // SSM_CHUNK, SSM_WIDTH), _BF16),
        ],
        scratch_shapes=[pltpu.VMEM((SSM_WIDTH // LANES, tm, LANES), _F32)],
        compiler_params=pltpu.CompilerParams(
            dimension_semantics=("arbitrary",), vmem_limit_bytes=VMEM_LIMIT),
        name="proj",
    )(x, gain.reshape(1, D), w_in, pos_f, invf, qg, kg, pn)


def _attn_body(n_cast, sink_ref, q_ref, kc_ref, vc_ref, kp_ref, vp_ref, *refs):
    o_ref = refs[n_cast]
    for src, dst in zip(refs[:n_cast], refs[n_cast + 1:]):
        _cast_block(src, dst)
    tq = q_ref.shape[0]
    kj = lax.broadcasted_iota(jnp.int32, (BLOCK, BLOCK), 0)
    qi = lax.broadcasted_iota(jnp.int32, (BLOCK, BLOCK), 1)
    from_prev = kj > qi
    has_prev = pl.program_id(0) > 0
    kv_lane = lax.broadcasted_iota(jnp.int32, (2 * BLOCK, LANES), 1)
    low_head = kv_lane < HEAD_DIM
    ones_at = ((kv_lane == HEAD_DIM).astype(_BF16), (kv_lane == 0).astype(_BF16))
    low_dims = lax.broadcasted_iota(jnp.int32, (LANES, BLOCK), 0) < HEAD_DIM
    nt = (((1,), (1,)), ((), ()))
    tn = (((0,), (0,)), ((), ()))

    for b in range(tq // BLOCK):
        rows = slice(b * BLOCK, (b + 1) * BLOCK)
        for hk in range(N_KV_HEADS):
            cols = slice(hk * LANES, (hk + 1) * LANES)
            if b == 0:
                kd = jnp.concatenate([kp_ref[:, cols], kc_ref[0:BLOCK, cols]], axis=0)
                vd = jnp.concatenate([vp_ref[:, cols], vc_ref[0:BLOCK, cols]], axis=0)
            else:
                kd = kc_ref[(b - 1) * BLOCK:(b + 1) * BLOCK, cols]
                vd = vc_ref[(b - 1) * BLOCK:(b + 1) * BLOCK, cols]
            zero = jnp.zeros_like(kd)
            k_half = (jnp.where(low_head, kd, zero), jnp.where(low_head, zero, kd))
            v_half = (jnp.where(low_head, vd, ones_at[0]), jnp.where(low_head, ones_at[1], vd))
            for pp in range(2):
                pair = hk * 2 + pp
                qp = q_ref[rows, pair * LANES:(pair + 1) * LANES]
                outs = []
                for half in range(2):
                    sink = sink_ref[pair * 2 + half]
                    s2 = lax.dot_general(k_half[half], qp, nt, preferred_element_type=_F32)
                    s_prev = s2[:BLOCK]
                    if b == 0:
                        s_prev = jnp.where(has_prev, s_prev, -jnp.inf)
                    s = jnp.where(from_prev, s_prev, s2[BLOCK:])
                    m = jnp.maximum(jnp.max(s, axis=0, keepdims=True), sink)
                    pb = jnp.exp(s - m).astype(_BF16)
                    pz = jnp.zeros_like(pb)
                    p2 = jnp.concatenate([jnp.where(from_prev, pb, pz), jnp.where(from_prev, pz, pb)], axis=0)
                    o = lax.dot_general(v_half[half], p2, tn, preferred_element_type=_F32)
                    sum_row = HEAD_DIM if half == 0 else 0
                    den = o[sum_row:sum_row + 1, :] + jnp.exp(sink - m)
                    outs.append(o * (1.0 / den))
                acc = jnp.where(low_dims, outs[0], outs[1])
                o_ref[rows, pair * LANES:(pair + 1) * LANES] = jnp.transpose(acc).astype(_BF16)


def _attn(q, k2, v2, sinks, to_cast=(), cast_col_block=None, *, tq=1024):
    L = q.shape[0]
    per = tq // BLOCK
    row = lambda i: (i, 0)
    prev = lambda i: (jnp.maximum(i * per - 1, 0), 0)
    cast_specs, cast_out_specs, cast_shapes = _row_cast_specs(to_cast, L // tq, cast_col_block)
    out = pl.pallas_call(
        functools.partial(_attn_body, len(to_cast)),
        grid=(L // tq,),
        in_specs=[
            pl.BlockSpec(memory_space=pltpu.SMEM),
            pl.BlockSpec((tq, ATTN_WIDTH), row),
            pl.BlockSpec((tq, 2 * KV_WIDTH), row),
            pl.BlockSpec((tq, 2 * KV_WIDTH), row),
            pl.BlockSpec((BLOCK, 2 * KV_WIDTH), prev),
            pl.BlockSpec((BLOCK, 2 * KV_WIDTH), prev),
        ] + cast_specs,
        out_specs=[pl.BlockSpec((tq, ATTN_WIDTH), row)] + cast_out_specs,
        out_shape=[jax.ShapeDtypeStruct((L, ATTN_WIDTH), _BF16)] + cast_shapes,
        compiler_params=pltpu.CompilerParams(
            dimension_semantics=("arbitrary",), vmem_limit_bytes=VMEM_LIMIT),
        name="attn",
    )(sinks, q, k2, v2, k2, v2, *to_cast)
    return out[0], out[1:]


def _ssm_body(u_ref, wst_ref, kw_ref, cp_ref, wg_ref, bg_ref, a_ref, aseg_ref, o_ref,
              lhs_ref, toep_ref, s_ref, xb_ref):
    T = SSM_CHUNK
    n_c = u_ref.shape[1]
    seg = n_c // SSM_SEGMENTS
    pitch = s_ref.shape[1] // SSM_SEGMENTS
    n_state_slabs = SLAB_STATES // LANES
    pair_w = 2 * LANES
    out_w = 2 * MXU_DIM

    for t in range(T):
        lhs_ref[:, t * LANES:(t + 1) * LANES] = u_ref[t]

    toep_ref[...] = jnp.zeros(toep_ref.shape, _BF16)
    for t in range(T):
        for tp in range(t, T):
            toep_ref[t * LANES:(t + 1) * LANES, tp * LANES:(tp + 1) * LANES] = (
                kw_ref[:, (tp - t) * LANES:(tp - t + 1) * LANES])

    per_dot = out_w // LANES
    for nb in range(2 * n_state_slabs // per_dot):
        res = jnp.dot(lhs_ref[...], wst_ref[:, nb * out_w:(nb + 1) * out_w], preferred_element_type=_F32)
        for part in range(per_dot):
            for j in range(SSM_SEGMENTS):
                s_ref[per_dot * nb + part, j * pitch:j * pitch + seg, :] = (
                    res[j * seg:(j + 1) * seg, part * LANES:(part + 1) * LANES])

    shape = (SSM_SEGMENTS, LANES)
    a_re = [jnp.broadcast_to(a_ref[0:1, k * LANES:(k + 1) * LANES], shape) for k in range(n_state_slabs)]
    a_im = [jnp.broadcast_to(a_ref[1:2, k * LANES:(k + 1) * LANES], shape) for k in range(n_state_slabs)]
    g_re = [jnp.broadcast_to(aseg_ref[0:1, k * LANES:(k + 1) * LANES], shape) for k in range(n_state_slabs)]
    g_im = [jnp.broadcast_to(aseg_ref[1:2, k * LANES:(k + 1) * LANES], shape) for k in range(n_state_slabs)]

    def seg_rows(w):
        return pl.ds(w, SSM_SEGMENTS, stride=pitch)

    def advance(w, carry, store):
        out = []
        for k in range(n_state_slabs):
            z_re, z_im = carry[2 * k], carry[2 * k + 1]
            s_re = s_ref[k, seg_rows(w), :]
            s_im = s_ref[n_state_slabs + k, seg_rows(w), :]
            if store:
                s_ref[k, seg_rows(w), :] = z_re
                s_ref[n_state_slabs + k, seg_rows(w), :] = z_im
            out.append(a_re[k] * z_re - a_im[k] * z_im + s_re)
            out.append(a_re[k] * z_im + a_im[k] * z_re + s_im)
        return tuple(out)

    zero = jnp.zeros(shape, _F32)
    ends = lax.fori_loop(0, seg, lambda w, c: advance(w, c, False), (zero,) * (2 * n_state_slabs),
                         unroll=SCAN_UNROLL)

    segidx = lax.broadcasted_iota(jnp.int32, shape, 0)
    init = []
    for k in range(n_state_slabs):
        i_re, i_im = zero, zero
        f_re, f_im = ends[2 * k], ends[2 * k + 1]
        for j in range(SSM_SEGMENTS - 1):
            c_re = g_re[k] * i_re - g_im[k] * i_im + f_re
            c_im = g_re[k] * i_im + g_im[k] * i_re + f_im
            i_re = jnp.where(segidx == j + 1, pltpu.roll(c_re, 1, 0), i_re)
            i_im = jnp.where(segidx == j + 1, pltpu.roll(c_im, 1, 0), i_im)
        init += [i_re, i_im]

    lax.fori_loop(0, seg, lambda w, c: advance(w, c, True), tuple(init), unroll=SCAN_UNROLL)

    for col in range(2 * n_state_slabs):
        for j in range(SSM_SEGMENTS):
            xb_ref[j * seg:(j + 1) * seg, col * LANES:(col + 1) * LANES] = (
                s_ref[col, j * pitch:j * pitch + seg, :].astype(_BF16))

    steps = out_w // LANES
    for i in range(T // steps):
        kk = (i + 1) * out_w
        cols = slice(i * out_w, (i + 1) * out_w)
        y = (jnp.dot(lhs_ref[:, :kk], toep_ref[:kk, cols], preferred_element_type=_F32)
             + jnp.dot(xb_ref[...], cp_ref[:, cols], preferred_element_type=_F32))
        z = jax.nn.gelu(y, approximate=True).astype(_BF16)
        for pp in range(steps // 2):
            gt = jnp.dot(z[:, pp * pair_w:(pp + 1) * pair_w], wg_ref[...],
                         preferred_element_type=_F32) + bg_ref[...]
            out = gt[:, :pair_w] * jax.nn.sigmoid(gt[:, pair_w:])
            t0 = i * steps + 2 * pp
            o_ref[t0] = out[:, :LANES].astype(_BF16)
            o_ref[t0 + 1] = out[:, LANES:].astype(_BF16)


def _complex_power(re, im, n):
    out_re, out_im = None, None
    while n:
        if n & 1:
            if out_re is None:
                out_re, out_im = re, im
            else:
                out_re, out_im = out_re * re - out_im * im, out_re * im + out_im * re
        n >>= 1
        if n:
            re, im = re * re - im * im, 2.0 * re * im
    return out_re, out_im


def _ssm_prep_body(seg, rows_ref, bt_ref, ct_ref, d_ref, wt_ref, bglu_ref,
                   wst_ref, kw_ref, cp_ref, wg_ref, bg_ref, ach_ref, aseg_ref):
    T, H = SSM_CHUNK, SSM_GROUP
    a_re, a_im = rows_ref[0:1, :], rows_ref[1:2, :]
    dt = jnp.exp(rows_ref[2:3, :])
    mag = jnp.exp(a_re * dt)
    ab_re = mag * jnp.cos(a_im * dt)
    ab_im = mag * jnp.sin(a_im * dt)
    inv_den = 1.0 / (a_re * a_re + a_im * a_im)
    nr = ab_re - 1.0
    coef_re = (nr * a_re + ab_im * a_im) * inv_den
    coef_im = (ab_im * a_re - nr * a_im) * inv_den

    def block_diag(t16, shape):
        row_g = lax.broadcasted_iota(jnp.int32, shape, 0) // H
        col_g = lax.broadcasted_iota(jnp.int32, shape, 1) // (shape[1] // SLAB_GROUPS)
        return jnp.where(row_g == col_g, jnp.concatenate([t16] * SLAB_GROUPS, axis=0), 0.0)

    cs = (LANES, SLAB_STATES)
    b_re, b_im = block_diag(bt_ref[0], cs), block_diag(bt_ref[1], cs)
    bb_re = coef_re * b_re - coef_im * b_im
    bb_im = coef_re * b_im + coef_im * b_re
    ct_re, ct_im = block_diag(ct_ref[0], cs), block_diag(ct_ref[1], cs)

    pw = [(jnp.ones_like(ab_re), jnp.zeros_like(ab_im))]
    for _ in range(T):
        pr, pi = pw[-1]
        pw.append((pr * ab_re - pi * ab_im, pr * ab_im + pi * ab_re))

    for t in range(T):
        pr, pi = pw[T - 1 - t]
        wst_ref[t * LANES:(t + 1) * LANES, :SLAB_STATES] = (bb_re * pr - bb_im * pi).astype(_BF16)
        wst_ref[t * LANES:(t + 1) * LANES, SLAB_STATES:] = (bb_re * pi + bb_im * pr).astype(_BF16)

    lags = []
    for k in range(T + 1):
        if k < T:
            lags.append(jnp.concatenate([ct_re, ct_im], axis=1))
        if k > 0:
            cp_ref[:SLAB_STATES, (k - 1) * LANES:k * LANES] = jnp.transpose(ct_re).astype(_BF16)
            cp_ref[SLAB_STATES:, (k - 1) * LANES:k * LANES] = jnp.transpose(-ct_im).astype(_BF16)
        ct_re, ct_im = ct_re * ab_re - ct_im * ab_im, ct_re * ab_im + ct_im * ab_re

    def split(a):
        hi = a.astype(_BF16)
        return hi, (a - hi.astype(_F32)).astype(_BF16)

    nt = (((1,), (1,)), ((), ()))
    a_hi, a_lo = split(jnp.concatenate(lags, axis=0))
    b_hi, b_lo = split(jnp.concatenate([bb_re, -bb_im], axis=1))
    kern_t = (lax.dot_general(a_hi, b_hi, nt, preferred_element_type=_F32)
              + lax.dot_general(a_hi, b_lo, nt, preferred_element_type=_F32)
              + lax.dot_general(a_lo, b_hi, nt, preferred_element_type=_F32))
    kern = jnp.transpose(kern_t)
    eye = (lax.broadcasted_iota(jnp.int32, (LANES, LANES), 0)
           == lax.broadcasted_iota(jnp.int32, (LANES, LANES), 1))
    kw_ref[:, :LANES] = (kern[:, :LANES] + jnp.where(eye, d_ref[...], 0.0)).astype(_BF16)
    kw_ref[:, LANES:] = kern[:, LANES:].astype(_BF16)

    cc = (LANES, LANES)
    w_lin = jnp.transpose(block_diag(wt_ref[0], cc)).astype(_BF16)
    w_gate = jnp.transpose(block_diag(wt_ref[1], cc)).astype(_BF16)
    wg_ref[...] = jnp.zeros(wg_ref.shape, _BF16)
    for t in range(2):
        wg_ref[t * LANES:(t + 1) * LANES, t * LANES:(t + 1) * LANES] = w_lin
        wg_ref[t * LANES:(t + 1) * LANES, (2 + t) * LANES:(3 + t) * LANES] = w_gate
    bg_ref[...] = jnp.concatenate([bglu_ref[0:1, :], bglu_ref[0:1, :], bglu_ref[1:2, :], bglu_ref[1:2, :]],
                                  axis=1)

    ach_ref[0:1, :], ach_ref[1:2, :] = pw[T]
    aseg_ref[0:1, :], aseg_ref[1:2, :] = _complex_power(pw[T][0], pw[T][1], seg)


def _ssm_weights(log_dt, a_re, a_im, b_re, b_im, c_re, c_im, d_skip, w_glu, b_glu, seg):
    P, H, T = SSM_STATE, SSM_GROUP, SSM_CHUNK
    SG, NS = SLAB_GROUPS, SSM_GROUPS // SLAB_GROUPS
    rows = jnp.stack([a_re.reshape(NS, SG * P), a_im.reshape(NS, SG * P),
                      jnp.repeat(log_dt, P).reshape(NS, SG * P)], axis=1)

    def per_group_rows(w, lead):
        t = w.reshape((NS, SG) + w.shape[1:])
        t = jnp.moveaxis(t, 2 + lead, 1)
        return t.reshape(NS, t.shape[1], -1)

    bt = jnp.stack([per_group_rows(b_re, 1), per_group_rows(b_im, 1)], axis=1)
    ct = jnp.stack([per_group_rows(c_re, 0), per_group_rows(c_im, 0)], axis=1)
    wt = jnp.stack([per_group_rows(w_glu[..., :H], 1), per_group_rows(w_glu[..., H:], 1)], axis=1)
    bglu = jnp.stack([b_glu[:, :H].reshape(NS, SG * H), b_glu[:, H:].reshape(NS, SG * H)], axis=1)
    d = d_skip.reshape(NS, 1, SG * H)

    blk3 = lambda i: (i, 0, 0)
    blk4 = lambda i: (i, 0, 0, 0)
    return pl.pallas_call(
        functools.partial(_ssm_prep_body, seg),
        grid=(NS,),
        in_specs=[
            pl.BlockSpec((None, 3, SG * P), blk3),
            pl.BlockSpec((None, 2, H, SG * P), blk4),
            pl.BlockSpec((None, 2, H, SG * P), blk4),
            pl.BlockSpec((None, 1, SG * H), blk3),
            pl.BlockSpec((None, 2, H, SG * H), blk4),
            pl.BlockSpec((None, 2, SG * H), blk3),
        ],
        out_specs=[
            pl.BlockSpec((None, T * LANES, 2 * SLAB_STATES), blk3),
            pl.BlockSpec((None, LANES, T * LANES), blk3),
            pl.BlockSpec((None, 2 * SLAB_STATES, T * LANES), blk3),
            pl.BlockSpec((None, 2 * LANES, 4 * LANES), blk3),
            pl.BlockSpec((None, 1, 4 * LANES), blk3),
            pl.BlockSpec((None, 2, SLAB_STATES), blk3),
            pl.BlockSpec((None, 2, SLAB_STATES), blk3),
        ],
        out_shape=[
            jax.ShapeDtypeStruct((NS, T * LANES, 2 * SLAB_STATES), _BF16),
            jax.ShapeDtypeStruct((NS, LANES, T * LANES), _BF16),
            jax.ShapeDtypeStruct((NS, 2 * SLAB_STATES, T * LANES), _BF16),
            jax.ShapeDtypeStruct((NS, 2 * LANES, 4 * LANES), _BF16),
            jax.ShapeDtypeStruct((NS, 1, 4 * LANES), _F32),
            jax.ShapeDtypeStruct((NS, 2, SLAB_STATES), _F32),
            jax.ShapeDtypeStruct((NS, 2, SLAB_STATES), _F32),
        ],
        compiler_params=pltpu.CompilerParams(
            dimension_semantics=("arbitrary",), vmem_limit_bytes=VMEM_LIMIT),
        name="ssm_prep",
    )(rows, bt, ct, d, wt, bglu)


def _ssm(u_t, weights):
    T, n_c, W = u_t.shape
    seg = n_c // SSM_SEGMENTS
    pitch = seg + SUBLANES
    ns = W // LANES
    wst, kw, cp, wg, bg, a_chunk, a_seg = weights
    slab = lambda i: (0, 0, i)
    blk = lambda i: (i, 0, 0)
    return pl.pallas_call(
        _ssm_body,
        grid=(ns,),
        in_specs=[
            pl.BlockSpec((T, n_c, LANES), slab),
            pl.BlockSpec((None, T * LANES, 2 * SLAB_STATES), blk),
            pl.BlockSpec((None, LANES, T * LANES), blk),
            pl.BlockSpec((None, 2 * SLAB_STATES, T * LANES), blk),
            pl.BlockSpec((None, 2 * LANES, 4 * LANES), blk),
            pl.BlockSpec((None, 1, 4 * LANES), blk),
            pl.BlockSpec((None, 2, SLAB_STATES), blk),
            pl.BlockSpec((None, 2, SLAB_STATES), blk),
        ],
        out_specs=pl.BlockSpec((T, n_c, LANES), slab),
        out_shape=jax.ShapeDtypeStruct((T, n_c, W), _BF16),
        scratch_shapes=[
            pltpu.VMEM((n_c, T * LANES), _BF16),
            pltpu.VMEM((T * LANES, T * LANES), _BF16),
            pltpu.VMEM((2 * SLAB_STATES // LANES, SSM_SEGMENTS * pitch, LANES), _F32),
            pltpu.VMEM((n_c, 2 * SLAB_STATES), _BF16),
        ],
        compiler_params=pltpu.CompilerParams(
            dimension_semantics=("arbitrary",), vmem_limit_bytes=VMEM_LIMIT),
        name="ssm",
    )(u_t, wst, kw, cp, wg, bg, a_chunk, a_seg)


def _outproj_body(n_cast, a_ref, s_ref, x_ref, ag_ref, sg_ref, w_ref, *refs):
    o_ref, sn_ref = refs[n_cast], refs[-1]
    for src, dst in zip(refs[:n_cast], refs[n_cast + 1:-1]):
        dst[...] = src[...].astype(_BF16)
    tm = x_ref.shape[0]
    an = _rms(a_ref[...].astype(_F32), ag_ref[...]).astype(_BF16)
    for t in range(SSM_CHUNK):
        sn_t = _rms(s_ref[t].astype(_F32), sg_ref[...])
        for s in range(SSM_WIDTH // LANES):
            sn_ref[s, pl.ds(t, tm // SSM_CHUNK, stride=SSM_CHUNK), :] = sn_t[:, s * LANES:(s + 1) * LANES]
    sn = jnp.concatenate([sn_ref[s] for s in range(SSM_WIDTH // LANES)], axis=1).astype(_BF16)
    mixed = jnp.concatenate([an, sn], axis=1)
    o_ref[...] = x_ref[...] + jnp.dot(mixed, w_ref[...], preferred_element_type=_F32)


def _outproj(attn, ssm_t, x, a_gain, s_gain, w_out, to_cast=(), *, tm=512):
    L, D = x.shape
    row = lambda i: (i, 0)
    fixed = lambda i: (0, 0)
    cast_specs, _, cast_shapes = _row_cast_specs(to_cast, L // tm)
    out = pl.pallas_call(
        functools.partial(_outproj_body, len(to_cast)),
        grid=(L // tm,),
        in_specs=[
            pl.BlockSpec((tm, ATTN_WIDTH), row),
            pl.BlockSpec((SSM_CHUNK, tm // SSM_CHUNK, SSM_WIDTH), lambda i: (0, i, 0)),
            pl.BlockSpec((tm, D), row),
            pl.BlockSpec((1, ATTN_WIDTH), fixed),
            pl.BlockSpec((1, SSM_WIDTH), fixed),
            pl.BlockSpec((ATTN_WIDTH + SSM_WIDTH, D), fixed),
        ] + cast_specs,
        out_specs=[pl.BlockSpec((tm, D), row)] + cast_specs,
        out_shape=[jax.ShapeDtypeStruct((L, D), _F32)] + cast_shapes,
        scratch_shapes=[pltpu.VMEM((SSM_WIDTH // LANES, tm, LANES), _F32)],
        compiler_params=pltpu.CompilerParams(
            dimension_semantics=("arbitrary",), vmem_limit_bytes=VMEM_LIMIT),
        name="outproj",
    )(attn, ssm_t, x, a_gain.reshape(1, -1), s_gain.reshape(1, -1), w_out, *to_cast)
    return out[0], out[1:]


def _layer(x, pos, invf, p):
    L = x.shape[0]
    seg = L // (SSM_SEGMENTS * SSM_CHUNK)
    w_gate1, w_up1 = _block_major_cast([p['ffn1_w_gate'], p['ffn1_w_up']])
    act, (w_down1,) = _ffn_up(x, p['ffn1_norm'], w_gate1, w_up1, [p['ffn1_w_down']])
    x, (w_in, w_out) = _ffn_down(act, w_down1, x, [p['w_in'], p['w_out']])
    q, k2, v2, u_t = _proj(x, p['mix_norm'], w_in, pos, invf, p['q_norm'], p['k_norm'])
    attn, (w_gate2, w_up2) = _attn(q, k2, v2, p['attn_sinks'], [p['ffn2_w_gate'], p['ffn2_w_up']], FFN_TF)
    weights = _ssm_weights(p['ssm_log_dt'], p['ssm_a_re'], p['ssm_a_im'], p['ssm_b_re'], p['ssm_b_im'],
                           p['ssm_c_re'], p['ssm_c_im'], p['ssm_d'], p['ssm_w_glu'], p['ssm_b_glu'], seg)
    ssm_t = _ssm(u_t, weights)
    x, (w_down2,) = _outproj(attn, ssm_t, x, p['attn_out_norm'], p['ssm_out_norm'], w_out,
                             [p['ffn2_w_down']])
    act, _ = _ffn_up(x, p['ffn2_norm'], w_gate2, w_up2)
    return _ffn_down(act, w_down2, x)[0]


def kernel(x, positions, ffn1_norm, ffn1_w_gate, ffn1_w_up, ffn1_w_down, mix_norm, w_in, q_norm, k_norm,
           attn_sinks, ssm_log_dt, ssm_a_re, ssm_a_im, ssm_b_re, ssm_b_im, ssm_c_re, ssm_c_im, ssm_d,
           ssm_w_glu, ssm_b_glu, attn_out_norm, ssm_out_norm, w_out, ffn2_norm, ffn2_w_gate, ffn2_w_up,
           ffn2_w_down):
    params = dict(
        ffn1_norm=ffn1_norm, ffn1_w_gate=ffn1_w_gate, ffn1_w_up=ffn1_w_up, ffn1_w_down=ffn1_w_down,
        mix_norm=mix_norm, w_in=w_in, q_norm=q_norm, k_norm=k_norm, attn_sinks=attn_sinks,
        ssm_log_dt=ssm_log_dt, ssm_a_re=ssm_a_re, ssm_a_im=ssm_a_im, ssm_b_re=ssm_b_re, ssm_b_im=ssm_b_im,
        ssm_c_re=ssm_c_re, ssm_c_im=ssm_c_im, ssm_d=ssm_d, ssm_w_glu=ssm_w_glu, ssm_b_glu=ssm_b_glu,
        attn_out_norm=attn_out_norm, ssm_out_norm=ssm_out_norm, w_out=w_out,
        ffn2_norm=ffn2_norm, ffn2_w_gate=ffn2_w_gate, ffn2_w_up=ffn2_w_up, ffn2_w_down=ffn2_w_down)
    depth = ffn1_norm.shape[0]
    half = HEAD_DIM // 2
    inv_freq = ROPE_THETA ** (-jnp.arange(half, dtype=_F32) * 2.0 / HEAD_DIM)
    invf = jnp.tile(inv_freq, LANES // half).reshape(1, LANES)
    outs = []
    for b in range(x.shape[0]):
        xb = x[b]
        for i in range(depth):
            xb = _layer(xb, positions[b], invf, {name: val[i] for name, val in params.items()})
        outs.append(xb)
    return jnp.stack(outs, axis=0)
```
